```python
import math
import jax, jax.numpy as jnp
from jax import lax
import numpy as np

D_MODEL = 1024
BATCH = 1
SEQ = 16384
DEPTH = 4

N_MIXERS = 2
ROPE_THETA = 500000.0
LN_EPS = 1e-5
RMS_EPS = 1e-6

MLA_HEADS = 8
MLA_NOPE = 128
MLA_ROPE = 64
MLA_V = 128
MLA_Q_RANK = 384
MLA_KV_RANK = 256
MLA_QBLOCK = 128

SWA_HEADS = 16
SWA_KV_HEADS = 4
SWA_HEAD_DIM = 64
SWA_WINDOW = 128
SWA_ROT = SWA_HEAD_DIM // 4

D_FF = ((8 * D_MODEL + 3 * 256 - 1) // (3 * 256)) * 256

DEEPNORM_ALPHA = (2 * DEPTH) ** 0.25
DEEPNORM_BETA = (8 * DEPTH) ** -0.25

N_MLA = (DEPTH + 1) // 2
N_SWA = DEPTH // 2

kernel_name = "hybrid_mla_swa_sink_deepnorm_adaln"


def layer_norm(x, g, b):
    xf = x.astype(jnp.float32)
    mu = jnp.mean(xf, -1, keepdims=True)
    var = jnp.mean(jnp.square(xf - mu), -1, keepdims=True)
    return ((xf - mu) * lax.rsqrt(var + LN_EPS) * g + b).astype(x.dtype)


def rms_norm(x, g):
    xf = x.astype(jnp.float32)
    return (xf * lax.rsqrt(jnp.mean(jnp.square(xf), -1, keepdims=True) + RMS_EPS) * g).astype(x.dtype)


def rope_cos_sin(positions, rot_dim):
    inv = ROPE_THETA ** (-jnp.arange(0, rot_dim, 2, dtype=jnp.float32) / rot_dim)
    ang = positions.astype(jnp.float32)[..., None] * inv
    return jnp.cos(ang), jnp.sin(ang)


def apply_rope(x, cos, sin):
    half = x.shape[-1] // 2
    x1, x2 = x[..., :half], x[..., half:]
    c = cos[:, :, None, :]
    s = sin[:, :, None, :]
    return jnp.concatenate([x1 * c - x2 * s, x2 * c + x1 * s], -1).astype(x.dtype)


def mla_mixer(h, cos, sin, w_in, q_norm, w_q_b, kv_norm, w_kv_b, w_o):
    B, S, _ = h.shape
    H = MLA_HEADS
    lat = h @ w_in
    q_lat = lat[..., :MLA_Q_RANK]
    kv_lat = lat[..., MLA_Q_RANK:MLA_Q_RANK + MLA_KV_RANK]
    k_rope = lat[..., MLA_Q_RANK + MLA_KV_RANK:]
    q = (rms_norm(q_lat, q_norm) @ w_q_b).reshape(B, S, H, MLA_NOPE + MLA_ROPE)
    q_nope = q[..., :MLA_NOPE]
    q_rope = apply_rope(q[..., MLA_NOPE:], cos, sin)
    k_rope = apply_rope(k_rope[:, :, None, :], cos, sin)[:, :, 0, :]
    kv = (rms_norm(kv_lat, kv_norm) @ w_kv_b).reshape(B, S, H, MLA_NOPE + MLA_V)
    k_nope = kv[..., :MLA_NOPE]
    v = kv[..., MLA_NOPE:]
    scale = (MLA_NOPE + MLA_ROPE) ** -0.5
    nb = S // MLA_QBLOCK
    qn_b = q_nope.reshape(B, nb, MLA_QBLOCK, H, MLA_NOPE).transpose(1, 0, 2, 3, 4)
    qr_b = q_rope.reshape(B, nb, MLA_QBLOCK, H, MLA_ROPE).transpose(1, 0, 2, 3, 4)
    starts = jnp.arange(nb, dtype=jnp.int32) * MLA_QBLOCK
    k_idx = jnp.arange(S, dtype=jnp.int32)

    def q_block(args):
        qn, qr, start = args
        s = (jnp.einsum('bqhd,bkhd->bhqk', qn, k_nope, preferred_element_type=jnp.float32)
             + jnp.einsum('bqhr,bkr->bhqk', qr, k_rope, preferred_element_type=jnp.float32)) * scale
        q_idx = start + jnp.arange(MLA_QBLOCK, dtype=jnp.int32)
        causal = k_idx[None, :] <= q_idx[:, None]
        s = jnp.where(causal[None, None], s, -jnp.inf)
        p = jax.nn.softmax(s, axis=-1).astype(v.dtype)
        return jnp.einsum('bhqk,bkhd->bqhd', p, v)

    o = lax.map(q_block, (qn_b, qr_b, starts))
    o = o.transpose(1, 0, 2, 3, 4).reshape(B, S, H * MLA_V)
    return o @ w_o


def partial_rope(x, cos, sin):
    return jnp.concatenate([apply_rope(x[..., :SWA_ROT], cos, sin), x[..., SWA_ROT:]], -1)


def swa_mixer(h, cos, sin, w_qkv, b_qkv, sinks, w_o, b_o):
    B, S, _ = h.shape
    HQ, HKV, HD, W = SWA_HEADS, SWA_KV_HEADS, SWA_HEAD_DIM, SWA_WINDOW
    G = HQ // HKV
    qkv = h @ w_qkv + b_qkv
    q = qkv[..., :HQ * HD].reshape(B, S, HQ, HD)
    k = qkv[..., HQ * HD:(HQ + HKV) * HD].reshape(B, S, HKV, HD)
    v = qkv[..., (HQ + HKV) * HD:].reshape(B, S, HKV, HD)
    q = partial_rope(q, cos, sin)
    k = partial_rope(k, cos, sin)
    nb = S // W
    qb = q.reshape(B, nb, W, HKV, G, HD)
    kb = k.reshape(B, nb, W, HKV, HD)
    vb = v.reshape(B, nb, W, HKV, HD)
    kpad = jnp.zeros_like(kb[:, :1])
    vpad = jnp.zeros_like(vb[:, :1])
    k2 = jnp.concatenate([jnp.concatenate([kpad, kb[:, :-1]], 1), kb], axis=2)
    v2 = jnp.concatenate([jnp.concatenate([vpad, vb[:, :-1]], 1), vb], axis=2)
    s = jnp.einsum('bnqhgd,bnkhd->bnhgqk', qb, k2, preferred_element_type=jnp.float32) * (HD ** -0.5)
    q_pos = jnp.arange(W, dtype=jnp.int32)[:, None] + W
    k_pos = jnp.arange(2 * W, dtype=jnp.int32)[None, :]
    rel = q_pos - k_pos
    band = (rel >= 0) & (rel < W)
    has_prev = (jnp.arange(nb)[:, None, None] > 0) | (k_pos[None] >= W)
    mask = band[None] & has_prev
    s = jnp.where(mask[None, :, None, None], s, -jnp.inf)
    sink = jnp.broadcast_to(sinks.astype(jnp.float32).reshape(HKV, G)[None, None, :, :, None, None],
                            s.shape[:-1] + (1,))
    p = jax.nn.softmax(jnp.concatenate([s, sink], -1), axis=-1)[..., :-1]
    o = jnp.einsum('bnhgqk,bnkhd->bnqhgd', p.astype(v.dtype), v2).reshape(B, S, HQ * HD)
    return o @ w_o + b_o


def swiglu(h, w_gate, w_up, w_down):
    return (jax.nn.silu(h @ w_gate) * (h @ w_up)) @ w_down


def setup_inputs(seed: int = 0) -> dict:
    key = jax.random.key(seed)
    ks = iter(jax.random.split(key, 32))
    D, F = D_MODEL, D_FF
    nrm = lambda shape, std: jax.random.normal(next(ks), shape, jnp.float32) * std
    offs = jax.random.randint(next(ks), (BATCH, 1), 0, 4096, dtype=jnp.int32)
    positions = offs + jnp.arange(SEQ, dtype=jnp.int32)[None, :]
    mla_in_w = MLA_Q_RANK + MLA_KV_RANK + MLA_ROPE
    swa_qkv_w = (SWA_HEADS + 2 * SWA_KV_HEADS) * SWA_HEAD_DIM
    return {
        "x": nrm((BATCH, SEQ, D), 1.0),
        "c": nrm((BATCH, D), 1.0),
        "positions": positions,
        "ada_w": nrm((DEPTH, D, 6 * D), 0.5 * D ** -0.5),
        "ada_b": nrm((DEPTH, 6 * D), 0.02),
        "ln_mix_g": 1.0 + nrm((DEPTH, D), 0.02),
        "ln_mix_b": nrm((DEPTH, D), 0.02),
        "ln_ffn_g": 1.0 + nrm((DEPTH, D), 0.02),
        "ln_ffn_b": nrm((DEPTH, D), 0.02),
        "ffn_w_gate": nrm((DEPTH, D, F), D ** -0.5),
        "ffn_w_up": nrm((DEPTH, D, F), D ** -0.5),
        "ffn_w_down": nrm((DEPTH, F, D), F ** -0.5 * DEEPNORM_BETA),
        "mla_w_in": nrm((N_MLA, D, mla_in_w), D ** -0.5),
        "mla_q_norm": 1.0 + nrm((N_MLA, MLA_Q_RANK), 0.02),
        "mla_w_q_b": nrm((N_MLA, MLA_Q_RANK, MLA_HEADS * (MLA_NOPE + MLA_ROPE)), MLA_Q_RANK ** -0.5),
        "mla_kv_norm": 1.0 + nrm((N_MLA, MLA_KV_RANK), 0.02),
        "mla_w_kv_b": nrm((N_MLA, MLA_KV_RANK, MLA_HEADS * (MLA_NOPE + MLA_V)), MLA_KV_RANK ** -0.5),
        "mla_w_o": nrm((N_MLA, MLA_HEADS * MLA_V, D), (MLA_HEADS * MLA_V) ** -0.5 * DEEPNORM_BETA),
        "swa_w_qkv": nrm((N_SWA, D, swa_qkv_w), D ** -0.5),
        "swa_b_qkv": nrm((N_SWA, swa_qkv_w), 0.02),
        "swa_sinks": nrm((N_SWA, SWA_HEADS), 1.0),
        "swa_w_o": nrm((N_SWA, SWA_HEADS * SWA_HEAD_DIM, D), (SWA_HEADS * SWA_HEAD_DIM) ** -0.5 * DEEPNORM_BETA),
        "swa_b_o": nrm((N_SWA, D), 0.02),
    }


def reference(x, c, positions, ada_w, ada_b, ln_mix_g, ln_mix_b, ln_ffn_g, ln_ffn_b,
              ffn_w_gate, ffn_w_up, ffn_w_down, mla_w_in, mla_q_norm, mla_w_q_b, mla_kv_norm,
              mla_w_kv_b, mla_w_o, swa_w_qkv, swa_b_qkv, swa_sinks, swa_w_o, swa_b_o):
    cos_a, sin_a = rope_cos_sin(positions, MLA_ROPE)
    cos_b, sin_b = rope_cos_sin(positions, SWA_ROT)
    cond = jax.nn.silu(c)
    for i in range(DEPTH):
        mod = (cond @ ada_w[i] + ada_b[i])[:, None, :]
        sh_m, sc_m, g_m, sh_f, sc_f, g_f = jnp.split(mod, 6, axis=-1)
        h = x * (1.0 + sc_m) + sh_m
        j = i // N_MIXERS
        if i % N_MIXERS == 0:
            y = mla_mixer(h, cos_a, sin_a, mla_w_in[j], mla_q_norm[j], mla_w_q_b[j],
                          mla_kv_norm[j], mla_w_kv_b[j], mla_w_o[j])
        else:
            y = swa_mixer(h, cos_b, sin_b, swa_w_qkv[j], swa_b_qkv[j], swa_sinks[j],
                          swa_w_o[j], swa_b_o[j])
        x = layer_norm(DEEPNORM_ALPHA * x + g_m * y, ln_mix_g[i], ln_mix_b[i])
        h = x * (1.0 + sc_f) + sh_f
        y = swiglu(h, ffn_w_gate[i], ffn_w_up[i], ffn_w_down[i])
        x = layer_norm(DEEPNORM_ALPHA * x + g_f * y, ln_ffn_g[i], ln_ffn_b[i])
    return x
```

```python
import functools

import jax
import jax.numpy as jnp
from jax import lax
from jax.experimental import pallas as pl
from jax.experimental.pallas import tpu as pltpu

D_MODEL = 1024
SEQ = 16384
DEPTH = 4
ROPE_THETA = 500000.0
LN_EPS = 1e-5
RMS_EPS = 1e-6

MLA_HEADS = 8
MLA_NOPE = 128
MLA_ROPE = 64
MLA_V = 128
MLA_Q_RANK = 384
MLA_KV_RANK = 256
MLA_QK_PAD = 256

SWA_HEADS = 16
SWA_KV_HEADS = 4
SWA_HEAD_DIM = 64
SWA_WINDOW = 128
SWA_ROT = SWA_HEAD_DIM // 4

D_FF = ((8 * D_MODEL + 3 * 256 - 1) // (3 * 256)) * 256
DEEPNORM_ALPHA = (2 * DEPTH) ** 0.25

LANES = 128
VMEM_LIMIT = 56 * 1024 * 1024

ROW_BLOCK = 512
MLA_TQ = 512
MLA_TK = 512
SWA_TQ = 512

BF16 = jnp.bfloat16
F32 = jnp.float32


def _dot(a, b):
    return jnp.dot(a, b, preferred_element_type=F32)


def _dot_nt(a, b):
    return lax.dot_general(a, b, (((1,), (1,)), ((), ())), preferred_element_type=F32)


def _layer_norm(z, g, b):
    mu = jnp.mean(z, axis=-1, keepdims=True)
    zc = z - mu
    var = jnp.mean(zc * zc, axis=-1, keepdims=True)
    return zc * lax.rsqrt(var + LN_EPS) * g + b


def _rms_norm(z, g):
    return z * lax.rsqrt(jnp.mean(z * z, axis=-1, keepdims=True) + RMS_EPS) * g


def _const_spec(shape):
    nd = len(shape)
    return pl.BlockSpec(shape, lambda *_: (0,) * nd, pipeline_mode=pl.Buffered(1))


def _params(n_grid):
    return pltpu.CompilerParams(
        dimension_semantics=("arbitrary",) * n_grid, vmem_limit_bytes=VMEM_LIMIT)


def _mod_kernel(c_ref, w_ref, b_ref, o_ref):
    c = c_ref[...]
    cond = (c * jax.nn.sigmoid(c)).astype(BF16)
    cond8 = jnp.broadcast_to(cond, (8, D_MODEL))
    y = _dot(cond8, w_ref[0].astype(BF16))
    o_ref[0] = y[0:1] + b_ref[0]


def _modulation(c, ada_w, ada_b):
    d = D_MODEL
    out = pl.pallas_call(
        _mod_kernel,
        out_shape=jax.ShapeDtypeStruct((DEPTH, 1, 6 * d), F32),
        grid=(DEPTH, 6),
        in_specs=[
            pl.BlockSpec((1, d), lambda i, k: (0, 0)),
            pl.BlockSpec((1, d, d), lambda i, k: (i, 0, k)),
            pl.BlockSpec((1, 1, d), lambda i, k: (i, 0, k)),
        ],
        out_specs=pl.BlockSpec((1, 1, d), lambda i, k: (i, 0, k)),
        compiler_params=_params(2),
        name="adaln_mod",
    )(c, ada_w, ada_b.reshape(DEPTH, 1, 6 * d))
    return out.reshape(DEPTH, 6, 1, d)


def _mla_proj_kernel(x_ref, sc_ref, sh_ref, w_in_ref, qn_ref, wq_ref, kvn_ref, wkt_ref, wv_ref,
                     cq_ref, sq_ref, ckt_ref, skt_ref, q_ref, kt_ref, v_ref):
    h = (x_ref[...] * (1.0 + sc_ref[...]) + sh_ref[...]).astype(BF16)
    lat = _dot(h, w_in_ref[...])
    q_lat = lat[:, :MLA_Q_RANK]
    kv_lat = lat[:, MLA_Q_RANK:MLA_Q_RANK + MLA_KV_RANK]
    kr = lat[:, MLA_Q_RANK + MLA_KV_RANK:]

    q = _dot(_rms_norm(q_lat, qn_ref[...]).astype(BF16), wq_ref[...])
    cq = cq_ref[...]
    sq = sq_ref[...]
    for hd in range(MLA_HEADS):
        base = hd * MLA_QK_PAD
        q_ref[:, base:base + LANES] = q[:, base:base + LANES].astype(BF16)
        xr = q[:, base + LANES:base + 2 * LANES]
        partner = pltpu.roll(xr, 32, axis=1) - pltpu.roll(xr, LANES - 32, axis=1)
        q_ref[:, base + LANES:base + 2 * LANES] = (xr * cq + partner * sq).astype(BF16)

    kvn = _rms_norm(kv_lat, kvn_ref[...]).astype(BF16)
    v_ref[...] = _dot(kvn, wv_ref[...]).astype(BF16)
    knt = _dot_nt(wkt_ref[...], kvn)
    krt = kr.T
    x1 = krt[0:32]
    x2 = krt[32:64]
    ck = ckt_ref[...]
    sk = skt_ref[...]
    tm = x_ref.shape[0]
    rope_rows = jnp.concatenate(
        [x1 * ck - x2 * sk, x2 * ck + x1 * sk, jnp.zeros((MLA_QK_PAD - MLA_NOPE - MLA_ROPE, tm), F32)],
        axis=0).astype(BF16)
    for hd in range(MLA_HEADS):
        kt_ref[hd, 0:MLA_NOPE, :] = knt[hd * MLA_NOPE:(hd + 1) * MLA_NOPE].astype(BF16)
        kt_ref[hd, MLA_NOPE:, :] = rope_rows


def _mla_proj(x, sc, sh, w, tabs):
    s, d = x.shape
    tm = ROW_BLOCK
    hq = MLA_HEADS * MLA_QK_PAD
    row = lambda i: (i, 0)
    return pl.pallas_call(
        _mla_proj_kernel,
        out_shape=(
            jax.ShapeDtypeStruct((s, hq), BF16),
            jax.ShapeDtypeStruct((MLA_HEADS, MLA_QK_PAD, s), BF16),
            jax.ShapeDtypeStruct((s, MLA_HEADS * MLA_V), BF16),
        ),
        grid=(s // tm,),
        in_specs=[
            pl.BlockSpec((tm, d), row),
            _const_spec((1, d)), _const_spec((1, d)),
            _const_spec(w["w_in"].shape), _const_spec(w["q_norm"].shape), _const_spec(w["wq"].shape),
            _const_spec(w["kv_norm"].shape), _const_spec(w["wkt"].shape), _const_spec(w["wv"].shape),
            pl.BlockSpec((tm, LANES), row), pl.BlockSpec((tm, LANES), row),
            pl.BlockSpec((32, tm), lambda i: (0, i)), pl.BlockSpec((32, tm), lambda i: (0, i)),
        ],
        out_specs=(
            pl.BlockSpec((tm, hq), row),
            pl.BlockSpec((MLA_HEADS, MLA_QK_PAD, tm), lambda i: (0, 0, i)),
            pl.BlockSpec((tm, MLA_HEADS * MLA_V), row),
        ),
        compiler_params=_params(1),
        name="mla_proj",
    )(x, sc, sh, w["w_in"], w["q_norm"], w["wq"], w["kv_norm"], w["wkt"], w["wv"],
      tabs["cq"], tabs["sq"], tabs["ckt"], tabs["skt"])


def _mla_attn_kernel(q_ref, kt_ref, v_ref, o_ref, m_ref, l_ref, acc_ref):
    qi = pl.program_id(1)
    tq, tk = MLA_TQ, MLA_TK
    scale = (MLA_NOPE + MLA_ROPE) ** -0.5
    q = q_ref[...]
    m_ref[...] = jnp.full(m_ref.shape, -jnp.inf, F32)
    l_ref[...] = jnp.zeros(l_ref.shape, F32)
    acc_ref[...] = jnp.zeros(acc_ref.shape, F32)

    def step(j, masked):
        k0 = pl.multiple_of(j * tk, tk)
        s = _dot(q, kt_ref[0, :, pl.ds(k0, tk)]) * scale
        if masked:
            row = lax.broadcasted_iota(jnp.int32, (tq, tk), 0) + qi * tq
            col = lax.broadcasted_iota(jnp.int32, (tq, tk), 1) + k0
            s = jnp.where(col <= row, s, -jnp.inf)
        m_prev = m_ref[...]
        m_new = jnp.maximum(m_prev, jnp.max(s, axis=-1, keepdims=True))
        alpha = jnp.exp(m_prev - m_new)
        p = jnp.exp(s - m_new)
        l_ref[...] = alpha * l_ref[...] + jnp.sum(p, axis=-1, keepdims=True)
        acc_ref[...] = alpha * acc_ref[...] + _dot(p.astype(BF16), v_ref[pl.ds(k0, tk), :])
        m_ref[...] = m_new

    n_full = qi * (tq // tk)

    def body(j, carry):
        step(j, False)
        return carry

    lax.fori_loop(0, n_full, body, 0)
    for d in range(tq // tk):
        step(n_full + d, True)
    o_ref[...] = (acc_ref[...] / l_ref[...]).astype(BF16)


def _mla_attn(q, kt, v):
    s = q.shape[0]
    tq = MLA_TQ
    return pl.pallas_call(
        _mla_attn_kernel,
        out_shape=jax.ShapeDtypeStruct((s, MLA_HEADS * MLA_V), BF16),
        grid=(MLA_HEADS, s // tq),
        in_specs=[
            pl.BlockSpec((tq, MLA_QK_PAD), lambda h, i: (i, h)),
            pl.BlockSpec((1, MLA_QK_PAD, s), lambda h, i: (h, 0, 0)),
            pl.BlockSpec((s, MLA_V), lambda h, i: (0, h)),
        ],
        out_specs=pl.BlockSpec((tq, MLA_V), lambda h, i: (i, h)),
        scratch_shapes=[
            pltpu.VMEM((tq, 1), F32), pltpu.VMEM((tq, 1), F32), pltpu.VMEM((tq, MLA_V), F32)],
        compiler_params=_params(2),
        name="mla_attn",
    )(q, kt, v)


def _out_ln_kernel(o_ref, x_ref, g_ref, w_ref, b_ref, lng_ref, lnb_ref, y_ref):
    y = _dot(o_ref[...], w_ref[...]) + b_ref[...]
    z = DEEPNORM_ALPHA * x_ref[...] + g_ref[...] * y
    y_ref[...] = _layer_norm(z, lng_ref[...], lnb_ref[...])


def _out_ln(o, x, gate, w_o, b_o, ln_g, ln_b):
    s, d = x.shape
    tm = ROW_BLOCK
    row = lambda i: (i, 0)
    return pl.pallas_call(
        _out_ln_kernel,
        out_shape=jax.ShapeDtypeStruct((s, d), F32),
        grid=(s // tm,),
        in_specs=[
            pl.BlockSpec((tm, o.shape[1]), row), pl.BlockSpec((tm, d), row),
            _const_spec((1, d)), _const_spec(w_o.shape), _const_spec((1, d)),
            _const_spec((1, d)), _const_spec((1, d)),
        ],
        out_specs=pl.BlockSpec((tm, d), row),
        compiler_params=_params(1),
        name="out_ln",
    )(o, x, gate, w_o, b_o, ln_g, ln_b)


def _ffn_kernel(x_ref, sc_ref, sh_ref, g_ref, wg_ref, wu_ref, wd_ref, lng_ref, lnb_ref, y_ref):
    x = x_ref[...]
    h = (x * (1.0 + sc_ref[...]) + sh_ref[...]).astype(BF16)
    gate = _dot(h, wg_ref[...])
    up = _dot(h, wu_ref[...])
    act = (gate * jax.nn.sigmoid(gate) * up).astype(BF16)
    y = _dot(act, wd_ref[...])
    z = DEEPNORM_ALPHA * x + g_ref[...] * y
    y_ref[...] = _layer_norm(z, lng_ref[...], lnb_ref[...])


def _ffn(x, sc, sh, gate, wg, wu, wd, ln_g, ln_b):
    s, d = x.shape
    tm = ROW_BLOCK
    row = lambda i: (i, 0)
    return pl.pallas_call(
        _ffn_kernel,
        out_shape=jax.ShapeDtypeStruct((s, d), F32),
        grid=(s // tm,),
        in_specs=[
            pl.BlockSpec((tm, d), row),
            _const_spec((1, d)), _const_spec((1, d)), _const_spec((1, d)),
            _const_spec(wg.shape), _const_spec(wu.shape), _const_spec(wd.shape),
            _const_spec((1, d)), _const_spec((1, d)),
        ],
        out_specs=pl.BlockSpec((tm, d), row),
        compiler_params=_params(1),
        name="ffn",
    )(x, sc, sh, gate, wg, wu, wd, ln_g, ln_b)


def _swa_proj_kernel(x_ref, sc_ref, sh_ref, w_ref, b_ref, c_ref, slo_ref, shi_ref, q_ref, k_ref, v_ref):
    h = (x_ref[...] * (1.0 + sc_ref[...]) + sh_ref[...]).astype(BF16)
    qkv = _dot(h, w_ref[...]) + b_ref[...]
    c = c_ref[...]
    slo = slo_ref[...]
    shi = shi_ref[...]
    nq = SWA_HEADS * SWA_HEAD_DIM
    nk = 2 * SWA_KV_HEADS * SWA_HEAD_DIM

    def rope(xc):
        return xc * c + pltpu.roll(xc, LANES - SWA_ROT // 2, axis=1) * slo + pltpu.roll(xc, SWA_ROT // 2, axis=1) * shi

    for j in range(nq // LANES):
        q_ref[:, j * LANES:(j + 1) * LANES] = rope(qkv[:, j * LANES:(j + 1) * LANES]).astype(BF16)
    for j in range(nk // LANES):
        k_ref[:, j * LANES:(j + 1) * LANES] = rope(qkv[:, nq + j * LANES:nq + (j + 1) * LANES]).astype(BF16)
    v_ref[...] = qkv[:, nq + nk:].astype(BF16)


def _swa_proj(x, sc, sh, w, tabs):
    s, d = x.shape
    tm = ROW_BLOCK
    nq = SWA_HEADS * SWA_HEAD_DIM
    nk = 2 * SWA_KV_HEADS * SWA_HEAD_DIM
    row = lambda i: (i, 0)
    return pl.pallas_call(
        _swa_proj_kernel,
        out_shape=(
            jax.ShapeDtypeStruct((s, nq), BF16),
            jax.ShapeDtypeStruct((s, nk), BF16),
            jax.ShapeDtypeStruct((s, nk), BF16),
        ),
        grid=(s // tm,),
        in_specs=[
            pl.BlockSpec((tm, d), row),
            _const_spec((1, d)), _const_spec((1, d)),
            _const_spec(w["w_qkv"].shape), _const_spec(w["b_qkv"].shape),
            pl.BlockSpec((tm, LANES), row), pl.BlockSpec((tm, LANES), row), pl.BlockSpec((tm, LANES), row),
        ],
        out_specs=(pl.BlockSpec((tm, nq), row), pl.BlockSpec((tm, nk), row), pl.BlockSpec((tm, nk), row)),
        compiler_params=_params(1),
        name="swa_proj",
    )(x, sc, sh, w["w_qkv"], w["b_qkv"], tabs["c"], tabs["slo"], tabs["shi"])


def _swa_attn_kernel(sinks_ref, q_ref, kp_ref, kc_ref, vp_ref, vc_ref, x_ref, g_ref, w_ref, b_ref,
                     lng_ref, lnb_ref, y_ref, kwin_ref, vwin_ref, o_ref):
    i = pl.program_id(0)
    w = SWA_WINDOW
    tq = SWA_TQ
    grp = SWA_HEADS // SWA_KV_HEADS
    scale = SWA_HEAD_DIM ** -0.5
    kwin_ref[0:w, :] = kp_ref[...]
    kwin_ref[w:, :] = kc_ref[...]
    vwin_ref[0:w, :] = vp_ref[...]
    vwin_ref[w:, :] = vc_ref[...]

    r = lax.broadcasted_iota(jnp.int32, (w, 2 * w), 0)
    cidx = lax.broadcasted_iota(jnp.int32, (w, 2 * w), 1)
    band = (cidx > r) & (cidx <= r + w)
    lane = lax.broadcasted_iota(jnp.int32, (1, LANES), 1)
    lo = lane < SWA_HEAD_DIM

    def block(b, carry):
        r0 = pl.multiple_of(b * w, w)
        first_key = jnp.where(i * (tq // w) + b > 0, 0, w)
        mask = band & (cidx >= first_key)
        for kvh in range(SWA_KV_HEADS):
            kk = kwin_ref[pl.ds(r0, 2 * w), kvh * LANES:(kvh + 1) * LANES]
            vv = vwin_ref[pl.ds(r0, 2 * w), kvh * LANES:(kvh + 1) * LANES]
            v_lo = jnp.where(lo, vv, jnp.zeros_like(vv))
            v_hi = jnp.where(lo, jnp.zeros_like(vv), vv)
            for pr in range(grp // 2):
                col = (kvh * grp + 2 * pr) * SWA_HEAD_DIM
                qp = q_ref[pl.ds(r0, w), col:col + LANES]
                out = jnp.zeros((w, LANES), F32)
                for half in range(2):
                    hd = kvh * grp + 2 * pr + half
                    qm = jnp.where(lo, qp, jnp.zeros_like(qp)) if half == 0 else jnp.where(lo, jnp.zeros_like(qp), qp)
                    s = _dot_nt(qm, kk) * scale
                    s = jnp.where(mask, s, -jnp.inf)
                    sink = sinks_ref[hd]
                    m = jnp.maximum(jnp.max(s, axis=-1, keepdims=True), sink)
                    p = jnp.exp(s - m)
                    denom = jnp.sum(p, axis=-1, keepdims=True) + jnp.exp(sink - m)
                    pv = _dot(p.astype(BF16), v_lo if half == 0 else v_hi)
                    out = out + pv / denom
                o_ref[pl.ds(r0, w), col:col + LANES] = out.astype(BF16)
        return carry

    lax.fori_loop(0, tq // w, block, 0)
    y = _dot(o_ref[...], w_ref[...]) + b_ref[...]
    z = DEEPNORM_ALPHA * x_ref[...] + g_ref[...] * y
    y_ref[...] = _layer_norm(z, lng_ref[...], lnb_ref[...])


def _swa_attn(sinks, q, k, v, x, gate, w_o, b_o, ln_g, ln_b):
    s, d = x.shape
    tq = SWA_TQ
    w = SWA_WINDOW
    nk = k.shape[1]
    r = tq // w
    row = lambda i: (i, 0)
    prev = lambda i: (jnp.maximum(i * r - 1, 0), 0)
    return pl.pallas_call(
        _swa_attn_kernel,
        out_shape=jax.ShapeDtypeStruct((s, d), F32),
        grid=(s // tq,),
        in_specs=[
            pl.BlockSpec(memory_space=pltpu.SMEM),
            pl.BlockSpec((tq, q.shape[1]), row),
            pl.BlockSpec((w, nk), prev), pl.BlockSpec((tq, nk), row),
            pl.BlockSpec((w, nk), prev), pl.BlockSpec((tq, nk), row),
            pl.BlockSpec((tq, d), row),
            _const_spec((1, d)), _const_spec(w_o.shape), _const_spec((1, d)),
            _const_spec((1, d)), _const_spec((1, d)),
        ],
        out_specs=pl.BlockSpec((tq, d), row),
        scratch_shapes=[
            pltpu.VMEM((tq + w, nk), BF16), pltpu.VMEM((tq + w, nk), BF16),
            pltpu.VMEM((tq, SWA_HEADS * SWA_HEAD_DIM), BF16)],
        compiler_params=_params(1),
        name="swa_attn",
    )(sinks, q, k, k, v, v, x, gate, w_o, b_o, ln_g, ln_b)


def _rope_angles(positions, rot_dim):
    inv = ROPE_THETA ** (-jnp.arange(0, rot_dim, 2, dtype=F32) / rot_dim)
    return positions.astype(F32)[:, None] * inv


def _mla_tables(positions):
    ang = _rope_angles(positions, MLA_ROPE)
    c, s = jnp.cos(ang), jnp.sin(ang)
    one = jnp.ones_like(c)
    zero = jnp.zeros_like(c)
    return {
        "cq": jnp.concatenate([c, c, one, one], axis=1),
        "sq": jnp.concatenate([s, s, zero, zero], axis=1),
        "ckt": c.T, "skt": s.T,
    }


def _swa_tables(positions):
    ang = _rope_angles(positions, SWA_ROT)
    c, s = jnp.cos(ang), jnp.sin(ang)
    n = positions.shape[0]
    rest = SWA_HEAD_DIM - SWA_ROT
    one = jnp.ones((n, rest), F32)
    z8 = jnp.zeros_like(s)
    zrest = jnp.zeros((n, rest), F32)
    c64 = jnp.concatenate([c, c, one], axis=1)
    slo64 = jnp.concatenate([-s, z8, zrest], axis=1)
    shi64 = jnp.concatenate([z8, s, zrest], axis=1)
    tile2 = lambda t: jnp.concatenate([t, t], axis=1)
    return {"c": tile2(c64), "slo": tile2(slo64), "shi": tile2(shi64)}


def _mla_weights(w_in, q_norm, w_q_b, kv_norm, w_kv_b, w_o):
    d = D_MODEL
    w_in_p = jnp.concatenate([w_in, jnp.zeros((d, 64), F32)], axis=1).astype(BF16)
    wq = w_q_b.reshape(MLA_Q_RANK, MLA_HEADS, MLA_NOPE + MLA_ROPE)
    wq = jnp.concatenate([wq, jnp.zeros((MLA_Q_RANK, MLA_HEADS, 64), F32)], axis=2)
    wq = wq.reshape(MLA_Q_RANK, MLA_HEADS * MLA_QK_PAD).astype(BF16)
    wkv = w_kv_b.reshape(MLA_KV_RANK, MLA_HEADS, MLA_NOPE + MLA_V)
    wkt = wkv[:, :, :MLA_NOPE].reshape(MLA_KV_RANK, MLA_HEADS * MLA_NOPE).T.astype(BF16)
    wv = wkv[:, :, MLA_NOPE:].reshape(MLA_KV_RANK, MLA_HEADS * MLA_V).astype(BF16)
    return {
        "w_in": w_in_p, "q_norm": q_norm.reshape(1, -1), "wq": wq,
        "kv_norm": kv_norm.reshape(1, -1), "wkt": wkt, "wv": wv, "w_o": w_o.astype(BF16),
    }


def _swa_weights(w_qkv, b_qkv, w_o):
    nq = SWA_HEADS * SWA_HEAD_DIM
    nkv = SWA_KV_HEADS * SWA_HEAD_DIM

    def dup(t):
        lead = t.shape[:-1]
        t = t.reshape(lead + (SWA_KV_HEADS, 1, SWA_HEAD_DIM))
        t = jnp.broadcast_to(t, lead + (SWA_KV_HEADS, 2, SWA_HEAD_DIM))
        return t.reshape(lead + (2 * nkv,))

    def relayout(t):
        return jnp.concatenate([t[..., :nq], dup(t[..., nq:nq + nkv]), dup(t[..., nq + nkv:])], axis=-1)

    return {
        "w_qkv": relayout(w_qkv).astype(BF16),
        "b_qkv": relayout(b_qkv).reshape(1, -1),
        "w_o": w_o.astype(BF16),
    }


def kernel(x, c, positions, ada_w, ada_b, ln_mix_g, ln_mix_b, ln_ffn_g, ln_ffn_b, ffn_w_gate, ffn_w_up, ffn_w_down, mla_w_in, mla_q_norm, mla_w_q_b, mla_kv_norm, mla_w_kv_b, mla_w_o, swa_w_qkv, swa_b_qkv, swa_sinks, swa_w_o, swa_b_o):
    b, s, d = x.shape
    assert (b, s, d) == (1, SEQ, D_MODEL)
    xs = x.reshape(s, d)
    pos = positions.reshape(s)
    mod = _modulation(c, ada_w, ada_b)
    mla_tabs = _mla_tables(pos)
    swa_tabs = _swa_tables(pos)
    zero_bias = jnp.zeros((1, d), F32)
    vec = lambda t: t.reshape(1, d)
    for i in range(DEPTH):
        sh_m, sc_m, g_m, sh_f, sc_f, g_f = (mod[i, k] for k in range(6))
        j = i // 2
        if i % 2 == 0:
            w = _mla_weights(mla_w_in[j], mla_q_norm[j], mla_w_q_b[j], mla_kv_norm[j], mla_w_kv_b[j], mla_w_o[j])
            q, kt, v = _mla_proj(xs, sc_m, sh_m, w, mla_tabs)
            o = _mla_attn(q, kt, v)
            xs = _out_ln(o, xs, g_m, w["w_o"], zero_bias, vec(ln_mix_g[i]), vec(ln_mix_b[i]))
        else:
            w = _swa_weights(swa_w_qkv[j], swa_b_qkv[j], swa_w_o[j])
            q, k, v = _swa_proj(xs, sc_m, sh_m, w, swa_tabs)
            xs = _swa_attn(swa_sinks[j], q, k, v, xs, g_m, w["w_o"], vec(swa_b_o[j]),
                           vec(ln_mix_g[i]), vec(ln_mix_b[i]))
        xs = _ffn(xs, sc_f, sh_f, g_f, ffn_w_gate[i].astype(BF16), ffn_w_up[i].astype(BF16),
                  ffn_w_down[i].astype(BF16), vec(ln_ffn_g[i]), vec(ln_ffn_b[i]))
    return xs.reshape(b, s, d)
```

```python
import functools

import jax
import jax.numpy as jnp
from jax import lax
from jax.experimental import pallas as pl
from jax.experimental.pallas import tpu as pltpu

D_MODEL = 1024
SEQ = 16384
DEPTH = 4
ROPE_THETA = 500000.0
LN_EPS = 1e-5
RMS_EPS = 1e-6

MLA_HEADS = 8
MLA_NOPE = 128
MLA_ROPE = 64
MLA_V = 128
MLA_Q_RANK = 384
MLA_KV_RANK = 256
MLA_QK_PAD = 256

SWA_HEADS = 16
SWA_KV_HEADS = 4
SWA_HEAD_DIM = 64
SWA_WINDOW = 128
SWA_ROT = SWA_HEAD_DIM // 4

D_FF = ((8 * D_MODEL + 3 * 256 - 1) // (3 * 256)) * 256
DEEPNORM_ALPHA = (2 * DEPTH) ** 0.25

LANES = 128
VMEM_LIMIT = 56 * 1024 * 1024

ROW_BLOCK = 512
MLA_T = 512
SWA_TQ = 512
MLA_Q_SCALE = (MLA_NOPE + MLA_ROPE) ** -0.5 * 1.4426950408889634

BF16 = jnp.bfloat16
F32 = jnp.float32


def _dot(a, b):
    return jnp.dot(a, b, preferred_element_type=F32)


def _dot_nt(a, b):
    return lax.dot_general(a, b, (((1,), (1,)), ((), ())), preferred_element_type=F32)


def _layer_norm(z, g, b):
    mu = jnp.mean(z, axis=-1, keepdims=True)
    zc = z - mu
    var = jnp.mean(zc * zc, axis=-1, keepdims=True)
    return zc * lax.rsqrt(var + LN_EPS) * g + b


def _rms_norm(z, g):
    return z * lax.rsqrt(jnp.mean(z * z, axis=-1, keepdims=True) + RMS_EPS) * g


def _const_spec(shape):
    nd = len(shape)
    return pl.BlockSpec(shape, lambda *_: (0,) * nd, pipeline_mode=pl.Buffered(1))


def _params(n_grid):
    return pltpu.CompilerParams(
        dimension_semantics=("arbitrary",) * n_grid, vmem_limit_bytes=VMEM_LIMIT)


def _mod_kernel(c_ref, w_ref, b_ref, o_ref):
    c = c_ref[...]
    cond = (c * jax.nn.sigmoid(c)).astype(BF16)
    cond8 = jnp.broadcast_to(cond, (8, D_MODEL))
    y = _dot(cond8, w_ref[0].astype(BF16))
    o_ref[0] = y[0:1] + b_ref[0]


def _modulation(c, ada_w, ada_b):
    d = D_MODEL
    out = pl.pallas_call(
        _mod_kernel,
        out_shape=jax.ShapeDtypeStruct((DEPTH, 1, 6 * d), F32),
        grid=(DEPTH, 6),
        in_specs=[
            pl.BlockSpec((1, d), lambda i, k: (0, 0)),
            pl.BlockSpec((1, d, d), lambda i, k: (i, 0, k)),
            pl.BlockSpec((1, 1, d), lambda i, k: (i, 0, k)),
        ],
        out_specs=pl.BlockSpec((1, 1, d), lambda i, k: (i, 0, k)),
        compiler_params=_params(2),
        name="adaln_mod",
    )(c, ada_w, ada_b.reshape(DEPTH, 1, 6 * d))
    return out.reshape(DEPTH, 6, 1, d)


def _mla_proj_kernel(x_ref, sc_ref, sh_ref, w_in_ref, qn_ref, wqt_ref, kvn_ref, wk_ref, wvt_ref,
                     cq_ref, sq_ref, ckt_ref, skt_ref, qt_ref, k_ref, vt_ref):
    tm = x_ref.shape[0]
    h = (x_ref[...] * (1.0 + sc_ref[...]) + sh_ref[...]).astype(BF16)
    lat = _dot(h, w_in_ref[...])
    q_lat = lat[:, :MLA_Q_RANK]
    kv_lat = lat[:, MLA_Q_RANK:MLA_Q_RANK + MLA_KV_RANK]
    kr = lat[:, MLA_Q_RANK + MLA_KV_RANK:]

    qn = _rms_norm(q_lat, qn_ref[...]).astype(BF16)
    qt = _dot_nt(wqt_ref[...], qn) * MLA_Q_SCALE
    ck = ckt_ref[...]
    sk = skt_ref[...]
    hw = MLA_NOPE + MLA_ROPE
    half = MLA_ROPE // 2
    for hd in range(MLA_HEADS):
        r0 = hd * hw
        x1 = qt[r0 + MLA_NOPE:r0 + MLA_NOPE + half]
        x2 = qt[r0 + MLA_NOPE + half:r0 + hw]
        qt_ref[hd, 0:MLA_NOPE, :] = qt[r0:r0 + MLA_NOPE].astype(BF16)
        qt_ref[hd, MLA_NOPE:MLA_NOPE + half, :] = (x1 * ck - x2 * sk).astype(BF16)
        qt_ref[hd, MLA_NOPE + half:hw, :] = (x2 * ck + x1 * sk).astype(BF16)
        qt_ref[hd, hw:, :] = jnp.zeros((MLA_QK_PAD - hw, tm), BF16)

    kvn = _rms_norm(kv_lat, kvn_ref[...]).astype(BF16)
    kn = _dot(kvn, wk_ref[...])
    partner = pltpu.roll(kr, half, axis=1) - pltpu.roll(kr, LANES - half, axis=1)
    k_rope = (kr * cq_ref[...] + partner * sq_ref[...]).astype(BF16)
    for hd in range(MLA_HEADS):
        k_ref[:, hd * MLA_QK_PAD:hd * MLA_QK_PAD + MLA_NOPE] = kn[:, hd * MLA_NOPE:(hd + 1) * MLA_NOPE].astype(BF16)
        k_ref[:, hd * MLA_QK_PAD + MLA_NOPE:(hd + 1) * MLA_QK_PAD] = k_rope

    vt = _dot_nt(wvt_ref[...], kvn)
    for hd in range(MLA_HEADS):
        vt_ref[hd] = vt[hd * MLA_V:(hd + 1) * MLA_V].astype(BF16)


def _mla_proj(x, sc, sh, w, tabs):
    s, d = x.shape
    tm = ROW_BLOCK
    hk = MLA_HEADS * MLA_QK_PAD
    row = lambda i: (i, 0)
    col = lambda i: (0, i)
    return pl.pallas_call(
        _mla_proj_kernel,
        out_shape=(
            jax.ShapeDtypeStruct((MLA_HEADS, MLA_QK_PAD, s), BF16),
            jax.ShapeDtypeStruct((s, hk), BF16),
            jax.ShapeDtypeStruct((MLA_HEADS, MLA_V, s), BF16),
        ),
        grid=(s // tm,),
        in_specs=[
            pl.BlockSpec((tm, d), row),
            _const_spec((1, d)), _const_spec((1, d)),
            _const_spec(w["w_in"].shape), _const_spec(w["q_norm"].shape), _const_spec(w["wqt"].shape),
            _const_spec(w["kv_norm"].shape), _const_spec(w["wk"].shape), _const_spec(w["wvt"].shape),
            pl.BlockSpec((tm, LANES), row), pl.BlockSpec((tm, LANES), row),
            pl.BlockSpec((MLA_ROPE // 2, tm), col), pl.BlockSpec((MLA_ROPE // 2, tm), col),
        ],
        out_specs=(
            pl.BlockSpec((MLA_HEADS, MLA_QK_PAD, tm), lambda i: (0, 0, i)),
            pl.BlockSpec((tm, hk), row),
            pl.BlockSpec((MLA_HEADS, MLA_V, tm), lambda i: (0, 0, i)),
        ),
        compiler_params=_params(1),
        name="mla_proj",
    )(x, sc, sh, w["w_in"], w["q_norm"], w["wqt"], w["kv_norm"], w["wk"], w["wvt"],
      tabs["cq"], tabs["sq"], tabs["ckt"], tabs["skt"])


def _mla_attn_kernel(qt_ref, k_ref, vt_ref, o_ref, s0_ref, s1_ref, m_ref, l_ref, acc_ref):
    qi = pl.program_id(1)
    t = MLA_T
    qt = qt_ref[0]
    m_ref[...] = jnp.full(m_ref.shape, -jnp.inf, F32)
    l_ref[...] = jnp.zeros(l_ref.shape, F32)
    acc_ref[...] = jnp.zeros(acc_ref.shape, F32)

    def scores(j):
        k0 = pl.multiple_of(j * t, t)
        return _dot(k_ref[pl.ds(k0, t), :], qt)

    def softmax_pv(j, st, masked):
        k0 = pl.multiple_of(j * t, t)
        if masked:
            key = lax.broadcasted_iota(jnp.int32, (t, t), 0)
            qry = lax.broadcasted_iota(jnp.int32, (t, t), 1)
            st = jnp.where(key <= qry, st, -jnp.inf)
        m_prev = m_ref[...]
        m_new = jnp.maximum(m_prev, jnp.max(st, axis=0, keepdims=True))
        alpha = jnp.exp2(m_prev - m_new)
        p = jnp.exp2(st - m_new)
        l_ref[...] = alpha * l_ref[...] + jnp.sum(p.reshape(t // 8, 8, t), axis=0)
        pv = _dot(vt_ref[0, :, pl.ds(k0, t)], p.astype(BF16))
        acc_ref[...] = alpha * acc_ref[...] + pv
        m_ref[...] = m_new

    odd = qi % 2 == 1

    @pl.when(odd)
    def _():
        s1_ref[...] = scores(0)
        s0_ref[...] = scores(1)
        softmax_pv(0, s1_ref[...], False)

    @pl.when(jnp.logical_not(odd))
    def _():
        s0_ref[...] = scores(0)

    first = qi % 2

    def body(jj, carry):
        j = first + 2 * jj
        s1_ref[...] = scores(j + 1)
        softmax_pv(j, s0_ref[...], False)
        s0_ref[...] = scores(j + 2)
        softmax_pv(j + 1, s1_ref[...], False)
        return carry

    lax.fori_loop(0, qi // 2, body, 0)
    softmax_pv(qi, s0_ref[...], True)
    l = jnp.sum(l_ref[...], axis=0, keepdims=True)
    o_ref[...] = (acc_ref[...] / l).T.astype(BF16)


def _mla_attn(qt, k, vt):
    s = k.shape[0]
    t = MLA_T
    return pl.pallas_call(
        _mla_attn_kernel,
        out_shape=jax.ShapeDtypeStruct((s, MLA_HEADS * MLA_V), BF16),
        grid=(MLA_HEADS, s // t),
        in_specs=[
            pl.BlockSpec((1, MLA_QK_PAD, t), lambda h, i: (h, 0, i)),
            pl.BlockSpec((s, MLA_QK_PAD), lambda h, i: (0, h)),
            pl.BlockSpec((1, MLA_V, s), lambda h, i: (h, 0, 0)),
        ],
        out_specs=pl.BlockSpec((t, MLA_V), lambda h, i: (i, h)),
        scratch_shapes=[
            pltpu.VMEM((t, t), F32), pltpu.VMEM((t, t), F32),
            pltpu.VMEM((1, t), F32), pltpu.VMEM((8, t), F32), pltpu.VMEM((MLA_V, t), F32)],
        compiler_params=_params(2),
        name="mla_attn",
    )(qt, k, vt)


def _out_ln_kernel(o_ref, x_ref, g_ref, w_ref, b_ref, lng_ref, lnb_ref, y_ref):
    y = _dot(o_ref[...], w_ref[...]) + b_ref[...]
    z = DEEPNORM_ALPHA * x_ref[...] + g_ref[...] * y
    y_ref[...] = _layer_norm(z, lng_ref[...], lnb_ref[...])


def _out_ln(o, x, gate, w_o, b_o, ln_g, ln_b):
    s, d = x.shape
    tm = ROW_BLOCK
    row = lambda i: (i, 0)
    return pl.pallas_call(
        _out_ln_kernel,
        out_shape=jax.ShapeDtypeStruct((s, d), F32),
        grid=(s // tm,),
        in_specs=[
            pl.BlockSpec((tm, o.shape[1]), row), pl.BlockSpec((tm, d), row),
            _const_spec((1, d)), _const_spec(w_o.shape), _const_spec((1, d)),
            _const_spec((1, d)), _const_spec((1, d)),
        ],
        out_specs=pl.BlockSpec((tm, d), row),
        compiler_params=_params(1),
        name="out_ln",
    )(o, x, gate, w_o, b_o, ln_g, ln_b)


def _ffn_kernel(x_ref, sc_ref, sh_ref, g_ref, wg_ref, wu_ref, wd_ref, lng_ref, lnb_ref, y_ref):
    x = x_ref[...]
    h = (x * (1.0 + sc_ref[...]) + sh_ref[...]).astype(BF16)
    gate = _dot(h, wg_ref[...])
    up = _dot(h, wu_ref[...])
    act = (gate * jax.nn.sigmoid(gate) * up).astype(BF16)
    y = _dot(act, wd_ref[...])
    z = DEEPNORM_ALPHA * x + g_ref[...] * y
    y_ref[...] = _layer_norm(z, lng_ref[...], lnb_ref[...])


def _ffn(x, sc, sh, gate, wg, wu, wd, ln_g, ln_b):
    s, d = x.shape
    tm = ROW_BLOCK
    row = lambda i: (i, 0)
    return pl.pallas_call(
        _ffn_kernel,
        out_shape=jax.ShapeDtypeStruct((s, d), F32),
        grid=(s // tm,),
        in_specs=[
            pl.BlockSpec((tm, d), row),
            _const_spec((1, d)), _const_spec((1, d)), _const_spec((1, d)),
            _const_spec(wg.shape), _const_spec(wu.shape), _const_spec(wd.shape),
            _const_spec((1, d)), _const_spec((1, d)),
        ],
        out_specs=pl.BlockSpec((tm, d), row),
        compiler_params=_params(1),
        name="ffn",
    )(x, sc, sh, gate, wg, wu, wd, ln_g, ln_b)


def _swa_proj_kernel(x_ref, sc_ref, sh_ref, w_ref, b_ref, c_ref, slo_ref, shi_ref, q_ref, k_ref, v_ref):
    h = (x_ref[...] * (1.0 + sc_ref[...]) + sh_ref[...]).astype(BF16)
    qkv = _dot(h, w_ref[...]) + b_ref[...]
    c = c_ref[...]
    slo = slo_ref[...]
    shi = shi_ref[...]
    nq = SWA_HEADS * SWA_HEAD_DIM
    nk = 2 * SWA_KV_HEADS * SWA_HEAD_DIM

    def rope(xc):
        return xc * c + pltpu.roll(xc, LANES - SWA_ROT // 2, axis=1) * slo + pltpu.roll(xc, SWA_ROT // 2, axis=1) * shi

    for j in range(nq // LANES):
        q_ref[:, j * LANES:(j + 1) * LANES] = rope(qkv[:, j * LANES:(j + 1) * LANES]).astype(BF16)
    for j in range(nk // LANES):
        k_ref[:, j * LANES:(j + 1) * LANES] = rope(qkv[:, nq + j * LANES:nq + (j + 1) * LANES]).astype(BF16)
    v_ref[...] = qkv[:, nq + nk:].astype(BF16)


def _swa_proj(x, sc, sh, w, tabs):
    s, d = x.shape
    tm = ROW_BLOCK
    nq = SWA_HEADS * SWA_HEAD_DIM
    nk = 2 * SWA_KV_HEADS * SWA_HEAD_DIM
    row = lambda i: (i, 0)
    return pl.pallas_call(
        _swa_proj_kernel,
        out_shape=(
            jax.ShapeDtypeStruct((s, nq), BF16),
            jax.ShapeDtypeStruct((s, nk), BF16),
            jax.ShapeDtypeStruct((s, nk), BF16),
        ),
        grid=(s // tm,),
        in_specs=[
            pl.BlockSpec((tm, d), row),
            _const_spec((1, d)), _const_spec((1, d)),
            _const_spec(w["w_qkv"].shape), _const_spec(w["b_qkv"].shape),
            pl.BlockSpec((tm, LANES), row), pl.BlockSpec((tm, LANES), row), pl.BlockSpec((tm, LANES), row),
        ],
        out_specs=(pl.BlockSpec((tm, nq), row), pl.BlockSpec((tm, nk), row), pl.BlockSpec((tm, nk), row)),
        compiler_params=_params(1),
        name="swa_proj",
    )(x, sc, sh, w["w_qkv"], w["b_qkv"], tabs["c"], tabs["slo"], tabs["shi"])


def _swa_attn_kernel(sinks_ref, q_ref, kp_ref, kc_ref, vp_ref, vc_ref, x_ref, g_ref, w_ref, b_ref,
                     lng_ref, lnb_ref, y_ref, kwin_ref, vwin_ref, o_ref):
    i = pl.program_id(0)
    w = SWA_WINDOW
    tq = SWA_TQ
    grp = SWA_HEADS // SWA_KV_HEADS
    scale = SWA_HEAD_DIM ** -0.5
    kwin_ref[0:w, :] = kp_ref[...]
    kwin_ref[w:, :] = kc_ref[...]
    vwin_ref[0:w, :] = vp_ref[...]
    vwin_ref[w:, :] = vc_ref[...]

    r = lax.broadcasted_iota(jnp.int32, (w, 2 * w), 0)
    cidx = lax.broadcasted_iota(jnp.int32, (w, 2 * w), 1)
    band = (cidx > r) & (cidx <= r + w)
    lane = lax.broadcasted_iota(jnp.int32, (1, LANES), 1)
    lo = lane < SWA_HEAD_DIM

    def block(b, carry):
        r0 = pl.multiple_of(b * w, w)
        first_key = jnp.where(i * (tq // w) + b > 0, 0, w)
        mask = band & (cidx >= first_key)
        for kvh in range(SWA_KV_HEADS):
            kk = kwin_ref[pl.ds(r0, 2 * w), kvh * LANES:(kvh + 1) * LANES]
            vv = vwin_ref[pl.ds(r0, 2 * w), kvh * LANES:(kvh + 1) * LANES]
            v_lo = jnp.where(lo, vv, jnp.zeros_like(vv))
            v_hi = jnp.where(lo, jnp.zeros_like(vv), vv)
            for pr in range(grp // 2):
                col = (kvh * grp + 2 * pr) * SWA_HEAD_DIM
                qp = q_ref[pl.ds(r0, w), col:col + LANES]
                out = jnp.zeros((w, LANES), F32)
                for half in range(2):
                    hd = kvh * grp + 2 * pr + half
                    qm = jnp.where(lo, qp, jnp.zeros_like(qp)) if half == 0 else jnp.where(lo, jnp.zeros_like(qp), qp)
                    s = _dot_nt(qm, kk) * scale
                    s = jnp.where(mask, s, -jnp.inf)
                    sink = sinks_ref[hd]
                    m = jnp.maximum(jnp.max(s, axis=-1, keepdims=True), sink)
                    p = jnp.exp(s - m)
                    denom = jnp.sum(p, axis=-1, keepdims=True) + jnp.exp(sink - m)
                    pv = _dot(p.astype(BF16), v_lo if half == 0 else v_hi)
                    out = out + pv / denom
                o_ref[pl.ds(r0, w), col:col + LANES] = out.astype(BF16)
        return carry

    lax.fori_loop(0, tq // w, block, 0)
    y = _dot(o_ref[...], w_ref[...]) + b_ref[...]
    z = DEEPNORM_ALPHA * x_ref[...] + g_ref[...] * y
    y_ref[...] = _layer_norm(z, lng_ref[...], lnb_ref[...])


def _swa_attn(sinks, q, k, v, x, gate, w_o, b_o, ln_g, ln_b):
    s, d = x.shape
    tq = SWA_TQ
    w = SWA_WINDOW
    nk = k.shape[1]
    r = tq // w
    row = lambda i: (i, 0)
    prev = lambda i: (jnp.maximum(i * r - 1, 0), 0)
    return pl.pallas_call(
        _swa_attn_kernel,
        out_shape=jax.ShapeDtypeStruct((s, d), F32),
        grid=(s // tq,),
        in_specs=[
            pl.BlockSpec(memory_space=pltpu.SMEM),
            pl.BlockSpec((tq, q.shape[1]), row),
            pl.BlockSpec((w, nk), prev), pl.BlockSpec((tq, nk), row),
            pl.BlockSpec((w, nk), prev), pl.BlockSpec((tq, nk), row),
            pl.BlockSpec((tq, d), row),
            _const_spec((1, d)), _const_spec(w_o.shape), _const_spec((1, d)),
            _const_spec((1, d)), _const_spec((1, d)),
        ],
        out_specs=pl.BlockSpec((tq, d), row),
        scratch_shapes=[
            pltpu.VMEM((tq + w, nk), BF16), pltpu.VMEM((tq + w, nk), BF16),
            pltpu.VMEM((tq, SWA_HEADS * SWA_HEAD_DIM), BF16)],
        compiler_params=_params(1),
        name="swa_attn",
    )(sinks, q, k, k, v, v, x, gate, w_o, b_o, ln_g, ln_b)


def _rope_angles(positions, rot_dim):
    inv = ROPE_THETA ** (-jnp.arange(0, rot_dim, 2, dtype=F32) / rot_dim)
    return positions.astype(F32)[:, None] * inv


def _mla_tables(positions):
    ang = _rope_angles(positions, MLA_ROPE)
    c, s = jnp.cos(ang), jnp.sin(ang)
    one = jnp.ones_like(c)
    zero = jnp.zeros_like(c)
    return {
        "cq": jnp.concatenate([c, c, one, one], axis=1),
        "sq": jnp.concatenate([s, s, zero, zero], axis=1),
        "ckt": c.T, "skt": s.T,
    }


def _swa_tables(positions):
    ang = _rope_angles(positions, SWA_ROT)
    c, s = jnp.cos(ang), jnp.sin(ang)
    n = positions.shape[0]
    rest = SWA_HEAD_DIM - SWA_ROT
    one = jnp.ones((n, rest), F32)
    z8 = jnp.zeros_like(s)
    zrest = jnp.zeros((n, rest), F32)
    c64 = jnp.concatenate([c, c, one], axis=1)
    slo64 = jnp.concatenate([-s, z8, zrest], axis=1)
    shi64 = jnp.concatenate([z8, s, zrest], axis=1)
    tile2 = lambda t: jnp.concatenate([t, t], axis=1)
    return {"c": tile2(c64), "slo": tile2(slo64), "shi": tile2(shi64)}


def _mla_weights(w_in, q_norm, w_q_b, kv_norm, w_kv_b, w_o):
    d = D_MODEL
    w_in_p = jnp.concatenate([w_in, jnp.zeros((d, 64), F32)], axis=1).astype(BF16)
    wkv = w_kv_b.reshape(MLA_KV_RANK, MLA_HEADS, MLA_NOPE + MLA_V)
    wk = wkv[:, :, :MLA_NOPE].reshape(MLA_KV_RANK, MLA_HEADS * MLA_NOPE).astype(BF16)
    wvt = wkv[:, :, MLA_NOPE:].reshape(MLA_KV_RANK, MLA_HEADS * MLA_V).T.astype(BF16)
    return {
        "w_in": w_in_p, "q_norm": q_norm.reshape(1, -1), "wqt": w_q_b.T.astype(BF16),
        "kv_norm": kv_norm.reshape(1, -1), "wk": wk, "wvt": wvt, "w_o": w_o.astype(BF16),
    }


def _swa_weights(w_qkv, b_qkv, w_o):
    nq = SWA_HEADS * SWA_HEAD_DIM
    nkv = SWA_KV_HEADS * SWA_HEAD_DIM

    def dup(t):
        lead = t.shape[:-1]
        t = t.reshape(lead + (SWA_KV_HEADS, 1, SWA_HEAD_DIM))
        t = jnp.broadcast_to(t, lead + (SWA_KV_HEADS, 2, SWA_HEAD_DIM))
        return t.reshape(lead + (2 * nkv,))

    def relayout(t):
        return jnp.concatenate([t[..., :nq], dup(t[..., nq:nq + nkv]), dup(t[..., nq + nkv:])], axis=-1)

    return {
        "w_qkv": relayout(w_qkv).astype(BF16),
        "b_qkv": relayout(b_qkv).reshape(1, -1),
        "w_o": w_o.astype(BF16),
    }


def kernel(x, c, positions, ada_w, ada_b, ln_mix_g, ln_mix_b, ln_ffn_g, ln_ffn_b, ffn_w_gate, ffn_w_up, ffn_w_down, mla_w_in, mla_q_norm, mla_w_q_b, mla_kv_norm, mla_w_kv_b, mla_w_o, swa_w_qkv, swa_b_qkv, swa_sinks, swa_w_o, swa_b_o):
    b, s, d = x.shape
    assert (b, s, d) == (1, SEQ, D_MODEL)
    xs = x.reshape(s, d)
    pos = positions.reshape(s)
    mod = _modulation(c, ada_w, ada_b)
    mla_tabs = _mla_tables(pos)
    swa_tabs = _swa_tables(pos)
    zero_bias = jnp.zeros((1, d), F32)
    vec = lambda t: t.reshape(1, d)
    for i in range(DEPTH):
        sh_m, sc_m, g_m, sh_f, sc_f, g_f = (mod[i, k] for k in range(6))
        j = i // 2
        if i % 2 == 0:
            w = _mla_weights(mla_w_in[j], mla_q_norm[j], mla_w_q_b[j], mla_kv_norm[j], mla_w_kv_b[j], mla_w_o[j])
            qt, k, vt = _mla_proj(xs, sc_m, sh_m, w, mla_tabs)
            o = _mla_attn(qt, k, vt)
            xs = _out_ln(o, xs, g_m, w["w_o"], zero_bias, vec(ln_mix_g[i]), vec(ln_mix_b[i]))
        else:
            w = _swa_weights(swa_w_qkv[j], swa_b_qkv[j], swa_w_o[j])
            q, k, v = _swa_proj(xs, sc_m, sh_m, w, swa_tabs)
            xs = _swa_attn(swa_sinks[j], q, k, v, xs, g_m, w["w_o"], vec(swa_b_o[j]),
                           vec(ln_mix_g[i]), vec(ln_mix_b[i]))
        xs = _ffn(xs, sc_f, sh_f, g_f, ffn_w_gate[i].astype(BF16), ffn_w_up[i].astype(BF16),
                  ffn_w_down[i].astype(BF16), vec(ln_ffn_g[i]), vec(ln_ffn_b[i]))
    return xs.reshape(b, s, d)
```

```python
import functools

import jax
import jax.numpy as jnp
from jax import lax
from jax.experimental import pallas as pl
from jax.experimental.pallas import tpu as pltpu

D_MODEL = 1024
SEQ = 16384
DEPTH = 4
ROPE_THETA = 500000.0
LN_EPS = 1e-5
RMS_EPS = 1e-6

MLA_HEADS = 8
MLA_NOPE = 128
MLA_ROPE = 64
MLA_V = 128
MLA_Q_RANK = 384
MLA_KV_RANK = 256
MLA_QK_PAD = 256

SWA_HEADS = 16
SWA_KV_HEADS = 4
SWA_HEAD_DIM = 64
SWA_WINDOW = 128
SWA_ROT = SWA_HEAD_DIM // 4

D_FF = ((8 * D_MODEL + 3 * 256 - 1) // (3 * 256)) * 256
DEEPNORM_ALPHA = (2 * DEPTH) ** 0.25

LANES = 128
VMEM_LIMIT = 56 * 1024 * 1024

ROW_BLOCK = 512
MLA_T = 512
MLA_UNROLL = 16
MLA_SBUFS = 4
MLA_ATTN_VMEM = 58 * 1024 * 1024
SWA_TQ = 512
MLA_Q_SCALE = (MLA_NOPE + MLA_ROPE) ** -0.5 * 1.4426950408889634

BF16 = jnp.bfloat16
F32 = jnp.float32


def _dot(a, b):
    return jnp.dot(a, b, preferred_element_type=F32)


def _dot_nt(a, b):
    return lax.dot_general(a, b, (((1,), (1,)), ((), ())), preferred_element_type=F32)


def _layer_norm(z, g, b):
    mu = jnp.mean(z, axis=-1, keepdims=True)
    zc = z - mu
    var = jnp.mean(zc * zc, axis=-1, keepdims=True)
    return zc * lax.rsqrt(var + LN_EPS) * g + b


def _rms_norm(z, g):
    return z * lax.rsqrt(jnp.mean(z * z, axis=-1, keepdims=True) + RMS_EPS) * g


def _const_spec(shape):
    nd = len(shape)
    return pl.BlockSpec(shape, lambda *_: (0,) * nd, pipeline_mode=pl.Buffered(1))


def _params(n_grid):
    return pltpu.CompilerParams(
        dimension_semantics=("arbitrary",) * n_grid, vmem_limit_bytes=VMEM_LIMIT)


def _mod_kernel(c_ref, w_ref, b_ref, o_ref):
    c = c_ref[...]
    cond = (c * jax.nn.sigmoid(c)).astype(BF16)
    cond8 = jnp.broadcast_to(cond, (8, D_MODEL))
    y = _dot(cond8, w_ref[0].astype(BF16))
    o_ref[0] = y[0:1] + b_ref[0]


def _modulation(c, ada_w, ada_b):
    d = D_MODEL
    out = pl.pallas_call(
        _mod_kernel,
        out_shape=jax.ShapeDtypeStruct((DEPTH, 1, 6 * d), F32),
        grid=(DEPTH, 6),
        in_specs=[
            pl.BlockSpec((1, d), lambda i, k: (0, 0)),
            pl.BlockSpec((1, d, d), lambda i, k: (i, 0, k)),
            pl.BlockSpec((1, 1, d), lambda i, k: (i, 0, k)),
        ],
        out_specs=pl.BlockSpec((1, 1, d), lambda i, k: (i, 0, k)),
        compiler_params=_params(2),
        name="adaln_mod",
    )(c, ada_w, ada_b.reshape(DEPTH, 1, 6 * d))
    return out.reshape(DEPTH, 6, 1, d)


def _mla_proj_kernel(x_ref, sc_ref, sh_ref, w_in_ref, qn_ref, wqt_ref, kvn_ref, wk_ref, wvt_ref,
                     ckt_ref, skt_ref, qt_ref, k_ref, vt_ref):
    tm = x_ref.shape[0]
    h = (x_ref[...] * (1.0 + sc_ref[...]) + sh_ref[...]).astype(BF16)
    lat = _dot(h, w_in_ref[...])
    q_lat = lat[:, :MLA_Q_RANK]
    kv_lat = lat[:, MLA_Q_RANK:MLA_Q_RANK + MLA_KV_RANK]
    kr = lat[:, MLA_Q_RANK + MLA_KV_RANK:]

    qn = _rms_norm(q_lat, qn_ref[...]).astype(BF16)
    qt = _dot_nt(wqt_ref[...], qn) * MLA_Q_SCALE
    ck = ckt_ref[...]
    sk = skt_ref[...]
    hw = MLA_NOPE + MLA_ROPE
    half = MLA_ROPE // 2
    for hd in range(MLA_HEADS):
        r0 = hd * hw
        x1 = qt[r0 + MLA_NOPE:r0 + MLA_NOPE + half]
        x2 = qt[r0 + MLA_NOPE + half:r0 + hw]
        qt_ref[hd, 0:MLA_NOPE, :] = qt[r0:r0 + MLA_NOPE].astype(BF16)
        qt_ref[hd, MLA_NOPE:MLA_NOPE + half, :] = (x1 * ck - x2 * sk).astype(BF16)
        qt_ref[hd, MLA_NOPE + half:hw, :] = (x2 * ck + x1 * sk).astype(BF16)
        qt_ref[hd, hw:, :] = jnp.zeros((MLA_QK_PAD - hw, tm), BF16)

    kvn = _rms_norm(kv_lat, kvn_ref[...]).astype(BF16)
    kn = _dot(kvn, wk_ref[...])
    krt = kr.T
    x1 = krt[0:half]
    x2 = krt[half:MLA_ROPE]
    k_rope = jnp.concatenate(
        [x1 * ck - x2 * sk, x2 * ck + x1 * sk, krt[MLA_ROPE:]], axis=0).T.astype(BF16)
    for hd in range(MLA_HEADS):
        k_ref[:, hd * MLA_QK_PAD:hd * MLA_QK_PAD + MLA_NOPE] = kn[:, hd * MLA_NOPE:(hd + 1) * MLA_NOPE].astype(BF16)
        k_ref[:, hd * MLA_QK_PAD + MLA_NOPE:(hd + 1) * MLA_QK_PAD] = k_rope

    vt = _dot_nt(wvt_ref[...], kvn)
    for hd in range(MLA_HEADS):
        vt_ref[hd] = vt[hd * MLA_V:(hd + 1) * MLA_V].astype(BF16)


def _mla_proj(x, sc, sh, w, tabs):
    s, d = x.shape
    tm = ROW_BLOCK
    hk = MLA_HEADS * MLA_QK_PAD
    row = lambda i: (i, 0)
    col = lambda i: (0, i)
    return pl.pallas_call(
        _mla_proj_kernel,
        out_shape=(
            jax.ShapeDtypeStruct((MLA_HEADS, MLA_QK_PAD, s), BF16),
            jax.ShapeDtypeStruct((s, hk), BF16),
            jax.ShapeDtypeStruct((MLA_HEADS, MLA_V, s), BF16),
        ),
        grid=(s // tm,),
        in_specs=[
            pl.BlockSpec((tm, d), row),
            _const_spec((1, d)), _const_spec((1, d)),
            _const_spec(w["w_in"].shape), _const_spec(w["q_norm"].shape), _const_spec(w["wqt"].shape),
            _const_spec(w["kv_norm"].shape), _const_spec(w["wk"].shape), _const_spec(w["wvt"].shape),
            pl.BlockSpec((MLA_ROPE // 2, tm), col), pl.BlockSpec((MLA_ROPE // 2, tm), col),
        ],
        out_specs=(
            pl.BlockSpec((MLA_HEADS, MLA_QK_PAD, tm), lambda i: (0, 0, i)),
            pl.BlockSpec((tm, hk), row),
            pl.BlockSpec((MLA_HEADS, MLA_V, tm), lambda i: (0, 0, i)),
        ),
        compiler_params=_params(1),
        name="mla_proj",
    )(x, sc, sh, w["w_in"], w["q_norm"], w["wqt"], w["kv_norm"], w["wk"], w["wvt"],
      tabs["ckt"], tabs["skt"])


def _mla_attn_kernel(qt_ref, k_ref, vt_ref, o_ref, *scratch):
    t = MLA_T
    nq = qt_ref.shape[2] // t
    n_full = nq * (nq - 1) // 2
    assert n_full % MLA_UNROLL == 0 and MLA_UNROLL % MLA_SBUFS == 0 and nq % MLA_SBUFS == 0
    bufs = scratch[:MLA_SBUFS]
    m_ref, l_ref, acc_ref = scratch[MLA_SBUFS:]
    m_ref[...] = jnp.full(m_ref.shape, -jnp.inf, F32)
    l_ref[...] = jnp.zeros(l_ref.shape, F32)
    acc_ref[...] = jnp.zeros(acc_ref.shape, F32)

    def scores(qi, j):
        q0 = pl.multiple_of(qi * t, t)
        k0 = pl.multiple_of(j * t, t)
        return _dot(k_ref[pl.ds(k0, t), :], qt_ref[0, :, pl.ds(q0, t)])

    def update(qi, j, st, diag):
        k0 = pl.multiple_of(j * t, t)
        if diag:
            key = lax.broadcasted_iota(jnp.int32, (t, t), 0)
            qry = lax.broadcasted_iota(jnp.int32, (t, t), 1)
            st = jnp.where(key <= qry, st, -jnp.inf)
        m_prev = m_ref[qi]
        m_new = jnp.maximum(m_prev, jnp.max(st, axis=0, keepdims=True))
        alpha = jnp.exp2(m_prev - m_new)
        p = jnp.exp2(st - m_new)
        l_new = alpha * l_ref[qi] + jnp.sum(p.reshape(t // 8, 8, t), axis=0)
        acc_new = alpha * acc_ref[qi] + _dot(vt_ref[0, :, pl.ds(k0, t)], p.astype(BF16))
        if diag:
            q0 = pl.multiple_of(qi * t, t)
            l = jnp.sum(l_new, axis=0, keepdims=True)
            o_ref[pl.ds(q0, t), :] = (acc_new / l).T.astype(BF16)
        else:
            m_ref[qi] = m_new
            l_ref[qi] = l_new
            acc_ref[qi] = acc_new

    bufs[0][...] = scores(1, 0)

    def full_body(_, carry):
        qi, j = carry
        for u in range(MLA_UNROLL):
            last = j + 1 == qi
            qi_n = jnp.where(last, qi + 1, qi)
            j_n = jnp.where(last, 0, j + 1)
            bufs[(u + 1) % MLA_SBUFS][...] = scores(jnp.minimum(qi_n, nq - 1), j_n)
            update(qi, j, bufs[u % MLA_SBUFS][...], False)
            qi, j = qi_n, j_n
        return qi, j

    lax.fori_loop(0, n_full // MLA_UNROLL, full_body, (jnp.int32(1), jnp.int32(0)))

    bufs[0][...] = scores(0, 0)

    def diag_body(i, carry):
        for u in range(MLA_SBUFS):
            qi = MLA_SBUFS * i + u
            nxt = jnp.minimum(qi + 1, nq - 1)
            bufs[(u + 1) % MLA_SBUFS][...] = scores(nxt, nxt)
            update(qi, qi, bufs[u % MLA_SBUFS][...], True)
        return carry

    lax.fori_loop(0, nq // MLA_SBUFS, diag_body, 0)


def _mla_attn(qt, k, vt):
    s = k.shape[0]
    t = MLA_T
    return pl.pallas_call(
        _mla_attn_kernel,
        out_shape=jax.ShapeDtypeStruct((s, MLA_HEADS * MLA_V), BF16),
        grid=(MLA_HEADS,),
        in_specs=[
            pl.BlockSpec((1, MLA_QK_PAD, s), lambda h: (h, 0, 0)),
            pl.BlockSpec((s, MLA_QK_PAD), lambda h: (0, h), pipeline_mode=pl.Buffered(1)),
            pl.BlockSpec((1, MLA_V, s), lambda h: (h, 0, 0)),
        ],
        out_specs=pl.BlockSpec((s, MLA_V), lambda h: (0, h)),
        scratch_shapes=[
            *[pltpu.VMEM((t, t), F32)] * MLA_SBUFS,
            pltpu.VMEM((s // t, 1, t), F32), pltpu.VMEM((s // t, 8, t), F32),
            pltpu.VMEM((s // t, MLA_V, t), F32)],
        compiler_params=pltpu.CompilerParams(
            dimension_semantics=("arbitrary",), vmem_limit_bytes=MLA_ATTN_VMEM),
        name="mla_attn",
    )(qt, k, vt)


def _ffn_kernel(*refs, with_mix):
    if with_mix:
        o_ref, wo_ref, gm_ref, lnmg_ref, lnmb_ref = refs[:5]
        refs = refs[5:]
    x_ref, sc_ref, sh_ref, g_ref, wg_ref, wu_ref, wd_ref, lng_ref, lnb_ref, y_ref = refs
    x = x_ref[...]
    if with_mix:
        z = DEEPNORM_ALPHA * x + gm_ref[...] * _dot(o_ref[...], wo_ref[...])
        x = _layer_norm(z, lnmg_ref[...], lnmb_ref[...])
    h = (x * (1.0 + sc_ref[...]) + sh_ref[...]).astype(BF16)
    gate = _dot(h, wg_ref[0])
    up = _dot(h, wu_ref[0])
    act = (gate * jax.nn.sigmoid(gate) * up).astype(BF16)
    y = _dot(act, wd_ref[0])
    z = DEEPNORM_ALPHA * x + g_ref[...] * y
    y_ref[...] = _layer_norm(z, lng_ref[...], lnb_ref[...])


def _ffn(x, layer, sc, sh, gate, wg, wu, wd, ln_g, ln_b, mix=None):
    s, d = x.shape
    tm = ROW_BLOCK
    row = lambda i: (i, 0)
    vec = _const_spec((1, d))
    stacked = lambda w: pl.BlockSpec((1,) + w.shape[1:], lambda i: (layer, 0, 0), pipeline_mode=pl.Buffered(1))
    mix_specs, mix_args = [], []
    if mix is not None:
        o, w_o = mix[0], mix[1]
        mix_specs = [pl.BlockSpec((tm, o.shape[1]), row), _const_spec(w_o.shape), vec, vec, vec]
        mix_args = list(mix)
    return pl.pallas_call(
        functools.partial(_ffn_kernel, with_mix=mix is not None),
        out_shape=jax.ShapeDtypeStruct((s, d), F32),
        grid=(s // tm,),
        in_specs=mix_specs + [
            pl.BlockSpec((tm, d), row), vec, vec, vec,
            stacked(wg), stacked(wu), stacked(wd), vec, vec,
        ],
        out_specs=pl.BlockSpec((tm, d), row),
        compiler_params=_params(1),
        name="mix_ffn" if mix is not None else "ffn",
    )(*mix_args, x, sc, sh, gate, wg, wu, wd, ln_g, ln_b)


def _swa_proj_kernel(x_ref, sc_ref, sh_ref, w_ref, b_ref, c_ref, slo_ref, shi_ref, q_ref, k_ref, v_ref):
    h = (x_ref[...] * (1.0 + sc_ref[...]) + sh_ref[...]).astype(BF16)
    qkv = _dot(h, w_ref[...]) + b_ref[...]
    c = c_ref[...]
    slo = slo_ref[...]
    shi = shi_ref[...]
    nq = SWA_HEADS * SWA_HEAD_DIM
    nk = 2 * SWA_KV_HEADS * SWA_HEAD_DIM

    def rope(xc):
        return xc * c + pltpu.roll(xc, LANES - SWA_ROT // 2, axis=1) * slo + pltpu.roll(xc, SWA_ROT // 2, axis=1) * shi

    for j in range(nq // LANES):
        q_ref[:, j * LANES:(j + 1) * LANES] = rope(qkv[:, j * LANES:(j + 1) * LANES]).astype(BF16)
    for j in range(nk // LANES):
        k_ref[:, j * LANES:(j + 1) * LANES] = rope(qkv[:, nq + j * LANES:nq + (j + 1) * LANES]).astype(BF16)
    v_ref[...] = qkv[:, nq + nk:].astype(BF16)


def _swa_proj(x, sc, sh, w, tabs):
    s, d = x.shape
    tm = ROW_BLOCK
    nq = SWA_HEADS * SWA_HEAD_DIM
    nk = 2 * SWA_KV_HEADS * SWA_HEAD_DIM
    row = lambda i: (i, 0)
    return pl.pallas_call(
        _swa_proj_kernel,
        out_shape=(
            jax.ShapeDtypeStruct((s, nq), BF16),
            jax.ShapeDtypeStruct((s, nk), BF16),
            jax.ShapeDtypeStruct((s, nk), BF16),
        ),
        grid=(s // tm,),
        in_specs=[
            pl.BlockSpec((tm, d), row),
            _const_spec((1, d)), _const_spec((1, d)),
            _const_spec(w["w_qkv"].shape), _const_spec(w["b_qkv"].shape),
            pl.BlockSpec((tm, LANES), row), pl.BlockSpec((tm, LANES), row), pl.BlockSpec((tm, LANES), row),
        ],
        out_specs=(pl.BlockSpec((tm, nq), row), pl.BlockSpec((tm, nk), row), pl.BlockSpec((tm, nk), row)),
        compiler_params=_params(1),
        name="swa_proj",
    )(x, sc, sh, w["w_qkv"], w["b_qkv"], tabs["c"], tabs["slo"], tabs["shi"])


def _swa_attn_kernel(sinks_ref, q_ref, kp_ref, kc_ref, vp_ref, vc_ref, x_ref, g_ref, w_ref, b_ref,
                     lng_ref, lnb_ref, y_ref, kwin_ref, vwin_ref, o_ref):
    i = pl.program_id(0)
    w = SWA_WINDOW
    tq = SWA_TQ
    grp = SWA_HEADS // SWA_KV_HEADS
    scale = SWA_HEAD_DIM ** -0.5
    kwin_ref[0:w, :] = kp_ref[...]
    kwin_ref[w:, :] = kc_ref[...]
    vwin_ref[0:w, :] = vp_ref[...]
    vwin_ref[w:, :] = vc_ref[...]

    r = lax.broadcasted_iota(jnp.int32, (w, 2 * w), 0)
    cidx = lax.broadcasted_iota(jnp.int32, (w, 2 * w), 1)
    band = (cidx > r) & (cidx <= r + w)
    lane = lax.broadcasted_iota(jnp.int32, (1, LANES), 1)
    lo = lane < SWA_HEAD_DIM

    def block(b, carry):
        r0 = pl.multiple_of(b * w, w)
        first_key = jnp.where(i * (tq // w) + b > 0, 0, w)
        mask = band & (cidx >= first_key)
        for kvh in range(SWA_KV_HEADS):
            kk = kwin_ref[pl.ds(r0, 2 * w), kvh * LANES:(kvh + 1) * LANES]
            vv = vwin_ref[pl.ds(r0, 2 * w), kvh * LANES:(kvh + 1) * LANES]
            v_lo = jnp.where(lo, vv, jnp.zeros_like(vv))
            v_hi = jnp.where(lo, jnp.zeros_like(vv), vv)
            for pr in range(grp // 2):
                col = (kvh * grp + 2 * pr) * SWA_HEAD_DIM
                qp = q_ref[pl.ds(r0, w), col:col + LANES]
                out = jnp.zeros((w, LANES), F32)
                for half in range(2):
                    hd = kvh * grp + 2 * pr + half
                    qm = jnp.where(lo, qp, jnp.zeros_like(qp)) if half == 0 else jnp.where(lo, jnp.zeros_like(qp), qp)
                    s = _dot_nt(qm, kk) * scale
                    s = jnp.where(mask, s, -jnp.inf)
                    sink = sinks_ref[hd]
                    m = jnp.maximum(jnp.max(s, axis=-1, keepdims=True), sink)
                    p = jnp.exp(s - m)
                    denom = jnp.sum(p, axis=-1, keepdims=True) + jnp.exp(sink - m)
                    pv = _dot(p.astype(BF16), v_lo if half == 0 else v_hi)
                    out = out + pv / denom
                o_ref[pl.ds(r0, w), col:col + LANES] = out.astype(BF16)
        return carry

    lax.fori_loop(0, tq // w, block, 0)
    y = _dot(o_ref[...], w_ref[...]) + b_ref[...]
    z = DEEPNORM_ALPHA * x_ref[...] + g_ref[...] * y
    y_ref[...] = _layer_norm(z, lng_ref[...], lnb_ref[...])


def _swa_attn(sinks, q, k, v, x, gate, w_o, b_o, ln_g, ln_b):
    s, d = x.shape
    tq = SWA_TQ
    w = SWA_WINDOW
    nk = k.shape[1]
    r = tq // w
    row = lambda i: (i, 0)
    prev = lambda i: (jnp.maximum(i * r - 1, 0), 0)
    return pl.pallas_call(
        _swa_attn_kernel,
        out_shape=jax.ShapeDtypeStruct((s, d), F32),
        grid=(s // tq,),
        in_specs=[
            pl.BlockSpec(memory_space=pltpu.SMEM),
            pl.BlockSpec((tq, q.shape[1]), row),
            pl.BlockSpec((w, nk), prev), pl.BlockSpec((tq, nk), row),
            pl.BlockSpec((w, nk), prev), pl.BlockSpec((tq, nk), row),
            pl.BlockSpec((tq, d), row),
            _const_spec((1, d)), _const_spec(w_o.shape), _const_spec((1, d)),
            _const_spec((1, d)), _const_spec((1, d)),
        ],
        out_specs=pl.BlockSpec((tq, d), row),
        scratch_shapes=[
            pltpu.VMEM((tq + w, nk), BF16), pltpu.VMEM((tq + w, nk), BF16),
            pltpu.VMEM((tq, SWA_HEADS * SWA_HEAD_DIM), BF16)],
        compiler_params=_params(1),
        name="swa_attn",
    )(sinks, q, k, k, v, v, x, gate, w_o, b_o, ln_g, ln_b)


def _rope_inv(rot_dim):
    return ROPE_THETA ** (-jnp.arange(0, rot_dim, 2, dtype=F32) / rot_dim)


def _mla_tables(positions):
    ang = _rope_inv(MLA_ROPE)[:, None] * positions.astype(F32)[None, :]
    return {"ckt": jnp.cos(ang), "skt": jnp.sin(ang)}


def _swa_tables(positions):
    inv = _rope_inv(SWA_ROT)
    z8 = jnp.zeros_like(inv)
    rest = jnp.zeros((SWA_HEAD_DIM - SWA_ROT,), F32)
    tile = LANES // SWA_HEAD_DIM
    pos = positions.astype(F32)[:, None]
    f_c = jnp.tile(jnp.concatenate([inv, inv, rest]), tile)[None, :]
    f_lo = jnp.tile(jnp.concatenate([inv, z8, rest]), tile)[None, :]
    f_hi = jnp.tile(jnp.concatenate([z8, inv, rest]), tile)[None, :]
    return {"c": jnp.cos(pos * f_c), "slo": -jnp.sin(pos * f_lo), "shi": jnp.sin(pos * f_hi)}


def _mla_weights(w_in, q_norm, w_q_b, kv_norm, w_kv_b, w_o):
    d = D_MODEL
    w_in_p = jnp.concatenate([w_in, jnp.zeros((d, 64), F32)], axis=1).astype(BF16)
    wkv = w_kv_b.reshape(MLA_KV_RANK, MLA_HEADS, MLA_NOPE + MLA_V)
    wk = wkv[:, :, :MLA_NOPE].reshape(MLA_KV_RANK, MLA_HEADS * MLA_NOPE).astype(BF16)
    wvt = wkv[:, :, MLA_NOPE:].reshape(MLA_KV_RANK, MLA_HEADS * MLA_V).T.astype(BF16)
    return {
        "w_in": w_in_p, "q_norm": q_norm.reshape(1, -1), "wqt": w_q_b.T.astype(BF16),
        "kv_norm": kv_norm.reshape(1, -1), "wk": wk, "wvt": wvt, "w_o": w_o.astype(BF16),
    }


def _swa_weights(w_qkv, b_qkv, w_o):
    nq = SWA_HEADS * SWA_HEAD_DIM
    nkv = SWA_KV_HEADS * SWA_HEAD_DIM

    def dup(t):
        lead = t.shape[:-1]
        t = t.reshape(lead + (SWA_KV_HEADS, 1, SWA_HEAD_DIM))
        t = jnp.broadcast_to(t, lead + (SWA_KV_HEADS, 2, SWA_HEAD_DIM))
        return t.reshape(lead + (2 * nkv,))

    def relayout(t):
        return jnp.concatenate([t[..., :nq], dup(t[..., nq:nq + nkv]), dup(t[..., nq + nkv:])], axis=-1)

    return {
        "w_qkv": relayout(w_qkv).astype(BF16),
        "b_qkv": relayout(b_qkv).reshape(1, -1),
        "w_o": w_o.astype(BF16),
    }


def kernel(x, c, positions, ada_w, ada_b, ln_mix_g, ln_mix_b, ln_ffn_g, ln_ffn_b, ffn_w_gate, ffn_w_up, ffn_w_down, mla_w_in, mla_q_norm, mla_w_q_b, mla_kv_norm, mla_w_kv_b, mla_w_o, swa_w_qkv, swa_b_qkv, swa_sinks, swa_w_o, swa_b_o):
    b, s, d = x.shape
    assert (b, s, d) == (1, SEQ, D_MODEL)
    xs = x.reshape(s, d)
    pos = positions.reshape(s)
    mod = _modulation(c, ada_w, ada_b)
    mla_tabs = _mla_tables(pos)
    swa_tabs = _swa_tables(pos)
    vec = lambda t: t.reshape(1, d)
    wg, wu, wd = ffn_w_gate.astype(BF16), ffn_w_up.astype(BF16), ffn_w_down.astype(BF16)
    for i in range(DEPTH):
        sh_m, sc_m, g_m, sh_f, sc_f, g_f = (mod[i, k] for k in range(6))
        j = i // 2
        mix = None
        if i % 2 == 0:
            w = _mla_weights(mla_w_in[j], mla_q_norm[j], mla_w_q_b[j], mla_kv_norm[j], mla_w_kv_b[j], mla_w_o[j])
            qt, k, vt = _mla_proj(xs, sc_m, sh_m, w, mla_tabs)
            o = _mla_attn(qt, k, vt)
            mix = (o, w["w_o"], g_m, vec(ln_mix_g[i]), vec(ln_mix_b[i]))
        else:
            w = _swa_weights(swa_w_qkv[j], swa_b_qkv[j], swa_w_o[j])
            q, k, v = _swa_proj(xs, sc_m, sh_m, w, swa_tabs)
            xs = _swa_attn(swa_sinks[j], q, k, v, xs, g_m, w["w_o"], vec(swa_b_o[j]),
                           vec(ln_mix_g[i]), vec(ln_mix_b[i]))
        xs = _ffn(xs, i, sc_f, sh_f, g_f, wg, wu, wd, vec(ln_ffn_g[i]), vec(ln_ffn_b[i]), mix=mix)
    return xs.reshape(b, s, d)
```

```python
import functools

import jax
import jax.numpy as jnp
from jax import lax
from jax.experimental import pallas as pl
from jax.experimental.pallas import tpu as pltpu

D_MODEL = 1024
SEQ = 16384
DEPTH = 4
ROPE_THETA = 500000.0
LN_EPS = 1e-5
RMS_EPS = 1e-6

MLA_HEADS = 8
MLA_NOPE = 128
MLA_ROPE = 64
MLA_V = 128
MLA_Q_RANK = 384
MLA_KV_RANK = 256
MLA_QK_PAD = 256

SWA_HEADS = 16
SWA_KV_HEADS = 4
SWA_HEAD_DIM = 64
SWA_WINDOW = 128
SWA_ROT = SWA_HEAD_DIM // 4

D_FF = ((8 * D_MODEL + 3 * 256 - 1) // (3 * 256)) * 256
DEEPNORM_ALPHA = (2 * DEPTH) ** 0.25

LANES = 128
VMEM_LIMIT = 56 * 1024 * 1024

ROW_BLOCK = 512
MLA_T = 512
MLA_UNROLL = 16
MLA_SBUFS = 4
MLA_ATTN_VMEM = 58 * 1024 * 1024
SWA_TQ = 512
SWA_SBUFS = 4
LOG2E = 1.4426950408889634
MLA_Q_SCALE = (MLA_NOPE + MLA_ROPE) ** -0.5 * LOG2E
SWA_Q_SCALE = SWA_HEAD_DIM ** -0.5 * LOG2E

BF16 = jnp.bfloat16
F32 = jnp.float32


def _dot(a, b):
    return jnp.dot(a, b, preferred_element_type=F32)


def _dot_nt(a, b):
    return lax.dot_general(a, b, (((1,), (1,)), ((), ())), preferred_element_type=F32)


def _layer_norm(z, g, b):
    mu = jnp.mean(z, axis=-1, keepdims=True)
    zc = z - mu
    var = jnp.mean(zc * zc, axis=-1, keepdims=True)
    return zc * lax.rsqrt(var + LN_EPS) * g + b


def _rms_norm(z, g):
    return z * lax.rsqrt(jnp.mean(z * z, axis=-1, keepdims=True) + RMS_EPS) * g


def _const_spec(shape):
    nd = len(shape)
    return pl.BlockSpec(shape, lambda *_: (0,) * nd, pipeline_mode=pl.Buffered(1))


def _params(n_grid):
    return pltpu.CompilerParams(
        dimension_semantics=("arbitrary",) * n_grid, vmem_limit_bytes=VMEM_LIMIT)


def _mod_kernel(c_ref, w_ref, b_ref, o_ref):
    c = c_ref[...]
    cond = (c * jax.nn.sigmoid(c)).astype(BF16)
    cond8 = jnp.broadcast_to(cond, (8, D_MODEL))
    y = _dot(cond8, w_ref[0].astype(BF16))
    o_ref[0] = y[0:1] + b_ref[0]


def _modulation(c, ada_w, ada_b):
    d = D_MODEL
    out = pl.pallas_call(
        _mod_kernel,
        out_shape=jax.ShapeDtypeStruct((DEPTH, 1, 6 * d), F32),
        grid=(DEPTH, 6),
        in_specs=[
            pl.BlockSpec((1, d), lambda i, k: (0, 0)),
            pl.BlockSpec((1, d, d), lambda i, k: (i, 0, k)),
            pl.BlockSpec((1, 1, d), lambda i, k: (i, 0, k)),
        ],
        out_specs=pl.BlockSpec((1, 1, d), lambda i, k: (i, 0, k)),
        compiler_params=_params(2),
        name="adaln_mod",
    )(c, ada_w, ada_b.reshape(DEPTH, 1, 6 * d))
    return out.reshape(DEPTH, 6, 1, d)


def _mla_proj_kernel(x_ref, sc_ref, sh_ref, w_in_ref, qn_ref, wqt_ref, kvn_ref, wk_ref, wvt_ref,
                     ckt_ref, skt_ref, qt_ref, k_ref, vt_ref):
    tm = x_ref.shape[0]
    h = (x_ref[...] * (1.0 + sc_ref[...]) + sh_ref[...]).astype(BF16)
    lat = _dot(h, w_in_ref[...])
    q_lat = lat[:, :MLA_Q_RANK]
    kv_lat = lat[:, MLA_Q_RANK:MLA_Q_RANK + MLA_KV_RANK]
    kr = lat[:, MLA_Q_RANK + MLA_KV_RANK:]

    qn = _rms_norm(q_lat, qn_ref[...]).astype(BF16)
    qt = _dot_nt(wqt_ref[...], qn) * MLA_Q_SCALE
    ck = ckt_ref[...]
    sk = skt_ref[...]
    hw = MLA_NOPE + MLA_ROPE
    half = MLA_ROPE // 2
    for hd in range(MLA_HEADS):
        r0 = hd * hw
        x1 = qt[r0 + MLA_NOPE:r0 + MLA_NOPE + half]
        x2 = qt[r0 + MLA_NOPE + half:r0 + hw]
        qt_ref[hd, 0:MLA_NOPE, :] = qt[r0:r0 + MLA_NOPE].astype(BF16)
        qt_ref[hd, MLA_NOPE:MLA_NOPE + half, :] = (x1 * ck - x2 * sk).astype(BF16)
        qt_ref[hd, MLA_NOPE + half:hw, :] = (x2 * ck + x1 * sk).astype(BF16)
        qt_ref[hd, hw:, :] = jnp.zeros((MLA_QK_PAD - hw, tm), BF16)

    kvn = _rms_norm(kv_lat, kvn_ref[...]).astype(BF16)
    kn = _dot(kvn, wk_ref[...])
    krt = kr.T
    x1 = krt[0:half]
    x2 = krt[half:MLA_ROPE]
    k_rope = jnp.concatenate(
        [x1 * ck - x2 * sk, x2 * ck + x1 * sk, krt[MLA_ROPE:]], axis=0).T.astype(BF16)
    for hd in range(MLA_HEADS):
        k_ref[:, hd * MLA_QK_PAD:hd * MLA_QK_PAD + MLA_NOPE] = kn[:, hd * MLA_NOPE:(hd + 1) * MLA_NOPE].astype(BF16)
        k_ref[:, hd * MLA_QK_PAD + MLA_NOPE:(hd + 1) * MLA_QK_PAD] = k_rope

    vt = _dot_nt(wvt_ref[...], kvn)
    for hd in range(MLA_HEADS):
        vt_ref[hd] = vt[hd * MLA_V:(hd + 1) * MLA_V].astype(BF16)


def _mla_proj(x, sc, sh, w, tabs):
    s, d = x.shape
    tm = ROW_BLOCK
    hk = MLA_HEADS * MLA_QK_PAD
    row = lambda i: (i, 0)
    col = lambda i: (0, i)
    return pl.pallas_call(
        _mla_proj_kernel,
        out_shape=(
            jax.ShapeDtypeStruct((MLA_HEADS, MLA_QK_PAD, s), BF16),
            jax.ShapeDtypeStruct((s, hk), BF16),
            jax.ShapeDtypeStruct((MLA_HEADS, MLA_V, s), BF16),
        ),
        grid=(s // tm,),
        in_specs=[
            pl.BlockSpec((tm, d), row),
            _const_spec((1, d)), _const_spec((1, d)),
            _const_spec(w["w_in"].shape), _const_spec(w["q_norm"].shape), _const_spec(w["wqt"].shape),
            _const_spec(w["kv_norm"].shape), _const_spec(w["wk"].shape), _const_spec(w["wvt"].shape),
            pl.BlockSpec((MLA_ROPE // 2, tm), col), pl.BlockSpec((MLA_ROPE // 2, tm), col),
        ],
        out_specs=(
            pl.BlockSpec((MLA_HEADS, MLA_QK_PAD, tm), lambda i: (0, 0, i)),
            pl.BlockSpec((tm, hk), row),
            pl.BlockSpec((MLA_HEADS, MLA_V, tm), lambda i: (0, 0, i)),
        ),
        compiler_params=_params(1),
        name="mla_proj",
    )(x, sc, sh, w["w_in"], w["q_norm"], w["wqt"], w["kv_norm"], w["wk"], w["wvt"],
      tabs[0], tabs[1])


def _mla_attn_kernel(qt_ref, k_ref, vt_ref, o_ref, *scratch):
    t = MLA_T
    nq = qt_ref.shape[2] // t
    n_full = nq * (nq - 1) // 2
    assert n_full % MLA_UNROLL == 0 and MLA_UNROLL % MLA_SBUFS == 0 and nq % MLA_SBUFS == 0
    bufs = scratch[:MLA_SBUFS]
    m_ref, l_ref, acc_ref = scratch[MLA_SBUFS:]
    m_ref[...] = jnp.full(m_ref.shape, -jnp.inf, F32)
    l_ref[...] = jnp.zeros(l_ref.shape, F32)
    acc_ref[...] = jnp.zeros(acc_ref.shape, F32)

    def scores(qi, j):
        q0 = pl.multiple_of(qi * t, t)
        k0 = pl.multiple_of(j * t, t)
        return _dot(k_ref[pl.ds(k0, t), :], qt_ref[0, :, pl.ds(q0, t)])

    def update(qi, j, st, diag):
        k0 = pl.multiple_of(j * t, t)
        if diag:
            key = lax.broadcasted_iota(jnp.int32, (t, t), 0)
            qry = lax.broadcasted_iota(jnp.int32, (t, t), 1)
            st = jnp.where(key <= qry, st, -jnp.inf)
        m_prev = m_ref[qi]
        m_new = jnp.maximum(m_prev, jnp.max(st, axis=0, keepdims=True))
        alpha = jnp.exp2(m_prev - m_new)
        p = jnp.exp2(st - m_new)
        l_new = alpha * l_ref[qi] + jnp.sum(p.reshape(t // 8, 8, t), axis=0)
        acc_new = alpha * acc_ref[qi] + _dot(vt_ref[0, :, pl.ds(k0, t)], p.astype(BF16))
        if diag:
            q0 = pl.multiple_of(qi * t, t)
            l = jnp.sum(l_new, axis=0, keepdims=True)
            o_ref[pl.ds(q0, t), :] = (acc_new / l).T.astype(BF16)
        else:
            m_ref[qi] = m_new
            l_ref[qi] = l_new
            acc_ref[qi] = acc_new

    bufs[0][...] = scores(1, 0)

    def full_body(_, carry):
        qi, j = carry
        for u in range(MLA_UNROLL):
            last = j + 1 == qi
            qi_n = jnp.where(last, qi + 1, qi)
            j_n = jnp.where(last, 0, j + 1)
            bufs[(u + 1) % MLA_SBUFS][...] = scores(jnp.minimum(qi_n, nq - 1), j_n)
            update(qi, j, bufs[u % MLA_SBUFS][...], False)
            qi, j = qi_n, j_n
        return qi, j

    lax.fori_loop(0, n_full // MLA_UNROLL, full_body, (jnp.int32(1), jnp.int32(0)))

    bufs[0][...] = scores(0, 0)

    def diag_body(i, carry):
        for u in range(MLA_SBUFS):
            qi = MLA_SBUFS * i + u
            nxt = jnp.minimum(qi + 1, nq - 1)
            bufs[(u + 1) % MLA_SBUFS][...] = scores(nxt, nxt)
            update(qi, qi, bufs[u % MLA_SBUFS][...], True)
        return carry

    lax.fori_loop(0, nq // MLA_SBUFS, diag_body, 0)


def _mla_attn(qt, k, vt):
    s = k.shape[0]
    t = MLA_T
    return pl.pallas_call(
        _mla_attn_kernel,
        out_shape=jax.ShapeDtypeStruct((s, MLA_HEADS * MLA_V), BF16),
        grid=(MLA_HEADS,),
        in_specs=[
            pl.BlockSpec((1, MLA_QK_PAD, s), lambda h: (h, 0, 0)),
            pl.BlockSpec((s, MLA_QK_PAD), lambda h: (0, h), pipeline_mode=pl.Buffered(1)),
            pl.BlockSpec((1, MLA_V, s), lambda h: (h, 0, 0)),
        ],
        out_specs=pl.BlockSpec((s, MLA_V), lambda h: (0, h)),
        scratch_shapes=[
            *[pltpu.VMEM((t, t), F32)] * MLA_SBUFS,
            pltpu.VMEM((s // t, 1, t), F32), pltpu.VMEM((s // t, 8, t), F32),
            pltpu.VMEM((s // t, MLA_V, t), F32)],
        compiler_params=pltpu.CompilerParams(
            dimension_semantics=("arbitrary",), vmem_limit_bytes=MLA_ATTN_VMEM),
        name="mla_attn",
    )(qt, k, vt)


def _ffn_kernel(*refs, with_mix):
    if with_mix:
        o_ref, wo_ref, gm_ref, lnmg_ref, lnmb_ref = refs[:5]
        refs = refs[5:]
    x_ref, sc_ref, sh_ref, g_ref, wg_ref, wu_ref, wd_ref, lng_ref, lnb_ref, y_ref = refs
    x = x_ref[...]
    if with_mix:
        z = DEEPNORM_ALPHA * x + gm_ref[...] * _dot(o_ref[...], wo_ref[...])
        x = _layer_norm(z, lnmg_ref[...], lnmb_ref[...])
    h = (x * (1.0 + sc_ref[...]) + sh_ref[...]).astype(BF16)
    gate = _dot(h, wg_ref[0])
    up = _dot(h, wu_ref[0])
    act = (gate * jax.nn.sigmoid(gate) * up).astype(BF16)
    y = _dot(act, wd_ref[0])
    z = DEEPNORM_ALPHA * x + g_ref[...] * y
    y_ref[...] = _layer_norm(z, lng_ref[...], lnb_ref[...])


def _ffn(x, layer, sc, sh, gate, wg, wu, wd, ln_g, ln_b, mix=None):
    s, d = x.shape
    tm = ROW_BLOCK
    row = lambda i: (i, 0)
    vec = _const_spec((1, d))
    stacked = lambda w: pl.BlockSpec((1,) + w.shape[1:], lambda i: (layer, 0, 0), pipeline_mode=pl.Buffered(1))
    mix_specs, mix_args = [], []
    if mix is not None:
        o, w_o = mix[0], mix[1]
        mix_specs = [pl.BlockSpec((tm, o.shape[1]), row), _const_spec(w_o.shape), vec, vec, vec]
        mix_args = list(mix)
    return pl.pallas_call(
        functools.partial(_ffn_kernel, with_mix=mix is not None),
        out_shape=jax.ShapeDtypeStruct((s, d), F32),
        grid=(s // tm,),
        in_specs=mix_specs + [
            pl.BlockSpec((tm, d), row), vec, vec, vec,
            stacked(wg), stacked(wu), stacked(wd), vec, vec,
        ],
        out_specs=pl.BlockSpec((tm, d), row),
        compiler_params=_params(1),
        name="mix_ffn" if mix is not None else "ffn",
    )(*mix_args, x, sc, sh, gate, wg, wu, wd, ln_g, ln_b)


def _rope_rows(xt, c, s):
    half = SWA_ROT // 2
    x1 = xt[0:half]
    x2 = xt[half:SWA_ROT]
    return jnp.concatenate([x1 * c - x2 * s, x2 * c + x1 * s, xt[SWA_ROT:]], axis=0)


def _swa_proj_kernel(x_ref, sc_ref, sh_ref, wt_ref, b_ref, ct_ref, st_ref, qt_ref, k_ref, vt_ref):
    h = (x_ref[...] * (1.0 + sc_ref[...]) + sh_ref[...]).astype(BF16)
    qkvt = _dot_nt(wt_ref[...], h) + b_ref[...]
    c = ct_ref[...]
    s = st_ref[...]
    hd = SWA_HEAD_DIM
    nq = SWA_HEADS * hd
    nk = SWA_KV_HEADS * LANES
    for i in range(SWA_HEADS):
        qt_ref[i * hd:(i + 1) * hd, :] = (_rope_rows(qkvt[i * hd:(i + 1) * hd], c, s) * SWA_Q_SCALE).astype(BF16)
    for g in range(SWA_KV_HEADS):
        r0 = nq + g * LANES
        kt = jnp.concatenate([_rope_rows(qkvt[r0:r0 + hd], c, s), qkvt[r0 + hd:r0 + LANES]], axis=0)
        k_ref[:, g * LANES:(g + 1) * LANES] = kt.T.astype(BF16)
    vt_ref[...] = qkvt[nq + nk:].astype(BF16)


def _swa_proj(x, sc, sh, w, tabs):
    s, d = x.shape
    tm = ROW_BLOCK
    nq = SWA_HEADS * SWA_HEAD_DIM
    nk = SWA_KV_HEADS * LANES
    nv = SWA_KV_HEADS * SWA_HEAD_DIM
    row = lambda i: (i, 0)
    col = lambda i: (0, i)
    return pl.pallas_call(
        _swa_proj_kernel,
        out_shape=(
            jax.ShapeDtypeStruct((nq, s), BF16),
            jax.ShapeDtypeStruct((s, nk), BF16),
            jax.ShapeDtypeStruct((nv, s), BF16),
        ),
        grid=(s // tm,),
        in_specs=[
            pl.BlockSpec((tm, d), row),
            _const_spec((1, d)), _const_spec((1, d)),
            _const_spec(w["w_qkv_t"].shape), _const_spec(w["b_qkv_t"].shape),
            pl.BlockSpec((SWA_ROT // 2, tm), col), pl.BlockSpec((SWA_ROT // 2, tm), col),
        ],
        out_specs=(pl.BlockSpec((nq, tm), col), pl.BlockSpec((tm, nk), row), pl.BlockSpec((nv, tm), col)),
        compiler_params=_params(1),
        name="swa_proj",
    )(x, sc, sh, w["w_qkv_t"], w["b_qkv_t"], tabs[0], tabs[1])


def _swa_attn_kernel(sinks_ref, qt_ref, kp_ref, kc_ref, vtp_ref, vtc_ref, x_ref, g_ref, w_ref, b_ref,
                     lng_ref, lnb_ref, y_ref, kwin_ref, vtwin_ref, ot_ref, *s_refs):
    i = pl.program_id(0)
    w = SWA_WINDOW
    tq = SWA_TQ
    hd = SWA_HEAD_DIM
    grp = SWA_HEADS // SWA_KV_HEADS
    kwin_ref[0:w, :] = kp_ref[...]
    kwin_ref[w:, :] = kc_ref[...]
    vtwin_ref[:, 0:w] = vtp_ref[...]
    vtwin_ref[:, w:] = vtc_ref[...]

    key = lax.broadcasted_iota(jnp.int32, (2 * w, w), 0)
    qry = lax.broadcasted_iota(jnp.int32, (2 * w, w), 1)
    band = (key > qry) & (key <= qry + w)

    def scores(b, g):
        r0 = b * w
        k_g = kwin_ref[r0:r0 + 2 * w, g * LANES:g * LANES + hd]
        q_g = jnp.concatenate(
            [qt_ref[h * hd:(h + 1) * hd, r0:r0 + w] for h in range(g * grp, (g + 1) * grp)], axis=1)
        return _dot(k_g, q_g)

    tasks = [(b, g) for b in range(tq // w) for g in range(SWA_KV_HEADS)]
    s_refs[0][...] = scores(*tasks[0])
    for n, (b, g) in enumerate(tasks):
        if n + 1 < len(tasks):
            s_refs[(n + 1) % len(s_refs)][...] = scores(*tasks[n + 1])
        st = s_refs[n % len(s_refs)][...]
        r0 = b * w
        heads = range(g * grp, (g + 1) * grp)
        if g == 0:
            first_key = jnp.where(i * (tq // w) + b > 0, 0, w)
            neg = jnp.where(band & (key >= first_key), 0.0, -jnp.inf)
            neg = jnp.concatenate([neg] * grp, axis=1)
        st = st + neg
        sink = jnp.concatenate(
            [jnp.full((1, w), sinks_ref[h] * LOG2E, F32) for h in heads], axis=1)
        m = jnp.maximum(jnp.max(st, axis=0, keepdims=True), sink)
        p = jnp.exp2(st - m)
        denom = jnp.sum(p, axis=0, keepdims=True) + jnp.exp2(sink - m)
        vt_g = vtwin_ref[g * hd:(g + 1) * hd, r0:r0 + 2 * w]
        ot = _dot(vt_g, p.astype(BF16)) / denom
        for c, h in enumerate(heads):
            ot_ref[h * hd:(h + 1) * hd, r0:r0 + w] = ot[:, c * w:(c + 1) * w]

    o = ot_ref[...].T.astype(BF16)
    y = _dot(o, w_ref[...]) + b_ref[...]
    z = DEEPNORM_ALPHA * x_ref[...] + g_ref[...] * y
    y_ref[...] = _layer_norm(z, lng_ref[...], lnb_ref[...])


def _swa_attn(sinks, qt, k, vt, x, gate, w_o, b_o, ln_g, ln_b):
    s, d = x.shape
    tq = SWA_TQ
    w = SWA_WINDOW
    nq = qt.shape[0]
    nk = k.shape[1]
    nv = vt.shape[0]
    r = tq // w
    row = lambda i: (i, 0)
    col = lambda i: (0, i)
    prev_row = lambda i: (jnp.maximum(i * r - 1, 0), 0)
    prev_col = lambda i: (0, jnp.maximum(i * r - 1, 0))
    return pl.pallas_call(
        _swa_attn_kernel,
        out_shape=jax.ShapeDtypeStruct((s, d), F32),
        grid=(s // tq,),
        in_specs=[
            pl.BlockSpec(memory_space=pltpu.SMEM),
            pl.BlockSpec((nq, tq), col),
            pl.BlockSpec((w, nk), prev_row), pl.BlockSpec((tq, nk), row),
            pl.BlockSpec((nv, w), prev_col), pl.BlockSpec((nv, tq), col),
            pl.BlockSpec((tq, d), row),
            _const_spec((1, d)), _const_spec(w_o.shape), _const_spec((1, d)),
            _const_spec((1, d)), _const_spec((1, d)),
        ],
        out_specs=pl.BlockSpec((tq, d), row),
        scratch_shapes=[
            pltpu.VMEM((tq + w, nk), BF16), pltpu.VMEM((nv, tq + w), BF16),
            pltpu.VMEM((nq, tq), F32),
            *[pltpu.VMEM((2 * w, w * SWA_HEADS // SWA_KV_HEADS), F32)] * SWA_SBUFS],
        compiler_params=_params(1),
        name="swa_attn",
    )(sinks, qt, k, k, vt, vt, x, gate, w_o, b_o, ln_g, ln_b)


def _rope_inv(rot_dim):
    return ROPE_THETA ** (-jnp.arange(0, rot_dim, 2, dtype=F32) / rot_dim)


def _rope_tables(positions, rot_dim):
    ang = _rope_inv(rot_dim)[:, None] * positions.astype(F32)[None, :]
    return jnp.cos(ang), jnp.sin(ang)


def _mla_weights(w_in, q_norm, w_q_b, kv_norm, w_kv_b, w_o):
    d = D_MODEL
    w_in_p = jnp.concatenate([w_in, jnp.zeros((d, 64), F32)], axis=1).astype(BF16)
    wkv = w_kv_b.reshape(MLA_KV_RANK, MLA_HEADS, MLA_NOPE + MLA_V)
    wk = wkv[:, :, :MLA_NOPE].reshape(MLA_KV_RANK, MLA_HEADS * MLA_NOPE).astype(BF16)
    wvt = wkv[:, :, MLA_NOPE:].reshape(MLA_KV_RANK, MLA_HEADS * MLA_V).T.astype(BF16)
    return {
        "w_in": w_in_p, "q_norm": q_norm.reshape(1, -1), "wqt": w_q_b.T.astype(BF16),
        "kv_norm": kv_norm.reshape(1, -1), "wk": wk, "wvt": wvt, "w_o": w_o.astype(BF16),
    }


def _swa_weights(w_qkv, b_qkv, w_o):
    nq = SWA_HEADS * SWA_HEAD_DIM
    nkv = SWA_KV_HEADS * SWA_HEAD_DIM

    def pad_heads(t):
        lead = t.shape[:-1]
        t = t.reshape(lead + (SWA_KV_HEADS, SWA_HEAD_DIM))
        t = jnp.concatenate([t, jnp.zeros_like(t)], axis=-1)
        return t.reshape(lead + (SWA_KV_HEADS * LANES,))

    def relayout(t):
        return jnp.concatenate([t[..., :nq], pad_heads(t[..., nq:nq + nkv]), t[..., nq + nkv:]], axis=-1)

    return {
        "w_qkv_t": relayout(w_qkv).T.astype(BF16),
        "b_qkv_t": relayout(b_qkv).reshape(-1, 1),
        "w_o": w_o.astype(BF16),
    }


def kernel(x, c, positions, ada_w, ada_b, ln_mix_g, ln_mix_b, ln_ffn_g, ln_ffn_b, ffn_w_gate, ffn_w_up, ffn_w_down, mla_w_in, mla_q_norm, mla_w_q_b, mla_kv_norm, mla_w_kv_b, mla_w_o, swa_w_qkv, swa_b_qkv, swa_sinks, swa_w_o, swa_b_o):
    b, s, d = x.shape
    assert (b, s, d) == (1, SEQ, D_MODEL)
    xs = x.reshape(s, d)
    pos = positions.reshape(s)
    mod = _modulation(c, ada_w, ada_b)
    mla_tabs = _rope_tables(pos, MLA_ROPE)
    swa_tabs = _rope_tables(pos, SWA_ROT)
    vec = lambda t: t.reshape(1, d)
    wg, wu, wd = ffn_w_gate.astype(BF16), ffn_w_up.astype(BF16), ffn_w_down.astype(BF16)
    for i in range(DEPTH):
        sh_m, sc_m, g_m, sh_f, sc_f, g_f = (mod[i, k] for k in range(6))
        j = i // 2
        mix = None
        if i % 2 == 0:
            w = _mla_weights(mla_w_in[j], mla_q_norm[j], mla_w_q_b[j], mla_kv_norm[j], mla_w_kv_b[j], mla_w_o[j])
            qt, k, vt = _mla_proj(xs, sc_m, sh_m, w, mla_tabs)
            o = _mla_attn(qt, k, vt)
            mix = (o, w["w_o"], g_m, vec(ln_mix_g[i]), vec(ln_mix_b[i]))
        else:
            w = _swa_weights(swa_w_qkv[j], swa_b_qkv[j], swa_w_o[j])
            qt, k, vt = _swa_proj(xs, sc_m, sh_m, w, swa_tabs)
            xs = _swa_attn(swa_sinks[j], qt, k, vt, xs, g_m, w["w_o"], vec(swa_b_o[j]),
                           vec(ln_mix_g[i]), vec(ln_mix_b[i]))
        xs = _ffn(xs, i, sc_f, sh_f, g_f, wg, wu, wd, vec(ln_ffn_g[i]), vec(ln_ffn_b[i]), mix=mix)
    return xs.reshape(b, s, d)
```

```python
import functools

import jax
import jax.numpy as jnp
from jax import lax
from jax.experimental import pallas as pl
from jax.experimental.pallas import tpu as pltpu

D_MODEL = 1024
SEQ = 16384
DEPTH = 4
ROPE_THETA = 500000.0
LN_EPS = 1e-5
RMS_EPS = 1e-6

MLA_HEADS = 8
MLA_NOPE = 128
MLA_ROPE = 64
MLA_V = 128
MLA_Q_RANK = 384
MLA_KV_RANK = 256
MLA_QK_PAD = 256

SWA_HEADS = 16
SWA_KV_HEADS = 4
SWA_HEAD_DIM = 64
SWA_WINDOW = 128
SWA_ROT = SWA_HEAD_DIM // 4

D_FF = ((8 * D_MODEL + 3 * 256 - 1) // (3 * 256)) * 256
DEEPNORM_ALPHA = (2 * DEPTH) ** 0.25

LANES = 128
VMEM_LIMIT = 56 * 1024 * 1024

ROW_BLOCK = 512
MLA_T = 512
MLA_UNROLL = 16
MLA_SBUFS = 4
MLA_ATTN_VMEM = 58 * 1024 * 1024
SWA_TQ = 512
SWA_SBUFS = 4
LOG2E = 1.4426950408889634
MLA_Q_SCALE = (MLA_NOPE + MLA_ROPE) ** -0.5 * LOG2E
SWA_Q_SCALE = SWA_HEAD_DIM ** -0.5 * LOG2E

BF16 = jnp.bfloat16
F32 = jnp.float32


def _dot(a, b):
    return jnp.dot(a, b, preferred_element_type=F32)


def _dot_nt(a, b):
    return lax.dot_general(a, b, (((1,), (1,)), ((), ())), preferred_element_type=F32)


def _layer_norm(z, g, b):
    mu = jnp.mean(z, axis=-1, keepdims=True)
    zc = z - mu
    var = jnp.mean(zc * zc, axis=-1, keepdims=True)
    return zc * lax.rsqrt(var + LN_EPS) * g + b


def _rms_norm(z, g):
    return z * lax.rsqrt(jnp.mean(z * z, axis=-1, keepdims=True) + RMS_EPS) * g


def _const_spec(shape):
    nd = len(shape)
    return pl.BlockSpec(shape, lambda *_: (0,) * nd, pipeline_mode=pl.Buffered(1))


def _params(n_grid):
    return pltpu.CompilerParams(
        dimension_semantics=("arbitrary",) * n_grid, vmem_limit_bytes=VMEM_LIMIT)


def _mod_kernel(c_ref, w_ref, b_ref, o_ref):
    c = c_ref[...]
    cond = (c * jax.nn.sigmoid(c)).astype(BF16)
    cond8 = jnp.broadcast_to(cond, (8, D_MODEL))
    y = _dot(cond8, w_ref[0].astype(BF16))
    o_ref[0] = y[0:1] + b_ref[0]


def _modulation(c, ada_w, ada_b):
    d = D_MODEL
    out = pl.pallas_call(
        _mod_kernel,
        out_shape=jax.ShapeDtypeStruct((DEPTH, 1, 6 * d), F32),
        grid=(DEPTH, 6),
        in_specs=[
            pl.BlockSpec((1, d), lambda i, k: (0, 0)),
            pl.BlockSpec((1, d, d), lambda i, k: (i, 0, k)),
            pl.BlockSpec((1, 1, d), lambda i, k: (i, 0, k)),
        ],
        out_specs=pl.BlockSpec((1, 1, d), lambda i, k: (i, 0, k)),
        compiler_params=_params(2),
        name="adaln_mod",
    )(c, ada_w, ada_b.reshape(DEPTH, 1, 6 * d))
    return out.reshape(DEPTH, 6, 1, d)


def _mla_proj_kernel(x_ref, sc_ref, sh_ref, w_in_ref, qn_ref, wqt_ref, kvn_ref, wk_ref, wvt_ref,
                     ckt_ref, skt_ref, qt_ref, k_ref, vt_ref):
    tm = x_ref.shape[0]
    h = (x_ref[...] * (1.0 + sc_ref[...]) + sh_ref[...]).astype(BF16)
    lat = _dot(h, w_in_ref[...])
    q_lat = lat[:, :MLA_Q_RANK]
    kv_lat = lat[:, MLA_Q_RANK:MLA_Q_RANK + MLA_KV_RANK]
    kr = lat[:, MLA_Q_RANK + MLA_KV_RANK:]

    qn = _rms_norm(q_lat, qn_ref[...]).astype(BF16)
    qt = _dot_nt(wqt_ref[...], qn) * MLA_Q_SCALE
    ck = ckt_ref[...]
    sk = skt_ref[...]
    hw = MLA_NOPE + MLA_ROPE
    half = MLA_ROPE // 2
    for hd in range(MLA_HEADS):
        r0 = hd * hw
        x1 = qt[r0 + MLA_NOPE:r0 + MLA_NOPE + half]
        x2 = qt[r0 + MLA_NOPE + half:r0 + hw]
        qt_ref[hd, 0:MLA_NOPE, :] = qt[r0:r0 + MLA_NOPE].astype(BF16)
        qt_ref[hd, MLA_NOPE:MLA_NOPE + half, :] = (x1 * ck - x2 * sk).astype(BF16)
        qt_ref[hd, MLA_NOPE + half:hw, :] = (x2 * ck + x1 * sk).astype(BF16)
        qt_ref[hd, hw:, :] = jnp.zeros((MLA_QK_PAD - hw, tm), BF16)

    kvn = _rms_norm(kv_lat, kvn_ref[...]).astype(BF16)
    kn = _dot(kvn, wk_ref[...])
    krt = kr.T
    x1 = krt[0:half]
    x2 = krt[half:MLA_ROPE]
    k_rope = jnp.concatenate(
        [x1 * ck - x2 * sk, x2 * ck + x1 * sk, krt[MLA_ROPE:]], axis=0).T.astype(BF16)
    for hd in range(MLA_HEADS):
        k_ref[:, hd * MLA_QK_PAD:hd * MLA_QK_PAD + MLA_NOPE] = kn[:, hd * MLA_NOPE:(hd + 1) * MLA_NOPE].astype(BF16)
        k_ref[:, hd * MLA_QK_PAD + MLA_NOPE:(hd + 1) * MLA_QK_PAD] = k_rope

    vt = _dot_nt(wvt_ref[...], kvn)
    for hd in range(MLA_HEADS):
        vt_ref[hd] = vt[hd * MLA_V:(hd + 1) * MLA_V].astype(BF16)


def _mla_proj(x, sc, sh, w, tabs):
    s, d = x.shape
    tm = ROW_BLOCK
    hk = MLA_HEADS * MLA_QK_PAD
    row = lambda i: (i, 0)
    col = lambda i: (0, i)
    return pl.pallas_call(
        _mla_proj_kernel,
        out_shape=(
            jax.ShapeDtypeStruct((MLA_HEADS, MLA_QK_PAD, s), BF16),
            jax.ShapeDtypeStruct((s, hk), BF16),
            jax.ShapeDtypeStruct((MLA_HEADS, MLA_V, s), BF16),
        ),
        grid=(s // tm,),
        in_specs=[
            pl.BlockSpec((tm, d), row),
            _const_spec((1, d)), _const_spec((1, d)),
            _const_spec(w["w_in"].shape), _const_spec(w["q_norm"].shape), _const_spec(w["wqt"].shape),
            _const_spec(w["kv_norm"].shape), _const_spec(w["wk"].shape), _const_spec(w["wvt"].shape),
            pl.BlockSpec((MLA_ROPE // 2, tm), col), pl.BlockSpec((MLA_ROPE // 2, tm), col),
        ],
        out_specs=(
            pl.BlockSpec((MLA_HEADS, MLA_QK_PAD, tm), lambda i: (0, 0, i)),
            pl.BlockSpec((tm, hk), row),
            pl.BlockSpec((MLA_HEADS, MLA_V, tm), lambda i: (0, 0, i)),
        ),
        compiler_params=_params(1),
        name="mla_proj",
    )(x, sc, sh, w["w_in"], w["q_norm"], w["wqt"], w["kv_norm"], w["wk"], w["wvt"],
      tabs[0], tabs[1])


def _mla_attn_kernel(qt_ref, k_ref, vt_ref, o_ref, *scratch):
    t = MLA_T
    nq = qt_ref.shape[2] // t
    n_full = nq * (nq - 1) // 2
    assert n_full % MLA_UNROLL == 0 and MLA_UNROLL % MLA_SBUFS == 0 and nq % MLA_SBUFS == 0
    bufs = scratch[:MLA_SBUFS]
    mbufs = scratch[MLA_SBUFS:2 * MLA_SBUFS]
    m_ref, l_ref, acc_ref = scratch[2 * MLA_SBUFS:]
    m_ref[...] = jnp.full(m_ref.shape, -jnp.inf, F32)
    l_ref[...] = jnp.zeros(l_ref.shape, F32)
    acc_ref[...] = jnp.zeros(acc_ref.shape, F32)

    def scores(slot, qi, j, diag):
        q0 = pl.multiple_of(qi * t, t)
        k0 = pl.multiple_of(j * t, t)
        st = _dot(k_ref[pl.ds(k0, t), :], qt_ref[0, :, pl.ds(q0, t)])
        if diag:
            key = lax.broadcasted_iota(jnp.int32, (t, t), 0)
            qry = lax.broadcasted_iota(jnp.int32, (t, t), 1)
            st = jnp.where(key <= qry, st, -jnp.inf)
        bufs[slot][...] = st
        mbufs[slot][...] = jnp.max(st.reshape(t // 8, 8, t), axis=0)

    def update(slot, qi, j, diag):
        k0 = pl.multiple_of(j * t, t)
        st = bufs[slot][...]
        m_prev = m_ref[qi]
        m_new = jnp.maximum(m_prev, jnp.max(mbufs[slot][...], axis=0, keepdims=True))
        alpha = jnp.exp2(m_prev - m_new)
        p = jnp.exp2(st - m_new)
        l_new = alpha * l_ref[qi] + jnp.sum(p.reshape(t // 8, 8, t), axis=0)
        acc_new = alpha * acc_ref[qi] + _dot(vt_ref[0, :, pl.ds(k0, t)], p.astype(BF16))
        if diag:
            q0 = pl.multiple_of(qi * t, t)
            l = jnp.sum(l_new, axis=0, keepdims=True)
            o_ref[pl.ds(q0, t), :] = (acc_new / l).T.astype(BF16)
        else:
            m_ref[qi] = m_new
            l_ref[qi] = l_new
            acc_ref[qi] = acc_new

    scores(0, 1, 0, False)

    def full_body(_, carry):
        qi, j = carry
        for u in range(MLA_UNROLL):
            last = j + 1 == qi
            qi_n = jnp.where(last, qi + 1, qi)
            j_n = jnp.where(last, 0, j + 1)
            scores((u + 1) % MLA_SBUFS, jnp.minimum(qi_n, nq - 1), j_n, False)
            update(u % MLA_SBUFS, qi, j, False)
            qi, j = qi_n, j_n
        return qi, j

    lax.fori_loop(0, n_full // MLA_UNROLL, full_body, (jnp.int32(1), jnp.int32(0)))

    scores(0, 0, 0, True)

    def diag_body(i, carry):
        for u in range(MLA_SBUFS):
            qi = MLA_SBUFS * i + u
            nxt = jnp.minimum(qi + 1, nq - 1)
            scores((u + 1) % MLA_SBUFS, nxt, nxt, True)
            update(u % MLA_SBUFS, qi, qi, True)
        return carry

    lax.fori_loop(0, nq // MLA_SBUFS, diag_body, 0)


def _mla_attn(qt, k, vt):
    s = k.shape[0]
    t = MLA_T
    return pl.pallas_call(
        _mla_attn_kernel,
        out_shape=jax.ShapeDtypeStruct((s, MLA_HEADS * MLA_V), BF16),
        grid=(MLA_HEADS,),
        in_specs=[
            pl.BlockSpec((1, MLA_QK_PAD, s), lambda h: (h, 0, 0)),
            pl.BlockSpec((s, MLA_QK_PAD), lambda h: (0, h), pipeline_mode=pl.Buffered(1)),
            pl.BlockSpec((1, MLA_V, s), lambda h: (h, 0, 0)),
        ],
        out_specs=pl.BlockSpec((s, MLA_V), lambda h: (0, h)),
        scratch_shapes=[
            *[pltpu.VMEM((t, t), F32)] * MLA_SBUFS,
            *[pltpu.VMEM((8, t), F32)] * MLA_SBUFS,
            pltpu.VMEM((s // t, 1, t), F32), pltpu.VMEM((s // t, 8, t), F32),
            pltpu.VMEM((s // t, MLA_V, t), F32)],
        compiler_params=pltpu.CompilerParams(
            dimension_semantics=("arbitrary",), vmem_limit_bytes=MLA_ATTN_VMEM),
        name="mla_attn",
    )(qt, k, vt)


def _ffn_kernel(*refs, with_mix):
    if with_mix:
        o_ref, wo_ref, gm_ref, lnmg_ref, lnmb_ref = refs[:5]
        refs = refs[5:]
    x_ref, sc_ref, sh_ref, g_ref, wg_ref, wu_ref, wd_ref, lng_ref, lnb_ref, y_ref = refs
    x = x_ref[...]
    if with_mix:
        z = DEEPNORM_ALPHA * x + gm_ref[...] * _dot(o_ref[...], wo_ref[...])
        x = _layer_norm(z, lnmg_ref[...], lnmb_ref[...])
    h = (x * (1.0 + sc_ref[...]) + sh_ref[...]).astype(BF16)
    gate = _dot(h, wg_ref[0])
    up = _dot(h, wu_ref[0])
    act = (gate * jax.nn.sigmoid(gate) * up).astype(BF16)
    y = _dot(act, wd_ref[0])
    z = DEEPNORM_ALPHA * x + g_ref[...] * y
    y_ref[...] = _layer_norm(z, lng_ref[...], lnb_ref[...])


def _ffn(x, layer, sc, sh, gate, wg, wu, wd, ln_g, ln_b, mix=None):
    s, d = x.shape
    tm = ROW_BLOCK
    row = lambda i: (i, 0)
    vec = _const_spec((1, d))
    stacked = lambda w: pl.BlockSpec((1,) + w.shape[1:], lambda i: (layer, 0, 0), pipeline_mode=pl.Buffered(1))
    mix_specs, mix_args = [], []
    if mix is not None:
        o, w_o = mix[0], mix[1]
        mix_specs = [pl.BlockSpec((tm, o.shape[1]), row), _const_spec(w_o.shape), vec, vec, vec]
        mix_args = list(mix)
    return pl.pallas_call(
        functools.partial(_ffn_kernel, with_mix=mix is not None),
        out_shape=jax.ShapeDtypeStruct((s, d), F32),
        grid=(s // tm,),
        in_specs=mix_specs + [
            pl.BlockSpec((tm, d), row), vec, vec, vec,
            stacked(wg), stacked(wu), stacked(wd), vec, vec,
        ],
        out_specs=pl.BlockSpec((tm, d), row),
        compiler_params=_params(1),
        name="mix_ffn" if mix is not None else "ffn",
    )(*mix_args, x, sc, sh, gate, wg, wu, wd, ln_g, ln_b)


def _rope_rows(xt, c, s):
    half = SWA_ROT // 2
    x1 = xt[0:half]
    x2 = xt[half:SWA_ROT]
    return jnp.concatenate([x1 * c - x2 * s, x2 * c + x1 * s, xt[SWA_ROT:]], axis=0)


def _swa_proj_kernel(x_ref, sc_ref, sh_ref, wt_ref, b_ref, ct_ref, st_ref, qt_ref, k_ref, vt_ref):
    h = (x_ref[...] * (1.0 + sc_ref[...]) + sh_ref[...]).astype(BF16)
    qkvt = _dot_nt(wt_ref[...], h) + b_ref[...]
    c = ct_ref[...]
    s = st_ref[...]
    hd = SWA_HEAD_DIM
    nq = SWA_HEADS * hd
    nk = SWA_KV_HEADS * LANES
    for i in range(SWA_HEADS):
        qt_ref[i * hd:(i + 1) * hd, :] = (_rope_rows(qkvt[i * hd:(i + 1) * hd], c, s) * SWA_Q_SCALE).astype(BF16)
    for g in range(SWA_KV_HEADS):
        r0 = nq + g * LANES
        kt = jnp.concatenate([_rope_rows(qkvt[r0:r0 + hd], c, s), qkvt[r0 + hd:r0 + LANES]], axis=0)
        k_ref[:, g * LANES:(g + 1) * LANES] = kt.T.astype(BF16)
    vt_ref[...] = qkvt[nq + nk:].astype(BF16)


def _swa_proj(x, sc, sh, w, tabs):
    s, d = x.shape
    tm = ROW_BLOCK
    nq = SWA_HEADS * SWA_HEAD_DIM
    nk = SWA_KV_HEADS * LANES
    nv = SWA_KV_HEADS * SWA_HEAD_DIM
    row = lambda i: (i, 0)
    col = lambda i: (0, i)
    return pl.pallas_call(
        _swa_proj_kernel,
        out_shape=(
            jax.ShapeDtypeStruct((nq, s), BF16),
            jax.ShapeDtypeStruct((s, nk), BF16),
            jax.ShapeDtypeStruct((nv, s), BF16),
        ),
        grid=(s // tm,),
        in_specs=[
            pl.BlockSpec((tm, d), row),
            _const_spec((1, d)), _const_spec((1, d)),
            _const_spec(w["w_qkv_t"].shape), _const_spec(w["b_qkv_t"].shape),
            pl.BlockSpec((SWA_ROT // 2, tm), col), pl.BlockSpec((SWA_ROT // 2, tm), col),
        ],
        out_specs=(pl.BlockSpec((nq, tm), col), pl.BlockSpec((tm, nk), row), pl.BlockSpec((nv, tm), col)),
        compiler_params=_params(1),
        name="swa_proj",
    )(x, sc, sh, w["w_qkv_t"], w["b_qkv_t"], tabs[0], tabs[1])


def _swa_attn_kernel(sinks_ref, qt_ref, kp_ref, kc_ref, vtp_ref, vtc_ref, x_ref, g_ref, w_ref, b_ref,
                     lng_ref, lnb_ref, y_ref, kwin_ref, vtwin_ref, ot_ref, *s_refs):
    i = pl.program_id(0)
    w = SWA_WINDOW
    tq = SWA_TQ
    hd = SWA_HEAD_DIM
    grp = SWA_HEADS // SWA_KV_HEADS
    kwin_ref[0:w, :] = kp_ref[...]
    kwin_ref[w:, :] = kc_ref[...]
    vtwin_ref[:, 0:w] = vtp_ref[...]
    vtwin_ref[:, w:] = vtc_ref[...]

    key = lax.broadcasted_iota(jnp.int32, (2 * w, w), 0)
    qry = lax.broadcasted_iota(jnp.int32, (2 * w, w), 1)
    band = (key > qry) & (key <= qry + w)

    def scores(b, g):
        r0 = b * w
        k_g = kwin_ref[r0:r0 + 2 * w, g * LANES:g * LANES + hd]
        q_g = jnp.concatenate(
            [qt_ref[h * hd:(h + 1) * hd, r0:r0 + w] for h in range(g * grp, (g + 1) * grp)], axis=1)
        return _dot(k_g, q_g)

    tasks = [(b, g) for b in range(tq // w) for g in range(SWA_KV_HEADS)]
    s_refs[0][...] = scores(*tasks[0])
    for n, (b, g) in enumerate(tasks):
        if n + 1 < len(tasks):
            s_refs[(n + 1) % len(s_refs)][...] = scores(*tasks[n + 1])
        st = s_refs[n % len(s_refs)][...]
        r0 = b * w
        heads = range(g * grp, (g + 1) * grp)
        if g == 0:
            first_key = jnp.where(i * (tq // w) + b > 0, 0, w)
            neg = jnp.where(band & (key >= first_key), 0.0, -jnp.inf)
            neg = jnp.concatenate([neg] * grp, axis=1)
        st = st + neg
        sink = jnp.concatenate(
            [jnp.full((1, w), sinks_ref[h] * LOG2E, F32) for h in heads], axis=1)
        m = jnp.maximum(jnp.max(st, axis=0, keepdims=True), sink)
        p = jnp.exp2(st - m)
        denom = jnp.sum(p, axis=0, keepdims=True) + jnp.exp2(sink - m)
        vt_g = vtwin_ref[g * hd:(g + 1) * hd, r0:r0 + 2 * w]
        ot = _dot(vt_g, p.astype(BF16)) / denom
        for c, h in enumerate(heads):
            ot_ref[h * hd:(h + 1) * hd, r0:r0 + w] = ot[:, c * w:(c + 1) * w]

    o = ot_ref[...].T.astype(BF16)
    y = _dot(o, w_ref[...]) + b_ref[...]
    z = DEEPNORM_ALPHA * x_ref[...] + g_ref[...] * y
    y_ref[...] = _layer_norm(z, lng_ref[...], lnb_ref[...])


def _swa_attn(sinks, qt, k, vt, x, gate, w_o, b_o, ln_g, ln_b):
    s, d = x.shape
    tq = SWA_TQ
    w = SWA_WINDOW
    nq = qt.shape[0]
    nk = k.shape[1]
    nv = vt.shape[0]
    r = tq // w
    row = lambda i: (i, 0)
    col = lambda i: (0, i)
    prev_row = lambda i: (jnp.maximum(i * r - 1, 0), 0)
    prev_col = lambda i: (0, jnp.maximum(i * r - 1, 0))
    return pl.pallas_call(
        _swa_attn_kernel,
        out_shape=jax.ShapeDtypeStruct((s, d), F32),
        grid=(s // tq,),
        in_specs=[
            pl.BlockSpec(memory_space=pltpu.SMEM),
            pl.BlockSpec((nq, tq), col),
            pl.BlockSpec((w, nk), prev_row), pl.BlockSpec((tq, nk), row),
            pl.BlockSpec((nv, w), prev_col), pl.BlockSpec((nv, tq), col),
            pl.BlockSpec((tq, d), row),
            _const_spec((1, d)), _const_spec(w_o.shape), _const_spec((1, d)),
            _const_spec((1, d)), _const_spec((1, d)),
        ],
        out_specs=pl.BlockSpec((tq, d), row),
        scratch_shapes=[
            pltpu.VMEM((tq + w, nk), BF16), pltpu.VMEM((nv, tq + w), BF16),
            pltpu.VMEM((nq, tq), F32),
            *[pltpu.VMEM((2 * w, w * SWA_HEADS // SWA_KV_HEADS), F32)] * SWA_SBUFS],
        compiler_params=_params(1),
        name="swa_attn",
    )(sinks, qt, k, k, vt, vt, x, gate, w_o, b_o, ln_g, ln_b)


def _rope_inv(rot_dim):
    return ROPE_THETA ** (-jnp.arange(0, rot_dim, 2, dtype=F32) / rot_dim)


def _rope_tables(positions, rot_dim):
    ang = _rope_inv(rot_dim)[:, None] * positions.astype(F32)[None, :]
    return jnp.cos(ang), jnp.sin(ang)


def _mla_weights(w_in, q_norm, w_q_b, kv_norm, w_kv_b, w_o):
    d = D_MODEL
    w_in_p = jnp.concatenate([w_in, jnp.zeros((d, 64), F32)], axis=1).astype(BF16)
    wkv = w_kv_b.reshape(MLA_KV_RANK, MLA_HEADS, MLA_NOPE + MLA_V)
    wk = wkv[:, :, :MLA_NOPE].reshape(MLA_KV_RANK, MLA_HEADS * MLA_NOPE).astype(BF16)
    wvt = wkv[:, :, MLA_NOPE:].reshape(MLA_KV_RANK, MLA_HEADS * MLA_V).T.astype(BF16)
    return {
        "w_in": w_in_p, "q_norm": q_norm.reshape(1, -1), "wqt": w_q_b.T.astype(BF16),
        "kv_norm": kv_norm.reshape(1, -1), "wk": wk, "wvt": wvt, "w_o": w_o.astype(BF16),
    }


def _swa_weights(w_qkv, b_qkv, w_o):
    nq = SWA_HEADS * SWA_HEAD_DIM
    nkv = SWA_KV_HEADS * SWA_HEAD_DIM

    def pad_heads(t):
        lead = t.shape[:-1]
        t = t.reshape(lead + (SWA_KV_HEADS, SWA_HEAD_DIM))
        t = jnp.concatenate([t, jnp.zeros_like(t)], axis=-1)
        return t.reshape(lead + (SWA_KV_HEADS * LANES,))

    def relayout(t):
        return jnp.concatenate([t[..., :nq], pad_heads(t[..., nq:nq + nkv]), t[..., nq + nkv:]], axis=-1)

    return {
        "w_qkv_t": relayout(w_qkv).T.astype(BF16),
        "b_qkv_t": relayout(b_qkv).reshape(-1, 1),
        "w_o": w_o.astype(BF16),
    }


def kernel(x, c, positions, ada_w, ada_b, ln_mix_g, ln_mix_b, ln_ffn_g, ln_ffn_b, ffn_w_gate, ffn_w_up, ffn_w_down, mla_w_in, mla_q_norm, mla_w_q_b, mla_kv_norm, mla_w_kv_b, mla_w_o, swa_w_qkv, swa_b_qkv, swa_sinks, swa_w_o, swa_b_o):
    b, s, d = x.shape
    assert (b, s, d) == (1, SEQ, D_MODEL)
    xs = x.reshape(s, d)
    pos = positions.reshape(s)
    mod = _modulation(c, ada_w, ada_b)
    mla_tabs = _rope_tables(pos, MLA_ROPE)
    swa_tabs = _rope_tables(pos, SWA_ROT)
    vec = lambda t: t.reshape(1, d)
    wg, wu, wd = ffn_w_gate.astype(BF16), ffn_w_up.astype(BF16), ffn_w_down.astype(BF16)
    for i in range(DEPTH):
        sh_m, sc_m, g_m, sh_f, sc_f, g_f = (mod[i, k] for k in range(6))
        j = i // 2
        mix = None
        if i % 2 == 0:
            w = _mla_weights(mla_w_in[j], mla_q_norm[j], mla_w_q_b[j], mla_kv_norm[j], mla_w_kv_b[j], mla_w_o[j])
            qt, k, vt = _mla_proj(xs, sc_m, sh_m, w, mla_tabs)
            o = _mla_attn(qt, k, vt)
            mix = (o, w["w_o"], g_m, vec(ln_mix_g[i]), vec(ln_mix_b[i]))
        else:
            w = _swa_weights(swa_w_qkv[j], swa_b_qkv[j], swa_w_o[j])
            qt, k, vt = _swa_proj(xs, sc_m, sh_m, w, swa_tabs)
            xs = _swa_attn(swa_sinks[j], qt, k, vt, xs, g_m, w["w_o"], vec(swa_b_o[j]),
                           vec(ln_mix_g[i]), vec(ln_mix_b[i]))
        xs = _ffn(xs, i, sc_f, sh_f, g_f, wg, wu, wd, vec(ln_ffn_g[i]), vec(ln_ffn_b[i]), mix=mix)
    return xs.reshape(b, s, d)
```

```python
import functools

import jax
import jax.numpy as jnp
from jax import lax
from jax.experimental import pallas as pl
from jax.experimental.pallas import tpu as pltpu

D_MODEL = 1024
SEQ = 16384
DEPTH = 4
ROPE_THETA = 500000.0
LN_EPS = 1e-5
RMS_EPS = 1e-6

MLA_HEADS = 8
MLA_NOPE = 128
MLA_ROPE = 64
MLA_V = 128
MLA_Q_RANK = 384
MLA_KV_RANK = 256
MLA_QK_PAD = 256
MLA_V_AUG = MLA_V + 16

SWA_HEADS = 16
SWA_KV_HEADS = 4
SWA_HEAD_DIM = 64
SWA_WINDOW = 128
SWA_ROT = SWA_HEAD_DIM // 4

D_FF = ((8 * D_MODEL + 3 * 256 - 1) // (3 * 256)) * 256
DEEPNORM_ALPHA = (2 * DEPTH) ** 0.25

LANES = 128
VMEM_LIMIT = 56 * 1024 * 1024

ROW_BLOCK = 512
MLA_T = 512
MLA_UNROLL = 16
MLA_SBUFS = 4
MLA_ATTN_VMEM = 58 * 1024 * 1024
SWA_TQ = 512
SWA_SBUFS = 4
LOG2E = 1.4426950408889634
MLA_Q_SCALE = (MLA_NOPE + MLA_ROPE) ** -0.5 * LOG2E
SWA_Q_SCALE = SWA_HEAD_DIM ** -0.5 * LOG2E

BF16 = jnp.bfloat16
F32 = jnp.float32


def _dot(a, b):
    return jnp.dot(a, b, preferred_element_type=F32)


def _dot_nt(a, b):
    return lax.dot_general(a, b, (((1,), (1,)), ((), ())), preferred_element_type=F32)


def _layer_norm(z, g, b):
    mu = jnp.mean(z, axis=-1, keepdims=True)
    zc = z - mu
    var = jnp.mean(zc * zc, axis=-1, keepdims=True)
    return zc * lax.rsqrt(var + LN_EPS) * g + b


def _rms_norm(z, g):
    return z * lax.rsqrt(jnp.mean(z * z, axis=-1, keepdims=True) + RMS_EPS) * g


def _const_spec(shape):
    nd = len(shape)
    return pl.BlockSpec(shape, lambda *_: (0,) * nd, pipeline_mode=pl.Buffered(1))


def _params(n_grid):
    return pltpu.CompilerParams(
        dimension_semantics=("arbitrary",) * n_grid, vmem_limit_bytes=VMEM_LIMIT)


def _mod_kernel(c_ref, w_ref, b_ref, o_ref):
    c = c_ref[...]
    cond = (c * jax.nn.sigmoid(c)).astype(BF16)
    cond8 = jnp.broadcast_to(cond, (8, D_MODEL))
    y = _dot(cond8, w_ref[0].astype(BF16))
    o_ref[0] = y[0:1] + b_ref[0]


def _modulation(c, ada_w, ada_b):
    d = D_MODEL
    out = pl.pallas_call(
        _mod_kernel,
        out_shape=jax.ShapeDtypeStruct((DEPTH, 1, 6 * d), F32),
        grid=(DEPTH, 6),
        in_specs=[
            pl.BlockSpec((1, d), lambda i, k: (0, 0)),
            pl.BlockSpec((1, d, d), lambda i, k: (i, 0, k)),
            pl.BlockSpec((1, 1, d), lambda i, k: (i, 0, k)),
        ],
        out_specs=pl.BlockSpec((1, 1, d), lambda i, k: (i, 0, k)),
        compiler_params=_params(2),
        name="adaln_mod",
    )(c, ada_w, ada_b.reshape(DEPTH, 1, 6 * d))
    return out.reshape(DEPTH, 6, 1, d)


def _mla_proj_kernel(x_ref, sc_ref, sh_ref, w_in_ref, qn_ref, wqt_ref, kvn_ref, wk_ref, wvt_ref,
                     ckt_ref, skt_ref, qt_ref, k_ref, vt_ref):
    tm = x_ref.shape[0]
    h = (x_ref[...] * (1.0 + sc_ref[...]) + sh_ref[...]).astype(BF16)
    lat = _dot(h, w_in_ref[...])
    q_lat = lat[:, :MLA_Q_RANK]
    kv_lat = lat[:, MLA_Q_RANK:MLA_Q_RANK + MLA_KV_RANK]
    kr = lat[:, MLA_Q_RANK + MLA_KV_RANK:]

    qn = _rms_norm(q_lat, qn_ref[...]).astype(BF16)
    qt = _dot_nt(wqt_ref[...], qn) * MLA_Q_SCALE
    ck = ckt_ref[...]
    sk = skt_ref[...]
    hw = MLA_NOPE + MLA_ROPE
    half = MLA_ROPE // 2
    for hd in range(MLA_HEADS):
        r0 = hd * hw
        x1 = qt[r0 + MLA_NOPE:r0 + MLA_NOPE + half]
        x2 = qt[r0 + MLA_NOPE + half:r0 + hw]
        qt_ref[hd, 0:MLA_NOPE, :] = qt[r0:r0 + MLA_NOPE].astype(BF16)
        qt_ref[hd, MLA_NOPE:MLA_NOPE + half, :] = (x1 * ck - x2 * sk).astype(BF16)
        qt_ref[hd, MLA_NOPE + half:hw, :] = (x2 * ck + x1 * sk).astype(BF16)
        qt_ref[hd, hw:, :] = jnp.zeros((MLA_QK_PAD - hw, tm), BF16)

    kvn = _rms_norm(kv_lat, kvn_ref[...]).astype(BF16)
    kn = _dot(kvn, wk_ref[...])
    krt = kr.T
    x1 = krt[0:half]
    x2 = krt[half:MLA_ROPE]
    k_rope = jnp.concatenate(
        [x1 * ck - x2 * sk, x2 * ck + x1 * sk, krt[MLA_ROPE:]], axis=0).T.astype(BF16)
    for hd in range(MLA_HEADS):
        k_ref[:, hd * MLA_QK_PAD:hd * MLA_QK_PAD + MLA_NOPE] = kn[:, hd * MLA_NOPE:(hd + 1) * MLA_NOPE].astype(BF16)
        k_ref[:, hd * MLA_QK_PAD + MLA_NOPE:(hd + 1) * MLA_QK_PAD] = k_rope

    vt = _dot_nt(wvt_ref[...], kvn)
    row = lax.broadcasted_iota(jnp.int32, (MLA_V_AUG - MLA_V, tm), 0)
    ones_row = jnp.where(row == 0, 1.0, 0.0).astype(BF16)
    for hd in range(MLA_HEADS):
        vt_ref[hd, 0:MLA_V, :] = vt[hd * MLA_V:(hd + 1) * MLA_V].astype(BF16)
        vt_ref[hd, MLA_V:, :] = ones_row


def _mla_proj(x, sc, sh, w, tabs):
    s, d = x.shape
    tm = ROW_BLOCK
    hk = MLA_HEADS * MLA_QK_PAD
    row = lambda i: (i, 0)
    col = lambda i: (0, i)
    return pl.pallas_call(
        _mla_proj_kernel,
        out_shape=(
            jax.ShapeDtypeStruct((MLA_HEADS, MLA_QK_PAD, s), BF16),
            jax.ShapeDtypeStruct((s, hk), BF16),
            jax.ShapeDtypeStruct((MLA_HEADS, MLA_V_AUG, s), BF16),
        ),
        grid=(s // tm,),
        in_specs=[
            pl.BlockSpec((tm, d), row),
            _const_spec((1, d)), _const_spec((1, d)),
            _const_spec(w["w_in"].shape), _const_spec(w["q_norm"].shape), _const_spec(w["wqt"].shape),
            _const_spec(w["kv_norm"].shape), _const_spec(w["wk"].shape), _const_spec(w["wvt"].shape),
            pl.BlockSpec((MLA_ROPE // 2, tm), col), pl.BlockSpec((MLA_ROPE // 2, tm), col),
        ],
        out_specs=(
            pl.BlockSpec((MLA_HEADS, MLA_QK_PAD, tm), lambda i: (0, 0, i)),
            pl.BlockSpec((tm, hk), row),
            pl.BlockSpec((MLA_HEADS, MLA_V_AUG, tm), lambda i: (0, 0, i)),
        ),
        compiler_params=_params(1),
        name="mla_proj",
    )(x, sc, sh, w["w_in"], w["q_norm"], w["wqt"], w["kv_norm"], w["wk"], w["wvt"],
      tabs[0], tabs[1])


def _mla_attn_kernel(qt_ref, k_ref, vt_ref, o_ref, *scratch):
    t = MLA_T
    nq = qt_ref.shape[2] // t
    n_full = nq * (nq - 1) // 2
    assert n_full % MLA_UNROLL == 0 and MLA_UNROLL % MLA_SBUFS == 0 and nq % MLA_SBUFS == 0
    bufs = scratch[:MLA_SBUFS]
    mbufs = scratch[MLA_SBUFS:2 * MLA_SBUFS]
    m_ref, acc_ref = scratch[2 * MLA_SBUFS:]
    m_ref[...] = jnp.full(m_ref.shape, -jnp.inf, F32)
    acc_ref[...] = jnp.zeros(acc_ref.shape, F32)

    def scores(slot, qi, j, diag):
        q0 = pl.multiple_of(qi * t, t)
        k0 = pl.multiple_of(j * t, t)
        st = _dot(k_ref[pl.ds(k0, t), :], qt_ref[0, :, pl.ds(q0, t)])
        if diag:
            key = lax.broadcasted_iota(jnp.int32, (t, t), 0)
            qry = lax.broadcasted_iota(jnp.int32, (t, t), 1)
            st = jnp.where(key <= qry, st, -jnp.inf)
        bufs[slot][...] = st
        mbufs[slot][...] = jnp.max(st.reshape(t // 8, 8, t), axis=0)

    def update(slot, qi, j, diag):
        k0 = pl.multiple_of(j * t, t)
        st = bufs[slot][...]
        m_prev = m_ref[qi]
        m_new = jnp.maximum(m_prev, jnp.max(mbufs[slot][...], axis=0, keepdims=True))
        alpha = jnp.exp2(m_prev - m_new)
        p = jnp.exp2((st - m_new).astype(BF16))
        acc_new = alpha * acc_ref[qi] + _dot(vt_ref[0, :, pl.ds(k0, t)], p)
        if diag:
            q0 = pl.multiple_of(qi * t, t)
            o_ref[pl.ds(q0, t), :] = (acc_new[0:MLA_V] / acc_new[MLA_V:MLA_V + 1]).T.astype(BF16)
        else:
            m_ref[qi] = m_new
            acc_ref[qi] = acc_new

    scores(0, 1, 0, False)

    def full_body(_, carry):
        qi, j = carry
        for u in range(MLA_UNROLL):
            last = j + 1 == qi
            qi_n = jnp.where(last, qi + 1, qi)
            j_n = jnp.where(last, 0, j + 1)
            scores((u + 1) % MLA_SBUFS, jnp.minimum(qi_n, nq - 1), j_n, False)
            update(u % MLA_SBUFS, qi, j, False)
            qi, j = qi_n, j_n
        return qi, j

    lax.fori_loop(0, n_full // MLA_UNROLL, full_body, (jnp.int32(1), jnp.int32(0)))

    scores(0, 0, 0, True)

    def diag_body(i, carry):
        for u in range(MLA_SBUFS):
            qi = MLA_SBUFS * i + u
            nxt = jnp.minimum(qi + 1, nq - 1)
            scores((u + 1) % MLA_SBUFS, nxt, nxt, True)
            update(u % MLA_SBUFS, qi, qi, True)
        return carry

    lax.fori_loop(0, nq // MLA_SBUFS, diag_body, 0)


def _mla_attn(qt, k, vt):
    s = k.shape[0]
    t = MLA_T
    return pl.pallas_call(
        _mla_attn_kernel,
        out_shape=jax.ShapeDtypeStruct((s, MLA_HEADS * MLA_V), BF16),
        grid=(MLA_HEADS,),
        in_specs=[
            pl.BlockSpec((1, MLA_QK_PAD, s), lambda h: (h, 0, 0)),
            pl.BlockSpec((s, MLA_QK_PAD), lambda h: (0, h), pipeline_mode=pl.Buffered(1)),
            pl.BlockSpec((1, MLA_V_AUG, s), lambda h: (h, 0, 0)),
        ],
        out_specs=pl.BlockSpec((s, MLA_V), lambda h: (0, h)),
        scratch_shapes=[
            *[pltpu.VMEM((t, t), F32)] * MLA_SBUFS,
            *[pltpu.VMEM((8, t), F32)] * MLA_SBUFS,
            pltpu.VMEM((s // t, 1, t), F32), pltpu.VMEM((s // t, MLA_V_AUG, t), F32)],
        compiler_params=pltpu.CompilerParams(
            dimension_semantics=("arbitrary",), vmem_limit_bytes=MLA_ATTN_VMEM),
        name="mla_attn",
    )(qt, k, vt)


def _ffn_kernel(*refs, with_mix):
    if with_mix:
        o_ref, wo_ref, gm_ref, lnmg_ref, lnmb_ref = refs[:5]
        refs = refs[5:]
    x_ref, sc_ref, sh_ref, g_ref, wg_ref, wu_ref, wd_ref, lng_ref, lnb_ref, y_ref = refs
    x = x_ref[...]
    if with_mix:
        z = DEEPNORM_ALPHA * x + gm_ref[...] * _dot(o_ref[...], wo_ref[...])
        x = _layer_norm(z, lnmg_ref[...], lnmb_ref[...])
    h = (x * (1.0 + sc_ref[...]) + sh_ref[...]).astype(BF16)
    gate = _dot(h, wg_ref[0])
    up = _dot(h, wu_ref[0])
    act = (gate * jax.nn.sigmoid(gate) * up).astype(BF16)
    y = _dot(act, wd_ref[0])
    z = DEEPNORM_ALPHA * x + g_ref[...] * y
    y_ref[...] = _layer_norm(z, lng_ref[...], lnb_ref[...])


def _ffn(x, layer, sc, sh, gate, wg, wu, wd, ln_g, ln_b, mix=None):
    s, d = x.shape
    tm = ROW_BLOCK
    row = lambda i: (i, 0)
    vec = _const_spec((1, d))
    stacked = lambda w: pl.BlockSpec((1,) + w.shape[1:], lambda i: (layer, 0, 0), pipeline_mode=pl.Buffered(1))
    mix_specs, mix_args = [], []
    if mix is not None:
        o, w_o = mix[0], mix[1]
        mix_specs = [pl.BlockSpec((tm, o.shape[1]), row), _const_spec(w_o.shape), vec, vec, vec]
        mix_args = list(mix)
    return pl.pallas_call(
        functools.partial(_ffn_kernel, with_mix=mix is not None),
        out_shape=jax.ShapeDtypeStruct((s, d), F32),
        grid=(s // tm,),
        in_specs=mix_specs + [
            pl.BlockSpec((tm, d), row), vec, vec, vec,
            stacked(wg), stacked(wu), stacked(wd), vec, vec,
        ],
        out_specs=pl.BlockSpec((tm, d), row),
        compiler_params=_params(1),
        name="mix_ffn" if mix is not None else "ffn",
    )(*mix_args, x, sc, sh, gate, wg, wu, wd, ln_g, ln_b)


def _rope_rows(xt, c, s):
    half = SWA_ROT // 2
    x1 = xt[0:half]
    x2 = xt[half:SWA_ROT]
    return jnp.concatenate([x1 * c - x2 * s, x2 * c + x1 * s, xt[SWA_ROT:]], axis=0)


def _swa_proj_kernel(x_ref, sc_ref, sh_ref, wt_ref, b_ref, ct_ref, st_ref, qt_ref, k_ref, vt_ref):
    h = (x_ref[...] * (1.0 + sc_ref[...]) + sh_ref[...]).astype(BF16)
    qkvt = _dot_nt(wt_ref[...], h) + b_ref[...]
    c = ct_ref[...]
    s = st_ref[...]
    hd = SWA_HEAD_DIM
    nq = SWA_HEADS * hd
    nk = SWA_KV_HEADS * LANES
    for i in range(SWA_HEADS):
        qt_ref[i * hd:(i + 1) * hd, :] = (_rope_rows(qkvt[i * hd:(i + 1) * hd], c, s) * SWA_Q_SCALE).astype(BF16)
    for g in range(SWA_KV_HEADS):
        r0 = nq + g * LANES
        kt = jnp.concatenate([_rope_rows(qkvt[r0:r0 + hd], c, s), qkvt[r0 + hd:r0 + LANES]], axis=0)
        k_ref[:, g * LANES:(g + 1) * LANES] = kt.T.astype(BF16)
    vt_ref[...] = qkvt[nq + nk:].astype(BF16)


def _swa_proj(x, sc, sh, w, tabs):
    s, d = x.shape
    tm = ROW_BLOCK
    nq = SWA_HEADS * SWA_HEAD_DIM
    nk = SWA_KV_HEADS * LANES
    nv = SWA_KV_HEADS * SWA_HEAD_DIM
    row = lambda i: (i, 0)
    col = lambda i: (0, i)
    return pl.pallas_call(
        _swa_proj_kernel,
        out_shape=(
            jax.ShapeDtypeStruct((nq, s), BF16),
            jax.ShapeDtypeStruct((s, nk), BF16),
            jax.ShapeDtypeStruct((nv, s), BF16),
        ),
        grid=(s // tm,),
        in_specs=[
            pl.BlockSpec((tm, d), row),
            _const_spec((1, d)), _const_spec((1, d)),
            _const_spec(w["w_qkv_t"].shape), _const_spec(w["b_qkv_t"].shape),
            pl.BlockSpec((SWA_ROT // 2, tm), col), pl.BlockSpec((SWA_ROT // 2, tm), col),
        ],
        out_specs=(pl.BlockSpec((nq, tm), col), pl.BlockSpec((tm, nk), row), pl.BlockSpec((nv, tm), col)),
        compiler_params=_params(1),
        name="swa_proj",
    )(x, sc, sh, w["w_qkv_t"], w["b_qkv_t"], tabs[0], tabs[1])


def _swa_attn_kernel(sinks_ref, qt_ref, kp_ref, kc_ref, vtp_ref, vtc_ref, x_ref, g_ref, w_ref, b_ref,
                     lng_ref, lnb_ref, y_ref, kwin_ref, vtwin_ref, ot_ref, *s_refs):
    i = pl.program_id(0)
    w = SWA_WINDOW
    tq = SWA_TQ
    hd = SWA_HEAD_DIM
    grp = SWA_HEADS // SWA_KV_HEADS
    kwin_ref[0:w, :] = kp_ref[...]
    kwin_ref[w:, :] = kc_ref[...]
    vtwin_ref[:, 0:w] = vtp_ref[...]
    vtwin_ref[:, w:] = vtc_ref[...]

    key = lax.broadcasted_iota(jnp.int32, (2 * w, w), 0)
    qry = lax.broadcasted_iota(jnp.int32, (2 * w, w), 1)
    band = (key > qry) & (key <= qry + w)

    def scores(b, g):
        r0 = b * w
        k_g = kwin_ref[r0:r0 + 2 * w, g * LANES:g * LANES + hd]
        q_g = jnp.concatenate(
            [qt_ref[h * hd:(h + 1) * hd, r0:r0 + w] for h in range(g * grp, (g + 1) * grp)], axis=1)
        return _dot(k_g, q_g)

    tasks = [(b, g) for b in range(tq // w) for g in range(SWA_KV_HEADS)]
    s_refs[0][...] = scores(*tasks[0])
    for n, (b, g) in enumerate(tasks):
        if n + 1 < len(tasks):
            s_refs[(n + 1) % len(s_refs)][...] = scores(*tasks[n + 1])
        st = s_refs[n % len(s_refs)][...]
        r0 = b * w
        heads = range(g * grp, (g + 1) * grp)
        if g == 0:
            first_key = jnp.where(i * (tq // w) + b > 0, 0, w)
            neg = jnp.where(band & (key >= first_key), 0.0, -jnp.inf)
            neg = jnp.concatenate([neg] * grp, axis=1)
        st = st + neg
        sink = jnp.concatenate(
            [jnp.full((1, w), sinks_ref[h] * LOG2E, F32) for h in heads], axis=1)
        m = jnp.maximum(jnp.max(st, axis=0, keepdims=True), sink)
        p = jnp.exp2(st - m)
        denom = jnp.sum(p, axis=0, keepdims=True) + jnp.exp2(sink - m)
        vt_g = vtwin_ref[g * hd:(g + 1) * hd, r0:r0 + 2 * w]
        ot = _dot(vt_g, p.astype(BF16)) / denom
        for c, h in enumerate(heads):
            ot_ref[h * hd:(h + 1) * hd, r0:r0 + w] = ot[:, c * w:(c + 1) * w]

    o = ot_ref[...].T.astype(BF16)
    y = _dot(o, w_ref[...]) + b_ref[...]
    z = DEEPNORM_ALPHA * x_ref[...] + g_ref[...] * y
    y_ref[...] = _layer_norm(z, lng_ref[...], lnb_ref[...])


def _swa_attn(sinks, qt, k, vt, x, gate, w_o, b_o, ln_g, ln_b):
    s, d = x.shape
    tq = SWA_TQ
    w = SWA_WINDOW
    nq = qt.shape[0]
    nk = k.shape[1]
    nv = vt.shape[0]
    r = tq // w
    row = lambda i: (i, 0)
    col = lambda i: (0, i)
    prev_row = lambda i: (jnp.maximum(i * r - 1, 0), 0)
    prev_col = lambda i: (0, jnp.maximum(i * r - 1, 0))
    return pl.pallas_call(
        _swa_attn_kernel,
        out_shape=jax.ShapeDtypeStruct((s, d), F32),
        grid=(s // tq,),
        in_specs=[
            pl.BlockSpec(memory_space=pltpu.SMEM),
            pl.BlockSpec((nq, tq), col),
            pl.BlockSpec((w, nk), prev_row), pl.BlockSpec((tq, nk), row),
            pl.BlockSpec((nv, w), prev_col), pl.BlockSpec((nv, tq), col),
            pl.BlockSpec((tq, d), row),
            _const_spec((1, d)), _const_spec(w_o.shape), _const_spec((1, d)),
            _const_spec((1, d)), _const_spec((1, d)),
        ],
        out_specs=pl.BlockSpec((tq, d), row),
        scratch_shapes=[
            pltpu.VMEM((tq + w, nk), BF16), pltpu.VMEM((nv, tq + w), BF16),
            pltpu.VMEM((nq, tq), F32),
            *[pltpu.VMEM((2 * w, w * SWA_HEADS // SWA_KV_HEADS), F32)] * SWA_SBUFS],
        compiler_params=_params(1),
        name="swa_attn",
    )(sinks, qt, k, k, vt, vt, x, gate, w_o, b_o, ln_g, ln_b)


def _rope_inv(rot_dim):
    return ROPE_THETA ** (-jnp.arange(0, rot_dim, 2, dtype=F32) / rot_dim)


def _rope_tables(positions, rot_dim):
    ang = _rope_inv(rot_dim)[:, None] * positions.astype(F32)[None, :]
    return jnp.cos(ang), jnp.sin(ang)


def _mla_weights(w_in, q_norm, w_q_b, kv_norm, w_kv_b, w_o):
    d = D_MODEL
    w_in_p = jnp.concatenate([w_in, jnp.zeros((d, 64), F32)], axis=1).astype(BF16)
    wkv = w_kv_b.reshape(MLA_KV_RANK, MLA_HEADS, MLA_NOPE + MLA_V)
    wk = wkv[:, :, :MLA_NOPE].reshape(MLA_KV_RANK, MLA_HEADS * MLA_NOPE).astype(BF16)
    wvt = wkv[:, :, MLA_NOPE:].reshape(MLA_KV_RANK, MLA_HEADS * MLA_V).T.astype(BF16)
    return {
        "w_in": w_in_p, "q_norm": q_norm.reshape(1, -1), "wqt": w_q_b.T.astype(BF16),
        "kv_norm": kv_norm.reshape(1, -1), "wk": wk, "wvt": wvt, "w_o": w_o.astype(BF16),
    }


def _swa_weights(w_qkv, b_qkv, w_o):
    nq = SWA_HEADS * SWA_HEAD_DIM
    nkv = SWA_KV_HEADS * SWA_HEAD_DIM

    def pad_heads(t):
        lead = t.shape[:-1]
        t = t.reshape(lead + (SWA_KV_HEADS, SWA_HEAD_DIM))
        t = jnp.concatenate([t, jnp.zeros_like(t)], axis=-1)
        return t.reshape(lead + (SWA_KV_HEADS * LANES,))

    def relayout(t):
        return jnp.concatenate([t[..., :nq], pad_heads(t[..., nq:nq + nkv]), t[..., nq + nkv:]], axis=-1)

    return {
        "w_qkv_t": relayout(w_qkv).T.astype(BF16),
        "b_qkv_t": relayout(b_qkv).reshape(-1, 1),
        "w_o": w_o.astype(BF16),
    }


def kernel(x, c, positions, ada_w, ada_b, ln_mix_g, ln_mix_b, ln_ffn_g, ln_ffn_b, ffn_w_gate, ffn_w_up, ffn_w_down, mla_w_in, mla_q_norm, mla_w_q_b, mla_kv_norm, mla_w_kv_b, mla_w_o, swa_w_qkv, swa_b_qkv, swa_sinks, swa_w_o, swa_b_o):
    b, s, d = x.shape
    assert (b, s, d) == (1, SEQ, D_MODEL)
    xs = x.reshape(s, d)
    pos = positions.reshape(s)
    mod = _modulation(c, ada_w, ada_b)
    mla_tabs = _rope_tables(pos, MLA_ROPE)
    swa_tabs = _rope_tables(pos, SWA_ROT)
    vec = lambda t: t.reshape(1, d)
    wg, wu, wd = ffn_w_gate.astype(BF16), ffn_w_up.astype(BF16), ffn_w_down.astype(BF16)
    for i in range(DEPTH):
        sh_m, sc_m, g_m, sh_f, sc_f, g_f = (mod[i, k] for k in range(6))
        j = i // 2
        mix = None
        if i % 2 == 0:
            w = _mla_weights(mla_w_in[j], mla_q_norm[j], mla_w_q_b[j], mla_kv_norm[j], mla_w_kv_b[j], mla_w_o[j])
            qt, k, vt = _mla_proj(xs, sc_m, sh_m, w, mla_tabs)
            o = _mla_attn(qt, k, vt)
            mix = (o, w["w_o"], g_m, vec(ln_mix_g[i]), vec(ln_mix_b[i]))
        else:
            w = _swa_weights(swa_w_qkv[j], swa_b_qkv[j], swa_w_o[j])
            qt, k, vt = _swa_proj(xs, sc_m, sh_m, w, swa_tabs)
            xs = _swa_attn(swa_sinks[j], qt, k, vt, xs, g_m, w["w_o"], vec(swa_b_o[j]),
                           vec(ln_mix_g[i]), vec(ln_mix_b[i]))
        xs = _ffn(xs, i, sc_f, sh_f, g_f, wg, wu, wd, vec(ln_ffn_g[i]), vec(ln_ffn_b[i]), mix=mix)
    return xs.reshape(b, s, d)
```

```python
import functools

import jax
import jax.numpy as jnp
from jax import lax
from jax.experimental import pallas as pl
from jax.experimental.pallas import tpu as pltpu

D_MODEL = 1024
SEQ = 16384
DEPTH = 4
ROPE_THETA = 500000.0
LN_EPS = 1e-5
RMS_EPS = 1e-6

MLA_HEADS = 8
MLA_NOPE = 128
MLA_ROPE = 64
MLA_V = 128
MLA_Q_RANK = 384
MLA_KV_RANK = 256
MLA_QK_PAD = 256
MLA_V_AUG = MLA_V + 16

SWA_HEADS = 16
SWA_KV_HEADS = 4
SWA_HEAD_DIM = 64
SWA_WINDOW = 128
SWA_ROT = SWA_HEAD_DIM // 4

D_FF = ((8 * D_MODEL + 3 * 256 - 1) // (3 * 256)) * 256
DEEPNORM_ALPHA = (2 * DEPTH) ** 0.25

LANES = 128
VMEM_LIMIT = 56 * 1024 * 1024

ROW_BLOCK = 512
MLA_T = 512
MLA_UNROLL = 16
MLA_SBUFS = 4
MLA_DIAG_UNROLL = 8
MLA_ATTN_VMEM = 58 * 1024 * 1024
SWA_TQ = 512
SWA_SBUFS = 4
LOG2E = 1.4426950408889634
MLA_Q_SCALE = (MLA_NOPE + MLA_ROPE) ** -0.5 * LOG2E
SWA_Q_SCALE = SWA_HEAD_DIM ** -0.5 * LOG2E

BF16 = jnp.bfloat16
F32 = jnp.float32


def _dot(a, b):
    return jnp.dot(a, b, preferred_element_type=F32)


def _dot_nt(a, b):
    return lax.dot_general(a, b, (((1,), (1,)), ((), ())), preferred_element_type=F32)


def _layer_norm(z, g, b):
    mu = jnp.mean(z, axis=-1, keepdims=True)
    zc = z - mu
    var = jnp.mean(zc * zc, axis=-1, keepdims=True)
    return zc * lax.rsqrt(var + LN_EPS) * g + b


def _rms_norm(z, g):
    return z * lax.rsqrt(jnp.mean(z * z, axis=-1, keepdims=True) + RMS_EPS) * g


def _const_spec(shape):
    nd = len(shape)
    return pl.BlockSpec(shape, lambda *_: (0,) * nd, pipeline_mode=pl.Buffered(1))


def _params(n_grid):
    return pltpu.CompilerParams(
        dimension_semantics=("arbitrary",) * n_grid, vmem_limit_bytes=VMEM_LIMIT)


def _mod_kernel(c_ref, w_ref, b_ref, o_ref):
    c = c_ref[...]
    cond = (c * jax.nn.sigmoid(c)).astype(BF16)
    cond8 = jnp.broadcast_to(cond, (8, D_MODEL))
    y = _dot(cond8, w_ref[0].astype(BF16))
    o_ref[0] = y[0:1] + b_ref[0]


def _modulation(c, ada_w, ada_b):
    d = D_MODEL
    out = pl.pallas_call(
        _mod_kernel,
        out_shape=jax.ShapeDtypeStruct((DEPTH, 1, 6 * d), F32),
        grid=(DEPTH, 6),
        in_specs=[
            pl.BlockSpec((1, d), lambda i, k: (0, 0)),
            pl.BlockSpec((1, d, d), lambda i, k: (i, 0, k)),
            pl.BlockSpec((1, 1, d), lambda i, k: (i, 0, k)),
        ],
        out_specs=pl.BlockSpec((1, 1, d), lambda i, k: (i, 0, k)),
        compiler_params=_params(2),
        name="adaln_mod",
    )(c, ada_w, ada_b.reshape(DEPTH, 1, 6 * d))
    return out.reshape(DEPTH, 6, 1, d)


def _mla_proj_kernel(x_ref, sc_ref, sh_ref, w_in_ref, qn_ref, wqt_ref, kvn_ref, wk_ref, wvt_ref,
                     ckt_ref, skt_ref, qt_ref, k_ref, vt_ref):
    tm = x_ref.shape[0]
    h = (x_ref[...] * (1.0 + sc_ref[...]) + sh_ref[...]).astype(BF16)
    lat = _dot(h, w_in_ref[...])
    q_lat = lat[:, :MLA_Q_RANK]
    kv_lat = lat[:, MLA_Q_RANK:MLA_Q_RANK + MLA_KV_RANK]
    kr = lat[:, MLA_Q_RANK + MLA_KV_RANK:]

    qn = _rms_norm(q_lat, qn_ref[...]).astype(BF16)
    qt = _dot_nt(wqt_ref[...], qn) * MLA_Q_SCALE
    ck = ckt_ref[...]
    sk = skt_ref[...]
    hw = MLA_NOPE + MLA_ROPE
    half = MLA_ROPE // 2
    for hd in range(MLA_HEADS):
        r0 = hd * hw
        x1 = qt[r0 + MLA_NOPE:r0 + MLA_NOPE + half]
        x2 = qt[r0 + MLA_NOPE + half:r0 + hw]
        qt_ref[hd, 0:MLA_NOPE, :] = qt[r0:r0 + MLA_NOPE].astype(BF16)
        qt_ref[hd, MLA_NOPE:MLA_NOPE + half, :] = (x1 * ck - x2 * sk).astype(BF16)
        qt_ref[hd, MLA_NOPE + half:hw, :] = (x2 * ck + x1 * sk).astype(BF16)
        qt_ref[hd, hw:, :] = jnp.zeros((MLA_QK_PAD - hw, tm), BF16)

    kvn = _rms_norm(kv_lat, kvn_ref[...]).astype(BF16)
    kn = _dot(kvn, wk_ref[...])
    krt = kr.T
    x1 = krt[0:half]
    x2 = krt[half:MLA_ROPE]
    k_rope = jnp.concatenate(
        [x1 * ck - x2 * sk, x2 * ck + x1 * sk, krt[MLA_ROPE:]], axis=0).T.astype(BF16)
    for hd in range(MLA_HEADS):
        k_ref[:, hd * MLA_QK_PAD:hd * MLA_QK_PAD + MLA_NOPE] = kn[:, hd * MLA_NOPE:(hd + 1) * MLA_NOPE].astype(BF16)
        k_ref[:, hd * MLA_QK_PAD + MLA_NOPE:(hd + 1) * MLA_QK_PAD] = k_rope

    vt = _dot_nt(wvt_ref[...], kvn)
    row = lax.broadcasted_iota(jnp.int32, (MLA_V_AUG - MLA_V, tm), 0)
    ones_row = jnp.where(row == 0, 1.0, 0.0).astype(BF16)
    for hd in range(MLA_HEADS):
        vt_ref[hd, 0:MLA_V, :] = vt[hd * MLA_V:(hd + 1) * MLA_V].astype(BF16)
        vt_ref[hd, MLA_V:, :] = ones_row


def _mla_proj(x, sc, sh, w, tabs):
    s, d = x.shape
    tm = ROW_BLOCK
    hk = MLA_HEADS * MLA_QK_PAD
    row = lambda i: (i, 0)
    col = lambda i: (0, i)
    return pl.pallas_call(
        _mla_proj_kernel,
        out_shape=(
            jax.ShapeDtypeStruct((MLA_HEADS, MLA_QK_PAD, s), BF16),
            jax.ShapeDtypeStruct((s, hk), BF16),
            jax.ShapeDtypeStruct((MLA_HEADS, MLA_V_AUG, s), BF16),
        ),
        grid=(s // tm,),
        in_specs=[
            pl.BlockSpec((tm, d), row),
            _const_spec((1, d)), _const_spec((1, d)),
            _const_spec(w["w_in"].shape), _const_spec(w["q_norm"].shape), _const_spec(w["wqt"].shape),
            _const_spec(w["kv_norm"].shape), _const_spec(w["wk"].shape), _const_spec(w["wvt"].shape),
            pl.BlockSpec((MLA_ROPE // 2, tm), col), pl.BlockSpec((MLA_ROPE // 2, tm), col),
        ],
        out_specs=(
            pl.BlockSpec((MLA_HEADS, MLA_QK_PAD, tm), lambda i: (0, 0, i)),
            pl.BlockSpec((tm, hk), row),
            pl.BlockSpec((MLA_HEADS, MLA_V_AUG, tm), lambda i: (0, 0, i)),
        ),
        compiler_params=_params(1),
        name="mla_proj",
    )(x, sc, sh, w["w_in"], w["q_norm"], w["wqt"], w["kv_norm"], w["wk"], w["wvt"],
      tabs[0], tabs[1])


def _mla_attn_kernel(qt_ref, k_ref, vt_ref, o_ref, *scratch):
    t = MLA_T
    nq = qt_ref.shape[2] // t
    n_full = nq * (nq - 1) // 2
    assert n_full % MLA_UNROLL == 0 and MLA_UNROLL % MLA_SBUFS == 0
    assert nq % MLA_DIAG_UNROLL == 0 and MLA_DIAG_UNROLL % MLA_SBUFS == 0
    bufs = scratch[:MLA_SBUFS]
    mbufs = scratch[MLA_SBUFS:2 * MLA_SBUFS]
    m_ref, acc_ref = scratch[2 * MLA_SBUFS:]
    m_ref[...] = jnp.full(m_ref.shape, -jnp.inf, F32)
    acc_ref[...] = jnp.zeros(acc_ref.shape, F32)

    def scores(slot, qi, j, diag):
        q0 = pl.multiple_of(qi * t, t)
        k0 = pl.multiple_of(j * t, t)
        st = _dot(k_ref[pl.ds(k0, t), :], qt_ref[0, :, pl.ds(q0, t)])
        bufs[slot][...] = st
        if not diag:
            mbufs[slot][...] = jnp.max(st.reshape(t // 8, 8, t), axis=0)

    def update(slot, qi, j, diag):
        k0 = pl.multiple_of(j * t, t)
        st = bufs[slot][...]
        m_prev = m_ref[qi]
        if diag:
            key = lax.broadcasted_iota(jnp.int32, (t, t), 0)
            qry = lax.broadcasted_iota(jnp.int32, (t, t), 1)
            st = jnp.where(key <= qry, st, -jnp.inf)
            m_blk = jnp.max(st, axis=0, keepdims=True)
        else:
            m_blk = jnp.max(mbufs[slot][...], axis=0, keepdims=True)
        m_new = jnp.maximum(m_prev, m_blk)
        alpha = jnp.exp2(m_prev - m_new)
        p = jnp.exp2(st - m_new).astype(BF16)
        acc_new = alpha * acc_ref[qi] + _dot(vt_ref[0, :, pl.ds(k0, t)], p)
        if diag:
            q0 = pl.multiple_of(qi * t, t)
            o_ref[pl.ds(q0, t), :] = (acc_new[0:MLA_V] / acc_new[MLA_V:MLA_V + 1]).T.astype(BF16)
        else:
            m_ref[qi] = m_new
            acc_ref[qi] = acc_new

    scores(0, 1, 0, False)

    def full_body(_, carry):
        qi, j = carry
        for u in range(MLA_UNROLL):
            last = j + 1 == qi
            qi_n = jnp.where(last, qi + 1, qi)
            j_n = jnp.where(last, 0, j + 1)
            scores((u + 1) % MLA_SBUFS, jnp.minimum(qi_n, nq - 1), j_n, False)
            update(u % MLA_SBUFS, qi, j, False)
            qi, j = qi_n, j_n
        return qi, j

    lax.fori_loop(0, n_full // MLA_UNROLL, full_body, (jnp.int32(1), jnp.int32(0)))

    scores(0, 0, 0, True)

    def diag_body(i, carry):
        for u in range(MLA_DIAG_UNROLL):
            qi = MLA_DIAG_UNROLL * i + u
            nxt = jnp.minimum(qi + 1, nq - 1)
            scores((u + 1) % MLA_SBUFS, nxt, nxt, True)
            update(u % MLA_SBUFS, qi, qi, True)
        return carry

    lax.fori_loop(0, nq // MLA_DIAG_UNROLL, diag_body, 0)


def _mla_attn(qt, k, vt):
    s = k.shape[0]
    t = MLA_T
    return pl.pallas_call(
        _mla_attn_kernel,
        out_shape=jax.ShapeDtypeStruct((s, MLA_HEADS * MLA_V), BF16),
        grid=(MLA_HEADS,),
        in_specs=[
            pl.BlockSpec((1, MLA_QK_PAD, s), lambda h: (h, 0, 0)),
            pl.BlockSpec((s, MLA_QK_PAD), lambda h: (0, h), pipeline_mode=pl.Buffered(1)),
            pl.BlockSpec((1, MLA_V_AUG, s), lambda h: (h, 0, 0)),
        ],
        out_specs=pl.BlockSpec((s, MLA_V), lambda h: (0, h)),
        scratch_shapes=[
            *[pltpu.VMEM((t, t), F32)] * MLA_SBUFS,
            *[pltpu.VMEM((8, t), F32)] * MLA_SBUFS,
            pltpu.VMEM((s // t, 1, t), F32), pltpu.VMEM((s // t, MLA_V_AUG, t), F32)],
        compiler_params=pltpu.CompilerParams(
            dimension_semantics=("arbitrary",), vmem_limit_bytes=MLA_ATTN_VMEM),
        name="mla_attn",
    )(qt, k, vt)


def _ffn_kernel(*refs, with_mix):
    if with_mix:
        o_ref, wo_ref, gm_ref, lnmg_ref, lnmb_ref = refs[:5]
        refs = refs[5:]
    x_ref, sc_ref, sh_ref, g_ref, wg_ref, wu_ref, wd_ref, lng_ref, lnb_ref, y_ref = refs
    x = x_ref[...]
    if with_mix:
        z = DEEPNORM_ALPHA * x + gm_ref[...] * _dot(o_ref[...], wo_ref[...])
        x = _layer_norm(z, lnmg_ref[...], lnmb_ref[...])
    h = (x * (1.0 + sc_ref[...]) + sh_ref[...]).astype(BF16)
    gate = _dot(h, wg_ref[0])
    up = _dot(h, wu_ref[0])
    act = (gate * jax.nn.sigmoid(gate) * up).astype(BF16)
    y = _dot(act, wd_ref[0])
    z = DEEPNORM_ALPHA * x + g_ref[...] * y
    y_ref[...] = _layer_norm(z, lng_ref[...], lnb_ref[...])


def _ffn(x, layer, sc, sh, gate, wg, wu, wd, ln_g, ln_b, mix=None):
    s, d = x.shape
    tm = ROW_BLOCK
    row = lambda i: (i, 0)
    vec = _const_spec((1, d))
    stacked = lambda w: pl.BlockSpec((1,) + w.shape[1:], lambda i: (layer, 0, 0), pipeline_mode=pl.Buffered(1))
    mix_specs, mix_args = [], []
    if mix is not None:
        o, w_o = mix[0], mix[1]
        mix_specs = [pl.BlockSpec((tm, o.shape[1]), row), _const_spec(w_o.shape), vec, vec, vec]
        mix_args = list(mix)
    return pl.pallas_call(
        functools.partial(_ffn_kernel, with_mix=mix is not None),
        out_shape=jax.ShapeDtypeStruct((s, d), F32),
        grid=(s // tm,),
        in_specs=mix_specs + [
            pl.BlockSpec((tm, d), row), vec, vec, vec,
            stacked(wg), stacked(wu), stacked(wd), vec, vec,
        ],
        out_specs=pl.BlockSpec((tm, d), row),
        compiler_params=_params(1),
        name="mix_ffn" if mix is not None else "ffn",
    )(*mix_args, x, sc, sh, gate, wg, wu, wd, ln_g, ln_b)


def _rope_rows(xt, c, s):
    half = SWA_ROT // 2
    x1 = xt[0:half]
    x2 = xt[half:SWA_ROT]
    return jnp.concatenate([x1 * c - x2 * s, x2 * c + x1 * s, xt[SWA_ROT:]], axis=0)


def _swa_proj_kernel(x_ref, sc_ref, sh_ref, wt_ref, b_ref, ct_ref, st_ref, qt_ref, k_ref, vt_ref):
    h = (x_ref[...] * (1.0 + sc_ref[...]) + sh_ref[...]).astype(BF16)
    qkvt = _dot_nt(wt_ref[...], h) + b_ref[...]
    c = ct_ref[...]
    s = st_ref[...]
    hd = SWA_HEAD_DIM
    nq = SWA_HEADS * hd
    nk = SWA_KV_HEADS * LANES
    for i in range(SWA_HEADS):
        qt_ref[i * hd:(i + 1) * hd, :] = (_rope_rows(qkvt[i * hd:(i + 1) * hd], c, s) * SWA_Q_SCALE).astype(BF16)
    for g in range(SWA_KV_HEADS):
        r0 = nq + g * LANES
        kt = jnp.concatenate([_rope_rows(qkvt[r0:r0 + hd], c, s), qkvt[r0 + hd:r0 + LANES]], axis=0)
        k_ref[:, g * LANES:(g + 1) * LANES] = kt.T.astype(BF16)
    vt_ref[...] = qkvt[nq + nk:].astype(BF16)


def _swa_proj(x, sc, sh, w, tabs):
    s, d = x.shape
    tm = ROW_BLOCK
    nq = SWA_HEADS * SWA_HEAD_DIM
    nk = SWA_KV_HEADS * LANES
    nv = SWA_KV_HEADS * SWA_HEAD_DIM
    row = lambda i: (i, 0)
    col = lambda i: (0, i)
    return pl.pallas_call(
        _swa_proj_kernel,
        out_shape=(
            jax.ShapeDtypeStruct((nq, s), BF16),
            jax.ShapeDtypeStruct((s, nk), BF16),
            jax.ShapeDtypeStruct((nv, s), BF16),
        ),
        grid=(s // tm,),
        in_specs=[
            pl.BlockSpec((tm, d), row),
            _const_spec((1, d)), _const_spec((1, d)),
            _const_spec(w["w_qkv_t"].shape), _const_spec(w["b_qkv_t"].shape),
            pl.BlockSpec((SWA_ROT // 2, tm), col), pl.BlockSpec((SWA_ROT // 2, tm), col),
        ],
        out_specs=(pl.BlockSpec((nq, tm), col), pl.BlockSpec((tm, nk), row), pl.BlockSpec((nv, tm), col)),
        compiler_params=_params(1),
        name="swa_proj",
    )(x, sc, sh, w["w_qkv_t"], w["b_qkv_t"], tabs[0], tabs[1])


def _swa_attn_kernel(sinks_ref, qt_ref, kp_ref, kc_ref, vtp_ref, vtc_ref, x_ref, g_ref, w_ref, b_ref,
                     lng_ref, lnb_ref, y_ref, kwin_ref, vtwin_ref, ot_ref, *s_refs):
    i = pl.program_id(0)
    w = SWA_WINDOW
    tq = SWA_TQ
    hd = SWA_HEAD_DIM
    grp = SWA_HEADS // SWA_KV_HEADS
    kwin_ref[0:w, :] = kp_ref[...]
    kwin_ref[w:, :] = kc_ref[...]
    vtwin_ref[:, 0:w] = vtp_ref[...]
    vtwin_ref[:, w:] = vtc_ref[...]

    key = lax.broadcasted_iota(jnp.int32, (2 * w, w), 0)
    qry = lax.broadcasted_iota(jnp.int32, (2 * w, w), 1)
    band = (key > qry) & (key <= qry + w)

    def scores(b, g):
        r0 = b * w
        k_g = kwin_ref[r0:r0 + 2 * w, g * LANES:g * LANES + hd]
        q_g = jnp.concatenate(
            [qt_ref[h * hd:(h + 1) * hd, r0:r0 + w] for h in range(g * grp, (g + 1) * grp)], axis=1)
        return _dot(k_g, q_g)

    tasks = [(b, g) for b in range(tq // w) for g in range(SWA_KV_HEADS)]
    s_refs[0][...] = scores(*tasks[0])
    for n, (b, g) in enumerate(tasks):
        if n + 1 < len(tasks):
            s_refs[(n + 1) % len(s_refs)][...] = scores(*tasks[n + 1])
        st = s_refs[n % len(s_refs)][...]
        r0 = b * w
        heads = range(g * grp, (g + 1) * grp)
        if g == 0:
            first_key = jnp.where(i * (tq // w) + b > 0, 0, w)
            neg = jnp.where(band & (key >= first_key), 0.0, -jnp.inf)
            neg = jnp.concatenate([neg] * grp, axis=1)
        st = st + neg
        sink = jnp.concatenate(
            [jnp.full((1, w), sinks_ref[h] * LOG2E, F32) for h in heads], axis=1)
        m = jnp.maximum(jnp.max(st, axis=0, keepdims=True), sink)
        p = jnp.exp2(st - m)
        denom = jnp.sum(p, axis=0, keepdims=True) + jnp.exp2(sink - m)
        vt_g = vtwin_ref[g * hd:(g + 1) * hd, r0:r0 + 2 * w]
        ot = _dot(vt_g, p.astype(BF16)) / denom
        for c, h in enumerate(heads):
            ot_ref[h * hd:(h + 1) * hd, r0:r0 + w] = ot[:, c * w:(c + 1) * w]

    o = ot_ref[...].T.astype(BF16)
    y = _dot(o, w_ref[...]) + b_ref[...]
    z = DEEPNORM_ALPHA * x_ref[...] + g_ref[...] * y
    y_ref[...] = _layer_norm(z, lng_ref[...], lnb_ref[...])


def _swa_attn(sinks, qt, k, vt, x, gate, w_o, b_o, ln_g, ln_b):
    s, d = x.shape
    tq = SWA_TQ
    w = SWA_WINDOW
    nq = qt.shape[0]
    nk = k.shape[1]
    nv = vt.shape[0]
    r = tq // w
    row = lambda i: (i, 0)
    col = lambda i: (0, i)
    prev_row = lambda i: (jnp.maximum(i * r - 1, 0), 0)
    prev_col = lambda i: (0, jnp.maximum(i * r - 1, 0))
    return pl.pallas_call(
        _swa_attn_kernel,
        out_shape=jax.ShapeDtypeStruct((s, d), F32),
        grid=(s // tq,),
        in_specs=[
            pl.BlockSpec(memory_space=pltpu.SMEM),
            pl.BlockSpec((nq, tq), col),
            pl.BlockSpec((w, nk), prev_row), pl.BlockSpec((tq, nk), row),
            pl.BlockSpec((nv, w), prev_col), pl.BlockSpec((nv, tq), col),
            pl.BlockSpec((tq, d), row),
            _const_spec((1, d)), _const_spec(w_o.shape), _const_spec((1, d)),
            _const_spec((1, d)), _const_spec((1, d)),
        ],
        out_specs=pl.BlockSpec((tq, d), row),
        scratch_shapes=[
            pltpu.VMEM((tq + w, nk), BF16), pltpu.VMEM((nv, tq + w), BF16),
            pltpu.VMEM((nq, tq), F32),
            *[pltpu.VMEM((2 * w, w * SWA_HEADS // SWA_KV_HEADS), F32)] * SWA_SBUFS],
        compiler_params=_params(1),
        name="swa_attn",
    )(sinks, qt, k, k, vt, vt, x, gate, w_o, b_o, ln_g, ln_b)


def _rope_inv(rot_dim):
    return ROPE_THETA ** (-jnp.arange(0, rot_dim, 2, dtype=F32) / rot_dim)


def _rope_tables(positions, rot_dim):
    ang = _rope_inv(rot_dim)[:, None] * positions.astype(F32)[None, :]
    return jnp.cos(ang), jnp.sin(ang)


def _mla_weights(w_in, q_norm, w_q_b, kv_norm, w_kv_b, w_o):
    d = D_MODEL
    w_in_p = jnp.concatenate([w_in, jnp.zeros((d, 64), F32)], axis=1).astype(BF16)
    wkv = w_kv_b.reshape(MLA_KV_RANK, MLA_HEADS, MLA_NOPE + MLA_V)
    wk = wkv[:, :, :MLA_NOPE].reshape(MLA_KV_RANK, MLA_HEADS * MLA_NOPE).astype(BF16)
    wvt = wkv[:, :, MLA_NOPE:].reshape(MLA_KV_RANK, MLA_HEADS * MLA_V).T.astype(BF16)
    return {
        "w_in": w_in_p, "q_norm": q_norm.reshape(1, -1), "wqt": w_q_b.T.astype(BF16),
        "kv_norm": kv_norm.reshape(1, -1), "wk": wk, "wvt": wvt, "w_o": w_o.astype(BF16),
    }


def _swa_weights(w_qkv, b_qkv, w_o):
    nq = SWA_HEADS * SWA_HEAD_DIM
    nkv = SWA_KV_HEADS * SWA_HEAD_DIM

    def pad_heads(t):
        lead = t.shape[:-1]
        t = t.reshape(lead + (SWA_KV_HEADS, SWA_HEAD_DIM))
        t = jnp.concatenate([t, jnp.zeros_like(t)], axis=-1)
        return t.reshape(lead + (SWA_KV_HEADS * LANES,))

    def relayout(t):
        return jnp.concatenate([t[..., :nq], pad_heads(t[..., nq:nq + nkv]), t[..., nq + nkv:]], axis=-1)

    return {
        "w_qkv_t": relayout(w_qkv).T.astype(BF16),
        "b_qkv_t": relayout(b_qkv).reshape(-1, 1),
        "w_o": w_o.astype(BF16),
    }


def kernel(x, c, positions, ada_w, ada_b, ln_mix_g, ln_mix_b, ln_ffn_g, ln_ffn_b, ffn_w_gate, ffn_w_up, ffn_w_down, mla_w_in, mla_q_norm, mla_w_q_b, mla_kv_norm, mla_w_kv_b, mla_w_o, swa_w_qkv, swa_b_qkv, swa_sinks, swa_w_o, swa_b_o):
    b, s, d = x.shape
    assert (b, s, d) == (1, SEQ, D_MODEL)
    xs = x.reshape(s, d)
    pos = positions.reshape(s)
    mod = _modulation(c, ada_w, ada_b)
    mla_tabs = _rope_tables(pos, MLA_ROPE)
    swa_tabs = _rope_tables(pos, SWA_ROT)
    vec = lambda t: t.reshape(1, d)
    wg, wu, wd = ffn_w_gate.astype(BF16), ffn_w_up.astype(BF16), ffn_w_down.astype(BF16)
    for i in range(DEPTH):
        sh_m, sc_m, g_m, sh_f, sc_f, g_f = (mod[i, k] for k in range(6))
        j = i // 2
        mix = None
        if i % 2 == 0:
            w = _mla_weights(mla_w_in[j], mla_q_norm[j], mla_w_q_b[j], mla_kv_norm[j], mla_w_kv_b[j], mla_w_o[j])
            qt, k, vt = _mla_proj(xs, sc_m, sh_m, w, mla_tabs)
            o = _mla_attn(qt, k, vt)
            mix = (o, w["w_o"], g_m, vec(ln_mix_g[i]), vec(ln_mix_b[i]))
        else:
            w = _swa_weights(swa_w_qkv[j], swa_b_qkv[j], swa_w_o[j])
            qt, k, vt = _swa_proj(xs, sc_m, sh_m, w, swa_tabs)
            xs = _swa_attn(swa_sinks[j], qt, k, vt, xs, g_m, w["w_o"], vec(swa_b_o[j]),
                           vec(ln_mix_g[i]), vec(ln_mix_b[i]))
        xs = _ffn(xs, i, sc_f, sh_f, g_f, wg, wu, wd, vec(ln_ffn_g[i]), vec(ln_ffn_b[i]), mix=mix)
    return xs.reshape(b, s, d)
```

```python
import functools

import jax
import jax.numpy as jnp
from jax import lax
from jax.experimental import pallas as pl
from jax.experimental.pallas import tpu as pltpu

D_MODEL = 1024
SEQ = 16384
DEPTH = 4
ROPE_THETA = 500000.0
LN_EPS = 1e-5
RMS_EPS = 1e-6

MLA_HEADS = 8
MLA_NOPE = 128
MLA_ROPE = 64
MLA_V = 128
MLA_Q_RANK = 384
MLA_KV_RANK = 256
MLA_QK_PAD = 256
MLA_V_AUG = MLA_V + 16

SWA_HEADS = 16
SWA_KV_HEADS = 4
SWA_HEAD_DIM = 64
SWA_WINDOW = 128
SWA_ROT = SWA_HEAD_DIM // 4

D_FF = ((8 * D_MODEL + 3 * 256 - 1) // (3 * 256)) * 256
DEEPNORM_ALPHA = (2 * DEPTH) ** 0.25

LANES = 128
VMEM_LIMIT = 56 * 1024 * 1024

ROW_BLOCK = 512
MLA_T = 512
MLA_UNROLL = 16
MLA_SBUFS = 2
MLA_DIAG_UNROLL = 8
MLA_ATTN_VMEM = 58 * 1024 * 1024
SWA_TQ = 512
SWA_SBUFS = 4
LOG2E = 1.4426950408889634
MLA_Q_SCALE = (MLA_NOPE + MLA_ROPE) ** -0.5 * LOG2E
SWA_Q_SCALE = SWA_HEAD_DIM ** -0.5 * LOG2E

BF16 = jnp.bfloat16
F32 = jnp.float32


def _dot(a, b):
    return jnp.dot(a, b, preferred_element_type=F32)


def _dot_nt(a, b):
    return lax.dot_general(a, b, (((1,), (1,)), ((), ())), preferred_element_type=F32)


def _layer_norm(z, g, b):
    mu = jnp.mean(z, axis=-1, keepdims=True)
    zc = z - mu
    var = jnp.mean(zc * zc, axis=-1, keepdims=True)
    return zc * lax.rsqrt(var + LN_EPS) * g + b


def _rms_norm(z, g):
    return z * lax.rsqrt(jnp.mean(z * z, axis=-1, keepdims=True) + RMS_EPS) * g


def _const_spec(shape):
    nd = len(shape)
    return pl.BlockSpec(shape, lambda *_: (0,) * nd, pipeline_mode=pl.Buffered(1))


def _params(n_grid):
    return pltpu.CompilerParams(
        dimension_semantics=("arbitrary",) * n_grid, vmem_limit_bytes=VMEM_LIMIT)


def _mod_kernel(c_ref, w_ref, b_ref, o_ref):
    c = c_ref[...]
    cond = (c * jax.nn.sigmoid(c)).astype(BF16)
    cond8 = jnp.broadcast_to(cond, (8, D_MODEL))
    y = _dot(cond8, w_ref[0].astype(BF16))
    o_ref[0] = y[0:1] + b_ref[0]


def _modulation(c, ada_w, ada_b):
    d = D_MODEL
    out = pl.pallas_call(
        _mod_kernel,
        out_shape=jax.ShapeDtypeStruct((DEPTH, 1, 6 * d), F32),
        grid=(DEPTH, 6),
        in_specs=[
            pl.BlockSpec((1, d), lambda i, k: (0, 0)),
            pl.BlockSpec((1, d, d), lambda i, k: (i, 0, k)),
            pl.BlockSpec((1, 1, d), lambda i, k: (i, 0, k)),
        ],
        out_specs=pl.BlockSpec((1, 1, d), lambda i, k: (i, 0, k)),
        compiler_params=_params(2),
        name="adaln_mod",
    )(c, ada_w, ada_b.reshape(DEPTH, 1, 6 * d))
    return out.reshape(DEPTH, 6, 1, d)


def _mla_proj_kernel(x_ref, sc_ref, sh_ref, w_in_ref, qn_ref, wqt_ref, kvn_ref, wk_ref, wvt_ref,
                     ckt_ref, skt_ref, qt_ref, k_ref, vt_ref):
    tm = x_ref.shape[0]
    h = (x_ref[...] * (1.0 + sc_ref[...]) + sh_ref[...]).astype(BF16)
    lat = _dot(h, w_in_ref[...])
    q_lat = lat[:, :MLA_Q_RANK]
    kv_lat = lat[:, MLA_Q_RANK:MLA_Q_RANK + MLA_KV_RANK]
    kr = lat[:, MLA_Q_RANK + MLA_KV_RANK:]

    qn = _rms_norm(q_lat, qn_ref[...]).astype(BF16)
    qt = _dot_nt(wqt_ref[...], qn) * MLA_Q_SCALE
    ck = ckt_ref[...]
    sk = skt_ref[...]
    hw = MLA_NOPE + MLA_ROPE
    half = MLA_ROPE // 2
    for hd in range(MLA_HEADS):
        r0 = hd * hw
        x1 = qt[r0 + MLA_NOPE:r0 + MLA_NOPE + half]
        x2 = qt[r0 + MLA_NOPE + half:r0 + hw]
        qt_ref[hd, 0:MLA_NOPE, :] = qt[r0:r0 + MLA_NOPE].astype(BF16)
        qt_ref[hd, MLA_NOPE:MLA_NOPE + half, :] = (x1 * ck - x2 * sk).astype(BF16)
        qt_ref[hd, MLA_NOPE + half:hw, :] = (x2 * ck + x1 * sk).astype(BF16)
        qt_ref[hd, hw:, :] = jnp.zeros((MLA_QK_PAD - hw, tm), BF16)

    kvn = _rms_norm(kv_lat, kvn_ref[...]).astype(BF16)
    kn = _dot(kvn, wk_ref[...])
    krt = kr.T
    x1 = krt[0:half]
    x2 = krt[half:MLA_ROPE]
    k_rope = jnp.concatenate(
        [x1 * ck - x2 * sk, x2 * ck + x1 * sk, krt[MLA_ROPE:]], axis=0).T.astype(BF16)
    for hd in range(MLA_HEADS):
        k_ref[:, hd * MLA_QK_PAD:hd * MLA_QK_PAD + MLA_NOPE] = kn[:, hd * MLA_NOPE:(hd + 1) * MLA_NOPE].astype(BF16)
        k_ref[:, hd * MLA_QK_PAD + MLA_NOPE:(hd + 1) * MLA_QK_PAD] = k_rope

    vt = _dot_nt(wvt_ref[...], kvn)
    row = lax.broadcasted_iota(jnp.int32, (MLA_V_AUG - MLA_V, tm), 0)
    ones_row = jnp.where(row == 0, 1.0, 0.0).astype(BF16)
    for hd in range(MLA_HEADS):
        vt_ref[hd, 0:MLA_V, :] = vt[hd * MLA_V:(hd + 1) * MLA_V].astype(BF16)
        vt_ref[hd, MLA_V:, :] = ones_row


def _mla_proj(x, sc, sh, w, tabs):
    s, d = x.shape
    tm = ROW_BLOCK
    hk = MLA_HEADS * MLA_QK_PAD
    row = lambda i: (i, 0)
    col = lambda i: (0, i)
    return pl.pallas_call(
        _mla_proj_kernel,
        out_shape=(
            jax.ShapeDtypeStruct((MLA_HEADS, MLA_QK_PAD, s), BF16),
            jax.ShapeDtypeStruct((s, hk), BF16),
            jax.ShapeDtypeStruct((MLA_HEADS, MLA_V_AUG, s), BF16),
        ),
        grid=(s // tm,),
        in_specs=[
            pl.BlockSpec((tm, d), row),
            _const_spec((1, d)), _const_spec((1, d)),
            _const_spec(w["w_in"].shape), _const_spec(w["q_norm"].shape), _const_spec(w["wqt"].shape),
            _const_spec(w["kv_norm"].shape), _const_spec(w["wk"].shape), _const_spec(w["wvt"].shape),
            pl.BlockSpec((MLA_ROPE // 2, tm), col), pl.BlockSpec((MLA_ROPE // 2, tm), col),
        ],
        out_specs=(
            pl.BlockSpec((MLA_HEADS, MLA_QK_PAD, tm), lambda i: (0, 0, i)),
            pl.BlockSpec((tm, hk), row),
            pl.BlockSpec((MLA_HEADS, MLA_V_AUG, tm), lambda i: (0, 0, i)),
        ),
        compiler_params=_params(1),
        name="mla_proj",
    )(x, sc, sh, w["w_in"], w["q_norm"], w["wqt"], w["kv_norm"], w["wk"], w["wvt"],
      tabs[0], tabs[1])


def _mla_attn_kernel(qt_ref, k_ref, vt_ref, o_ref, *scratch):
    t = MLA_T
    nq = qt_ref.shape[2] // t
    n_full = nq * (nq - 1) // 2
    assert n_full % MLA_UNROLL == 0 and MLA_UNROLL % MLA_SBUFS == 0
    assert nq % MLA_DIAG_UNROLL == 0 and MLA_DIAG_UNROLL % MLA_SBUFS == 0
    bufs = scratch[:MLA_SBUFS]
    mbufs = scratch[MLA_SBUFS:2 * MLA_SBUFS]
    m_ref, acc_ref = scratch[2 * MLA_SBUFS:]
    m_ref[...] = jnp.full(m_ref.shape, -jnp.inf, F32)
    acc_ref[...] = jnp.zeros(acc_ref.shape, F32)

    def scores(slot, qi, j, diag):
        q0 = pl.multiple_of(qi * t, t)
        k0 = pl.multiple_of(j * t, t)
        st = _dot(k_ref[pl.ds(k0, t), :], qt_ref[0, :, pl.ds(q0, t)])
        bufs[slot][...] = st
        if not diag:
            mbufs[slot][...] = jnp.max(st.reshape(t // 8, 8, t), axis=0)

    def update(slot, qi, j, diag):
        k0 = pl.multiple_of(j * t, t)
        st = bufs[slot][...]
        m_prev = m_ref[qi]
        if diag:
            key = lax.broadcasted_iota(jnp.int32, (t, t), 0)
            qry = lax.broadcasted_iota(jnp.int32, (t, t), 1)
            st = jnp.where(key <= qry, st, -jnp.inf)
            m_blk = jnp.max(st, axis=0, keepdims=True)
        else:
            m_blk = jnp.max(mbufs[slot][...], axis=0, keepdims=True)
        m_new = jnp.maximum(m_prev, m_blk)
        alpha = jnp.exp2(m_prev - m_new)
        p = jnp.exp2(st - m_new).astype(BF16)
        acc_new = alpha * acc_ref[qi] + _dot(vt_ref[0, :, pl.ds(k0, t)], p)
        if diag:
            q0 = pl.multiple_of(qi * t, t)
            o_ref[pl.ds(q0, t), :] = (acc_new[0:MLA_V] / acc_new[MLA_V:MLA_V + 1]).T.astype(BF16)
        else:
            m_ref[qi] = m_new
            acc_ref[qi] = acc_new

    scores(0, 1, 0, False)

    def full_body(_, carry):
        qi, j = carry
        for u in range(MLA_UNROLL):
            last = j + 1 == qi
            qi_n = jnp.where(last, qi + 1, qi)
            j_n = jnp.where(last, 0, j + 1)
            scores((u + 1) % MLA_SBUFS, jnp.minimum(qi_n, nq - 1), j_n, False)
            update(u % MLA_SBUFS, qi, j, False)
            qi, j = qi_n, j_n
        return qi, j

    lax.fori_loop(0, n_full // MLA_UNROLL, full_body, (jnp.int32(1), jnp.int32(0)))

    scores(0, 0, 0, True)

    def diag_body(i, carry):
        for u in range(MLA_DIAG_UNROLL):
            qi = MLA_DIAG_UNROLL * i + u
            nxt = jnp.minimum(qi + 1, nq - 1)
            scores((u + 1) % MLA_SBUFS, nxt, nxt, True)
            update(u % MLA_SBUFS, qi, qi, True)
        return carry

    lax.fori_loop(0, nq // MLA_DIAG_UNROLL, diag_body, 0)


def _mla_attn(qt, k, vt):
    s = k.shape[0]
    t = MLA_T
    return pl.pallas_call(
        _mla_attn_kernel,
        out_shape=jax.ShapeDtypeStruct((s, MLA_HEADS * MLA_V), BF16),
        grid=(MLA_HEADS,),
        in_specs=[
            pl.BlockSpec((1, MLA_QK_PAD, s), lambda h: (h, 0, 0)),
            pl.BlockSpec((s, MLA_QK_PAD), lambda h: (0, h)),
            pl.BlockSpec((1, MLA_V_AUG, s), lambda h: (h, 0, 0)),
        ],
        out_specs=pl.BlockSpec((s, MLA_V), lambda h: (0, h), pipeline_mode=pl.Buffered(1)),
        scratch_shapes=[
            *[pltpu.VMEM((t, t), F32)] * MLA_SBUFS,
            *[pltpu.VMEM((8, t), F32)] * MLA_SBUFS,
            pltpu.VMEM((s // t, 1, t), F32), pltpu.VMEM((s // t, MLA_V_AUG, t), F32)],
        compiler_params=pltpu.CompilerParams(
            dimension_semantics=("arbitrary",), vmem_limit_bytes=MLA_ATTN_VMEM),
        name="mla_attn",
    )(qt, k, vt)


def _ffn_kernel(*refs, with_mix):
    if with_mix:
        o_ref, wo_ref, gm_ref, lnmg_ref, lnmb_ref = refs[:5]
        refs = refs[5:]
    x_ref, sc_ref, sh_ref, g_ref, wg_ref, wu_ref, wd_ref, lng_ref, lnb_ref, y_ref = refs
    x = x_ref[...]
    if with_mix:
        z = DEEPNORM_ALPHA * x + gm_ref[...] * _dot(o_ref[...], wo_ref[...])
        x = _layer_norm(z, lnmg_ref[...], lnmb_ref[...])
    h = (x * (1.0 + sc_ref[...]) + sh_ref[...]).astype(BF16)
    gate = _dot(h, wg_ref[0])
    up = _dot(h, wu_ref[0])
    act = (gate * jax.nn.sigmoid(gate) * up).astype(BF16)
    y = _dot(act, wd_ref[0])
    z = DEEPNORM_ALPHA * x + g_ref[...] * y
    y_ref[...] = _layer_norm(z, lng_ref[...], lnb_ref[...])


def _ffn(x, layer, sc, sh, gate, wg, wu, wd, ln_g, ln_b, mix=None):
    s, d = x.shape
    tm = ROW_BLOCK
    row = lambda i: (i, 0)
    vec = _const_spec((1, d))
    stacked = lambda w: pl.BlockSpec((1,) + w.shape[1:], lambda i: (layer, 0, 0), pipeline_mode=pl.Buffered(1))
    mix_specs, mix_args = [], []
    if mix is not None:
        o, w_o = mix[0], mix[1]
        mix_specs = [pl.BlockSpec((tm, o.shape[1]), row), _const_spec(w_o.shape), vec, vec, vec]
        mix_args = list(mix)
    return pl.pallas_call(
        functools.partial(_ffn_kernel, with_mix=mix is not None),
        out_shape=jax.ShapeDtypeStruct((s, d), F32),
        grid=(s // tm,),
        in_specs=mix_specs + [
            pl.BlockSpec((tm, d), row), vec, vec, vec,
            stacked(wg), stacked(wu), stacked(wd), vec, vec,
        ],
        out_specs=pl.BlockSpec((tm, d), row),
        compiler_params=_params(1),
        name="mix_ffn" if mix is not None else "ffn",
    )(*mix_args, x, sc, sh, gate, wg, wu, wd, ln_g, ln_b)


def _rope_rows(xt, c, s):
    half = SWA_ROT // 2
    x1 = xt[0:half]
    x2 = xt[half:SWA_ROT]
    return jnp.concatenate([x1 * c - x2 * s, x2 * c + x1 * s, xt[SWA_ROT:]], axis=0)


def _swa_proj_kernel(x_ref, sc_ref, sh_ref, wt_ref, b_ref, ct_ref, st_ref, qt_ref, k_ref, vt_ref):
    h = (x_ref[...] * (1.0 + sc_ref[...]) + sh_ref[...]).astype(BF16)
    qkvt = _dot_nt(wt_ref[...], h) + b_ref[...]
    c = ct_ref[...]
    s = st_ref[...]
    hd = SWA_HEAD_DIM
    nq = SWA_HEADS * hd
    nk = SWA_KV_HEADS * LANES
    for i in range(SWA_HEADS):
        qt_ref[i * hd:(i + 1) * hd, :] = (_rope_rows(qkvt[i * hd:(i + 1) * hd], c, s) * SWA_Q_SCALE).astype(BF16)
    for g in range(SWA_KV_HEADS):
        r0 = nq + g * LANES
        kt = jnp.concatenate([_rope_rows(qkvt[r0:r0 + hd], c, s), qkvt[r0 + hd:r0 + LANES]], axis=0)
        k_ref[:, g * LANES:(g + 1) * LANES] = kt.T.astype(BF16)
    vt_ref[...] = qkvt[nq + nk:].astype(BF16)


def _swa_proj(x, sc, sh, w, tabs):
    s, d = x.shape
    tm = ROW_BLOCK
    nq = SWA_HEADS * SWA_HEAD_DIM
    nk = SWA_KV_HEADS * LANES
    nv = SWA_KV_HEADS * SWA_HEAD_DIM
    row = lambda i: (i, 0)
    col = lambda i: (0, i)
    return pl.pallas_call(
        _swa_proj_kernel,
        out_shape=(
            jax.ShapeDtypeStruct((nq, s), BF16),
            jax.ShapeDtypeStruct((s, nk), BF16),
            jax.ShapeDtypeStruct((nv, s), BF16),
        ),
        grid=(s // tm,),
        in_specs=[
            pl.BlockSpec((tm, d), row),
            _const_spec((1, d)), _const_spec((1, d)),
            _const_spec(w["w_qkv_t"].shape), _const_spec(w["b_qkv_t"].shape),
            pl.BlockSpec((SWA_ROT // 2, tm), col), pl.BlockSpec((SWA_ROT // 2, tm), col),
        ],
        out_specs=(pl.BlockSpec((nq, tm), col), pl.BlockSpec((tm, nk), row), pl.BlockSpec((nv, tm), col)),
        compiler_params=_params(1),
        name="swa_proj",
    )(x, sc, sh, w["w_qkv_t"], w["b_qkv_t"], tabs[0], tabs[1])


def _swa_attn_kernel(sinks_ref, qt_ref, kp_ref, kc_ref, vtp_ref, vtc_ref, x_ref, g_ref, w_ref, b_ref,
                     lng_ref, lnb_ref, y_ref, kwin_ref, vtwin_ref, ot_ref, *s_refs):
    i = pl.program_id(0)
    w = SWA_WINDOW
    tq = SWA_TQ
    hd = SWA_HEAD_DIM
    grp = SWA_HEADS // SWA_KV_HEADS
    kwin_ref[0:w, :] = kp_ref[...]
    kwin_ref[w:, :] = kc_ref[...]
    vtwin_ref[:, 0:w] = vtp_ref[...]
    vtwin_ref[:, w:] = vtc_ref[...]

    key = lax.broadcasted_iota(jnp.int32, (2 * w, w), 0)
    qry = lax.broadcasted_iota(jnp.int32, (2 * w, w), 1)
    band = (key > qry) & (key <= qry + w)

    def scores(b, g):
        r0 = b * w
        k_g = kwin_ref[r0:r0 + 2 * w, g * LANES:g * LANES + hd]
        q_g = jnp.concatenate(
            [qt_ref[h * hd:(h + 1) * hd, r0:r0 + w] for h in range(g * grp, (g + 1) * grp)], axis=1)
        return _dot(k_g, q_g)

    tasks = [(b, g) for b in range(tq // w) for g in range(SWA_KV_HEADS)]
    s_refs[0][...] = scores(*tasks[0])
    for n, (b, g) in enumerate(tasks):
        if n + 1 < len(tasks):
            s_refs[(n + 1) % len(s_refs)][...] = scores(*tasks[n + 1])
        st = s_refs[n % len(s_refs)][...]
        r0 = b * w
        heads = range(g * grp, (g + 1) * grp)
        if g == 0:
            first_key = jnp.where(i * (tq // w) + b > 0, 0, w)
            neg = jnp.where(band & (key >= first_key), 0.0, -jnp.inf)
            neg = jnp.concatenate([neg] * grp, axis=1)
        st = st + neg
        sink = jnp.concatenate(
            [jnp.full((1, w), sinks_ref[h] * LOG2E, F32) for h in heads], axis=1)
        m = jnp.maximum(jnp.max(st, axis=0, keepdims=True), sink)
        p = jnp.exp2(st - m)
        denom = jnp.sum(p, axis=0, keepdims=True) + jnp.exp2(sink - m)
        vt_g = vtwin_ref[g * hd:(g + 1) * hd, r0:r0 + 2 * w]
        ot = _dot(vt_g, p.astype(BF16)) / denom
        for c, h in enumerate(heads):
            ot_ref[h * hd:(h + 1) * hd, r0:r0 + w] = ot[:, c * w:(c + 1) * w]

    o = ot_ref[...].T.astype(BF16)
    y = _dot(o, w_ref[...]) + b_ref[...]
    z = DEEPNORM_ALPHA * x_ref[...] + g_ref[...] * y
    y_ref[...] = _layer_norm(z, lng_ref[...], lnb_ref[...])


def _swa_attn(sinks, qt, k, vt, x, gate, w_o, b_o, ln_g, ln_b):
    s, d = x.shape
    tq = SWA_TQ
    w = SWA_WINDOW
    nq = qt.shape[0]
    nk = k.shape[1]
    nv = vt.shape[0]
    r = tq // w
    row = lambda i: (i, 0)
    col = lambda i: (0, i)
    prev_row = lambda i: (jnp.maximum(i * r - 1, 0), 0)
    prev_col = lambda i: (0, jnp.maximum(i * r - 1, 0))
    return pl.pallas_call(
        _swa_attn_kernel,
        out_shape=jax.ShapeDtypeStruct((s, d), F32),
        grid=(s // tq,),
        in_specs=[
            pl.BlockSpec(memory_space=pltpu.SMEM),
            pl.BlockSpec((nq, tq), col),
            pl.BlockSpec((w, nk), prev_row), pl.BlockSpec((tq, nk), row),
            pl.BlockSpec((nv, w), prev_col), pl.BlockSpec((nv, tq), col),
            pl.BlockSpec((tq, d), row),
            _const_spec((1, d)), _const_spec(w_o.shape), _const_spec((1, d)),
            _const_spec((1, d)), _const_spec((1, d)),
        ],
        out_specs=pl.BlockSpec((tq, d), row),
        scratch_shapes=[
            pltpu.VMEM((tq + w, nk), BF16), pltpu.VMEM((nv, tq + w), BF16),
            pltpu.VMEM((nq, tq), F32),
            *[pltpu.VMEM((2 * w, w * SWA_HEADS // SWA_KV_HEADS), F32)] * SWA_SBUFS],
        compiler_params=_params(1),
        name="swa_attn",
    )(sinks, qt, k, k, vt, vt, x, gate, w_o, b_o, ln_g, ln_b)


def _rope_inv(rot_dim):
    return ROPE_THETA ** (-jnp.arange(0, rot_dim, 2, dtype=F32) / rot_dim)


def _rope_tables(positions, rot_dim):
    ang = _rope_inv(rot_dim)[:, None] * positions.astype(F32)[None, :]
    return jnp.cos(ang), jnp.sin(ang)


def _mla_weights(w_in, q_norm, w_q_b, kv_norm, w_kv_b, w_o):
    d = D_MODEL
    w_in_p = jnp.concatenate([w_in, jnp.zeros((d, 64), F32)], axis=1).astype(BF16)
    wkv = w_kv_b.reshape(MLA_KV_RANK, MLA_HEADS, MLA_NOPE + MLA_V)
    wk = wkv[:, :, :MLA_NOPE].reshape(MLA_KV_RANK, MLA_HEADS * MLA_NOPE).astype(BF16)
    wvt = wkv[:, :, MLA_NOPE:].reshape(MLA_KV_RANK, MLA_HEADS * MLA_V).T.astype(BF16)
    return {
        "w_in": w_in_p, "q_norm": q_norm.reshape(1, -1), "wqt": w_q_b.T.astype(BF16),
        "kv_norm": kv_norm.reshape(1, -1), "wk": wk, "wvt": wvt, "w_o": w_o.astype(BF16),
    }


def _swa_weights(w_qkv, b_qkv, w_o):
    nq = SWA_HEADS * SWA_HEAD_DIM
    nkv = SWA_KV_HEADS * SWA_HEAD_DIM

    def pad_heads(t):
        lead = t.shape[:-1]
        t = t.reshape(lead + (SWA_KV_HEADS, SWA_HEAD_DIM))
        t = jnp.concatenate([t, jnp.zeros_like(t)], axis=-1)
        return t.reshape(lead + (SWA_KV_HEADS * LANES,))

    def relayout(t):
        return jnp.concatenate([t[..., :nq], pad_heads(t[..., nq:nq + nkv]), t[..., nq + nkv:]], axis=-1)

    return {
        "w_qkv_t": relayout(w_qkv).T.astype(BF16),
        "b_qkv_t": relayout(b_qkv).reshape(-1, 1),
        "w_o": w_o.astype(BF16),
    }


def kernel(x, c, positions, ada_w, ada_b, ln_mix_g, ln_mix_b, ln_ffn_g, ln_ffn_b, ffn_w_gate, ffn_w_up, ffn_w_down, mla_w_in, mla_q_norm, mla_w_q_b, mla_kv_norm, mla_w_kv_b, mla_w_o, swa_w_qkv, swa_b_qkv, swa_sinks, swa_w_o, swa_b_o):
    b, s, d = x.shape
    assert (b, s, d) == (1, SEQ, D_MODEL)
    xs = x.reshape(s, d)
    pos = positions.reshape(s)
    mod = _modulation(c, ada_w, ada_b)
    mla_tabs = _rope_tables(pos, MLA_ROPE)
    swa_tabs = _rope_tables(pos, SWA_ROT)
    vec = lambda t: t.reshape(1, d)
    wg, wu, wd = ffn_w_gate.astype(BF16), ffn_w_up.astype(BF16), ffn_w_down.astype(BF16)
    for i in range(DEPTH):
        sh_m, sc_m, g_m, sh_f, sc_f, g_f = (mod[i, k] for k in range(6))
        j = i // 2
        mix = None
        if i % 2 == 0:
            w = _mla_weights(mla_w_in[j], mla_q_norm[j], mla_w_q_b[j], mla_kv_norm[j], mla_w_kv_b[j], mla_w_o[j])
            qt, k, vt = _mla_proj(xs, sc_m, sh_m, w, mla_tabs)
            o = _mla_attn(qt, k, vt)
            mix = (o, w["w_o"], g_m, vec(ln_mix_g[i]), vec(ln_mix_b[i]))
        else:
            w = _swa_weights(swa_w_qkv[j], swa_b_qkv[j], swa_w_o[j])
            qt, k, vt = _swa_proj(xs, sc_m, sh_m, w, swa_tabs)
            xs = _swa_attn(swa_sinks[j], qt, k, vt, xs, g_m, w["w_o"], vec(swa_b_o[j]),
                           vec(ln_mix_g[i]), vec(ln_mix_b[i]))
        xs = _ffn(xs, i, sc_f, sh_f, g_f, wg, wu, wd, vec(ln_ffn_g[i]), vec(ln_ffn_b[i]), mix=mix)
    return xs.reshape(b, s, d)
```

```python
import functools

import jax
import jax.numpy as jnp
from jax import lax
from jax.experimental import pallas as pl
from jax.experimental.pallas import tpu as pltpu

D_MODEL = 1024
SEQ = 16384
DEPTH = 4
ROPE_THETA = 500000.0
LN_EPS = 1e-5
RMS_EPS = 1e-6

MLA_HEADS = 8
MLA_NOPE = 128
MLA_ROPE = 64
MLA_V = 128
MLA_Q_RANK = 384
MLA_KV_RANK = 256
MLA_QK = MLA_NOPE + MLA_ROPE
MLA_QK_PAD = 256
MLA_V_AUG = MLA_V + 16

SWA_HEADS = 16
SWA_KV_HEADS = 4
SWA_HEAD_DIM = 64
SWA_WINDOW = 128
SWA_ROT = SWA_HEAD_DIM // 4

D_FF = ((8 * D_MODEL + 3 * 256 - 1) // (3 * 256)) * 256
DEEPNORM_ALPHA = (2 * DEPTH) ** 0.25

LANES = 128
VMEM_LIMIT = 56 * 1024 * 1024

ROW_BLOCK = 512
MLA_T = 512
MLA_UNROLL = 16
MLA_SBUFS = 2
MLA_DIAG_UNROLL = 8
MLA_ATTN_VMEM = 58 * 1024 * 1024
SWA_TQ = 512
SWA_SBUFS = 4
LOG2E = 1.4426950408889634
MLA_Q_SCALE = (MLA_NOPE + MLA_ROPE) ** -0.5 * LOG2E
SWA_Q_SCALE = SWA_HEAD_DIM ** -0.5 * LOG2E

BF16 = jnp.bfloat16
F32 = jnp.float32


def _dot(a, b):
    return jnp.dot(a, b, preferred_element_type=F32)


def _dot_nt(a, b):
    return lax.dot_general(a, b, (((1,), (1,)), ((), ())), preferred_element_type=F32)


def _layer_norm(z, g, b):
    mu = jnp.mean(z, axis=-1, keepdims=True)
    zc = z - mu
    var = jnp.mean(zc * zc, axis=-1, keepdims=True)
    return zc * lax.rsqrt(var + LN_EPS) * g + b


def _rms_norm(z, g):
    return z * lax.rsqrt(jnp.mean(z * z, axis=-1, keepdims=True) + RMS_EPS) * g


def _const_spec(shape):
    nd = len(shape)
    return pl.BlockSpec(shape, lambda *_: (0,) * nd, pipeline_mode=pl.Buffered(1))


def _params(n_grid):
    return pltpu.CompilerParams(
        dimension_semantics=("arbitrary",) * n_grid, vmem_limit_bytes=VMEM_LIMIT)


def _mod_kernel(c_ref, w_ref, b_ref, o_ref):
    c = c_ref[...]
    cond = (c * jax.nn.sigmoid(c)).astype(BF16)
    cond8 = jnp.broadcast_to(cond, (8, D_MODEL))
    y = _dot(cond8, w_ref[0].astype(BF16))
    o_ref[0] = y[0:1] + b_ref[0]


def _modulation(c, ada_w, ada_b):
    d = D_MODEL
    out = pl.pallas_call(
        _mod_kernel,
        out_shape=jax.ShapeDtypeStruct((DEPTH, 1, 6 * d), F32),
        grid=(DEPTH, 6),
        in_specs=[
            pl.BlockSpec((1, d), lambda i, k: (0, 0)),
            pl.BlockSpec((1, d, d), lambda i, k: (i, 0, k)),
            pl.BlockSpec((1, 1, d), lambda i, k: (i, 0, k)),
        ],
        out_specs=pl.BlockSpec((1, 1, d), lambda i, k: (i, 0, k)),
        compiler_params=_params(2),
        name="adaln_mod",
    )(c, ada_w, ada_b.reshape(DEPTH, 1, 6 * d))
    return out.reshape(DEPTH, 6, 1, d)


def _mla_proj_kernel(x_ref, sc_ref, sh_ref, w_in_ref, qn_ref, wqt_ref, kvn_ref, wk_ref, wvt_ref,
                     ckt_ref, skt_ref, qt_ref, k_ref, vt_ref):
    tm = x_ref.shape[0]
    h = (x_ref[...] * (1.0 + sc_ref[...]) + sh_ref[...]).astype(BF16)
    lat = _dot(h, w_in_ref[...])
    q_lat = lat[:, :MLA_Q_RANK]
    kv_lat = lat[:, MLA_Q_RANK:MLA_Q_RANK + MLA_KV_RANK]
    kr = lat[:, MLA_Q_RANK + MLA_KV_RANK:]

    qn = _rms_norm(q_lat, qn_ref[...]).astype(BF16)
    qt = _dot_nt(wqt_ref[...], qn) * MLA_Q_SCALE
    ck = ckt_ref[...]
    sk = skt_ref[...]
    hw = MLA_NOPE + MLA_ROPE
    half = MLA_ROPE // 2
    for hd in range(MLA_HEADS):
        r0 = hd * hw
        x1 = qt[r0 + MLA_NOPE:r0 + MLA_NOPE + half]
        x2 = qt[r0 + MLA_NOPE + half:r0 + hw]
        qt_ref[hd, 0:MLA_NOPE, :] = qt[r0:r0 + MLA_NOPE].astype(BF16)
        qt_ref[hd, MLA_NOPE:MLA_NOPE + half, :] = (x1 * ck - x2 * sk).astype(BF16)
        qt_ref[hd, MLA_NOPE + half:hw, :] = (x2 * ck + x1 * sk).astype(BF16)

    kvn = _rms_norm(kv_lat, kvn_ref[...]).astype(BF16)
    kn = _dot(kvn, wk_ref[...])
    krt = kr.T
    x1 = krt[0:half]
    x2 = krt[half:MLA_ROPE]
    k_rope = jnp.concatenate(
        [x1 * ck - x2 * sk, x2 * ck + x1 * sk, krt[MLA_ROPE:]], axis=0).T.astype(BF16)
    for hd in range(MLA_HEADS):
        k_ref[:, hd * MLA_QK_PAD:hd * MLA_QK_PAD + MLA_NOPE] = kn[:, hd * MLA_NOPE:(hd + 1) * MLA_NOPE].astype(BF16)
        k_ref[:, hd * MLA_QK_PAD + MLA_NOPE:(hd + 1) * MLA_QK_PAD] = k_rope

    vt = _dot_nt(wvt_ref[...], kvn)
    row = lax.broadcasted_iota(jnp.int32, (MLA_V_AUG - MLA_V, tm), 0)
    ones_row = jnp.where(row == 0, 1.0, 0.0).astype(BF16)
    for hd in range(MLA_HEADS):
        vt_ref[hd, 0:MLA_V, :] = vt[hd * MLA_V:(hd + 1) * MLA_V].astype(BF16)
        vt_ref[hd, MLA_V:, :] = ones_row


def _mla_proj(x, sc, sh, w, tabs):
    s, d = x.shape
    tm = ROW_BLOCK
    hk = MLA_HEADS * MLA_QK_PAD
    row = lambda i: (i, 0)
    col = lambda i: (0, i)
    return pl.pallas_call(
        _mla_proj_kernel,
        out_shape=(
            jax.ShapeDtypeStruct((MLA_HEADS, MLA_QK, s), BF16),
            jax.ShapeDtypeStruct((s, hk), BF16),
            jax.ShapeDtypeStruct((MLA_HEADS, MLA_V_AUG, s), BF16),
        ),
        grid=(s // tm,),
        in_specs=[
            pl.BlockSpec((tm, d), row),
            _const_spec((1, d)), _const_spec((1, d)),
            _const_spec(w["w_in"].shape), _const_spec(w["q_norm"].shape), _const_spec(w["wqt"].shape),
            _const_spec(w["kv_norm"].shape), _const_spec(w["wk"].shape), _const_spec(w["wvt"].shape),
            pl.BlockSpec((MLA_ROPE // 2, tm), col), pl.BlockSpec((MLA_ROPE // 2, tm), col),
        ],
        out_specs=(
            pl.BlockSpec((MLA_HEADS, MLA_QK, tm), lambda i: (0, 0, i)),
            pl.BlockSpec((tm, hk), row),
            pl.BlockSpec((MLA_HEADS, MLA_V_AUG, tm), lambda i: (0, 0, i)),
        ),
        compiler_params=_params(1),
        name="mla_proj",
    )(x, sc, sh, w["w_in"], w["q_norm"], w["wqt"], w["kv_norm"], w["wk"], w["wvt"],
      tabs[0], tabs[1])


def _mla_attn_kernel(qt_ref, k_ref, vt_ref, o_ref, *scratch):
    t = MLA_T
    nq = qt_ref.shape[2] // t
    n_full = nq * (nq - 1) // 2
    assert n_full % MLA_UNROLL == 0 and MLA_UNROLL % MLA_SBUFS == 0
    assert nq % MLA_DIAG_UNROLL == 0 and MLA_DIAG_UNROLL % MLA_SBUFS == 0
    bufs = scratch[:MLA_SBUFS]
    mbufs = scratch[MLA_SBUFS:2 * MLA_SBUFS]
    m_ref, acc_ref = scratch[2 * MLA_SBUFS:]
    m_ref[...] = jnp.full(m_ref.shape, -jnp.inf, F32)
    acc_ref[...] = jnp.zeros(acc_ref.shape, F32)

    def scores(slot, qi, j, diag):
        q0 = pl.multiple_of(qi * t, t)
        k0 = pl.multiple_of(j * t, t)
        st = _dot(k_ref[pl.ds(k0, t), 0:MLA_QK], qt_ref[0, :, pl.ds(q0, t)])
        bufs[slot][...] = st
        if not diag:
            mbufs[slot][...] = jnp.max(st.reshape(t // 8, 8, t), axis=0)

    def update(slot, qi, j, diag):
        k0 = pl.multiple_of(j * t, t)
        st = bufs[slot][...]
        m_prev = m_ref[qi]
        if diag:
            key = lax.broadcasted_iota(jnp.int32, (t, t), 0)
            qry = lax.broadcasted_iota(jnp.int32, (t, t), 1)
            st = jnp.where(key <= qry, st, -jnp.inf)
            m_blk = jnp.max(st, axis=0, keepdims=True)
        else:
            m_blk = jnp.max(mbufs[slot][...], axis=0, keepdims=True)
        m_new = jnp.maximum(m_prev, m_blk)
        alpha = jnp.exp2(m_prev - m_new)
        p = jnp.exp2(st - m_new).astype(BF16)
        acc_new = alpha * acc_ref[qi] + _dot(vt_ref[0, :, pl.ds(k0, t)], p)
        if diag:
            q0 = pl.multiple_of(qi * t, t)
            o_ref[pl.ds(q0, t), :] = (acc_new[0:MLA_V] / acc_new[MLA_V:MLA_V + 1]).T.astype(BF16)
        else:
            m_ref[qi] = m_new
            acc_ref[qi] = acc_new

    scores(0, 1, 0, False)

    def full_body(_, carry):
        qi, j = carry
        for u in range(MLA_UNROLL):
            last = j + 1 == qi
            qi_n = jnp.where(last, qi + 1, qi)
            j_n = jnp.where(last, 0, j + 1)
            scores((u + 1) % MLA_SBUFS, jnp.minimum(qi_n, nq - 1), j_n, False)
            update(u % MLA_SBUFS, qi, j, False)
            qi, j = qi_n, j_n
        return qi, j

    lax.fori_loop(0, n_full // MLA_UNROLL, full_body, (jnp.int32(1), jnp.int32(0)))

    scores(0, 0, 0, True)

    def diag_body(i, carry):
        for u in range(MLA_DIAG_UNROLL):
            qi = MLA_DIAG_UNROLL * i + u
            nxt = jnp.minimum(qi + 1, nq - 1)
            scores((u + 1) % MLA_SBUFS, nxt, nxt, True)
            update(u % MLA_SBUFS, qi, qi, True)
        return carry

    lax.fori_loop(0, nq // MLA_DIAG_UNROLL, diag_body, 0)


def _mla_attn(qt, k, vt):
    s = k.shape[0]
    t = MLA_T
    return pl.pallas_call(
        _mla_attn_kernel,
        out_shape=jax.ShapeDtypeStruct((s, MLA_HEADS * MLA_V), BF16),
        grid=(MLA_HEADS,),
        in_specs=[
            pl.BlockSpec((1, MLA_QK, s), lambda h: (h, 0, 0)),
            pl.BlockSpec((s, MLA_QK_PAD), lambda h: (0, h)),
            pl.BlockSpec((1, MLA_V_AUG, s), lambda h: (h, 0, 0)),
        ],
        out_specs=pl.BlockSpec((s, MLA_V), lambda h: (0, h)),
        scratch_shapes=[
            *[pltpu.VMEM((t, t), F32)] * MLA_SBUFS,
            *[pltpu.VMEM((8, t), F32)] * MLA_SBUFS,
            pltpu.VMEM((s // t, 1, t), F32), pltpu.VMEM((s // t, MLA_V_AUG, t), F32)],
        compiler_params=pltpu.CompilerParams(
            dimension_semantics=("arbitrary",), vmem_limit_bytes=MLA_ATTN_VMEM),
        name="mla_attn",
    )(qt, k, vt)


def _ffn_kernel(*refs, with_mix):
    if with_mix:
        o_ref, wo_ref, gm_ref, lnmg_ref, lnmb_ref = refs[:5]
        refs = refs[5:]
    x_ref, sc_ref, sh_ref, g_ref, wg_ref, wu_ref, wd_ref, lng_ref, lnb_ref, y_ref = refs
    x = x_ref[...]
    if with_mix:
        z = DEEPNORM_ALPHA * x + gm_ref[...] * _dot(o_ref[...], wo_ref[...])
        x = _layer_norm(z, lnmg_ref[...], lnmb_ref[...])
    h = (x * (1.0 + sc_ref[...]) + sh_ref[...]).astype(BF16)
    gate = _dot(h, wg_ref[0])
    up = _dot(h, wu_ref[0])
    act = (gate * jax.nn.sigmoid(gate) * up).astype(BF16)
    y = _dot(act, wd_ref[0])
    z = DEEPNORM_ALPHA * x + g_ref[...] * y
    y_ref[...] = _layer_norm(z, lng_ref[...], lnb_ref[...])


def _ffn(x, layer, sc, sh, gate, wg, wu, wd, ln_g, ln_b, mix=None):
    s, d = x.shape
    tm = ROW_BLOCK
    row = lambda i: (i, 0)
    vec = _const_spec((1, d))
    stacked = lambda w: pl.BlockSpec((1,) + w.shape[1:], lambda i: (layer, 0, 0), pipeline_mode=pl.Buffered(1))
    mix_specs, mix_args = [], []
    if mix is not None:
        o, w_o = mix[0], mix[1]
        mix_specs = [pl.BlockSpec((tm, o.shape[1]), row), _const_spec(w_o.shape), vec, vec, vec]
        mix_args = list(mix)
    return pl.pallas_call(
        functools.partial(_ffn_kernel, with_mix=mix is not None),
        out_shape=jax.ShapeDtypeStruct((s, d), F32),
        grid=(s // tm,),
        in_specs=mix_specs + [
            pl.BlockSpec((tm, d), row), vec, vec, vec,
            stacked(wg), stacked(wu), stacked(wd), vec, vec,
        ],
        out_specs=pl.BlockSpec((tm, d), row),
        compiler_params=_params(1),
        name="mix_ffn" if mix is not None else "ffn",
    )(*mix_args, x, sc, sh, gate, wg, wu, wd, ln_g, ln_b)


def _rope_rows(xt, c, s):
    half = SWA_ROT // 2
    x1 = xt[0:half]
    x2 = xt[half:SWA_ROT]
    return jnp.concatenate([x1 * c - x2 * s, x2 * c + x1 * s, xt[SWA_ROT:]], axis=0)


def _swa_proj_kernel(x_ref, sc_ref, sh_ref, wt_ref, b_ref, ct_ref, st_ref, qt_ref, k_ref, vt_ref):
    h = (x_ref[...] * (1.0 + sc_ref[...]) + sh_ref[...]).astype(BF16)
    qkvt = _dot_nt(wt_ref[...], h) + b_ref[...]
    c = ct_ref[...]
    s = st_ref[...]
    hd = SWA_HEAD_DIM
    nq = SWA_HEADS * hd
    nk = SWA_KV_HEADS * LANES
    for i in range(SWA_HEADS):
        qt_ref[i * hd:(i + 1) * hd, :] = (_rope_rows(qkvt[i * hd:(i + 1) * hd], c, s) * SWA_Q_SCALE).astype(BF16)
    for g in range(SWA_KV_HEADS):
        r0 = nq + g * LANES
        kt = jnp.concatenate([_rope_rows(qkvt[r0:r0 + hd], c, s), qkvt[r0 + hd:r0 + LANES]], axis=0)
        k_ref[:, g * LANES:(g + 1) * LANES] = kt.T.astype(BF16)
    vt_ref[...] = qkvt[nq + nk:].astype(BF16)


def _swa_proj(x, sc, sh, w, tabs):
    s, d = x.shape
    tm = ROW_BLOCK
    nq = SWA_HEADS * SWA_HEAD_DIM
    nk = SWA_KV_HEADS * LANES
    nv = SWA_KV_HEADS * SWA_HEAD_DIM
    row = lambda i: (i, 0)
    col = lambda i: (0, i)
    return pl.pallas_call(
        _swa_proj_kernel,
        out_shape=(
            jax.ShapeDtypeStruct((nq, s), BF16),
            jax.ShapeDtypeStruct((s, nk), BF16),
            jax.ShapeDtypeStruct((nv, s), BF16),
        ),
        grid=(s // tm,),
        in_specs=[
            pl.BlockSpec((tm, d), row),
            _const_spec((1, d)), _const_spec((1, d)),
            _const_spec(w["w_qkv_t"].shape), _const_spec(w["b_qkv_t"].shape),
            pl.BlockSpec((SWA_ROT // 2, tm), col), pl.BlockSpec((SWA_ROT // 2, tm), col),
        ],
        out_specs=(pl.BlockSpec((nq, tm), col), pl.BlockSpec((tm, nk), row), pl.BlockSpec((nv, tm), col)),
        compiler_params=_params(1),
        name="swa_proj",
    )(x, sc, sh, w["w_qkv_t"], w["b_qkv_t"], tabs[0], tabs[1])


def _swa_attn_kernel(sinks_ref, qt_ref, kp_ref, kc_ref, vtp_ref, vtc_ref, x_ref, g_ref, w_ref, b_ref,
                     lng_ref, lnb_ref, y_ref, kwin_ref, vtwin_ref, ot_ref, *s_refs):
    i = pl.program_id(0)
    w = SWA_WINDOW
    tq = SWA_TQ
    hd = SWA_HEAD_DIM
    grp = SWA_HEADS // SWA_KV_HEADS
    kwin_ref[0:w, :] = kp_ref[...]
    kwin_ref[w:, :] = kc_ref[...]
    vtwin_ref[:, 0:w] = vtp_ref[...]
    vtwin_ref[:, w:] = vtc_ref[...]

    key = lax.broadcasted_iota(jnp.int32, (2 * w, w), 0)
    qry = lax.broadcasted_iota(jnp.int32, (2 * w, w), 1)
    band = (key > qry) & (key <= qry + w)

    def scores(b, g):
        r0 = b * w
        k_g = kwin_ref[r0:r0 + 2 * w, g * LANES:g * LANES + hd]
        q_g = jnp.concatenate(
            [qt_ref[h * hd:(h + 1) * hd, r0:r0 + w] for h in range(g * grp, (g + 1) * grp)], axis=1)
        return _dot(k_g, q_g)

    tasks = [(b, g) for b in range(tq // w) for g in range(SWA_KV_HEADS)]
    s_refs[0][...] = scores(*tasks[0])
    for n, (b, g) in enumerate(tasks):
        if n + 1 < len(tasks):
            s_refs[(n + 1) % len(s_refs)][...] = scores(*tasks[n + 1])
        st = s_refs[n % len(s_refs)][...]
        r0 = b * w
        heads = range(g * grp, (g + 1) * grp)
        if g == 0:
            first_key = jnp.where(i * (tq // w) + b > 0, 0, w)
            neg = jnp.where(band & (key >= first_key), 0.0, -jnp.inf)
            neg = jnp.concatenate([neg] * grp, axis=1)
        st = st + neg
        sink = jnp.concatenate(
            [jnp.full((1, w), sinks_ref[h] * LOG2E, F32) for h in heads], axis=1)
        m = jnp.maximum(jnp.max(st, axis=0, keepdims=True), sink)
        p = jnp.exp2(st - m)
        denom = jnp.sum(p, axis=0, keepdims=True) + jnp.exp2(sink - m)
        vt_g = vtwin_ref[g * hd:(g + 1) * hd, r0:r0 + 2 * w]
        ot = _dot(vt_g, p.astype(BF16)) / denom
        for c, h in enumerate(heads):
            ot_ref[h * hd:(h + 1) * hd, r0:r0 + w] = ot[:, c * w:(c + 1) * w]

    o = ot_ref[...].T.astype(BF16)
    y = _dot(o, w_ref[...]) + b_ref[...]
    z = DEEPNORM_ALPHA * x_ref[...] + g_ref[...] * y
    y_ref[...] = _layer_norm(z, lng_ref[...], lnb_ref[...])


def _swa_attn(sinks, qt, k, vt, x, gate, w_o, b_o, ln_g, ln_b):
    s, d = x.shape
    tq = SWA_TQ
    w = SWA_WINDOW
    nq = qt.shape[0]
    nk = k.shape[1]
    nv = vt.shape[0]
    r = tq // w
    row = lambda i: (i, 0)
    col = lambda i: (0, i)
    prev_row = lambda i: (jnp.maximum(i * r - 1, 0), 0)
    prev_col = lambda i: (0, jnp.maximum(i * r - 1, 0))
    return pl.pallas_call(
        _swa_attn_kernel,
        out_shape=jax.ShapeDtypeStruct((s, d), F32),
        grid=(s // tq,),
        in_specs=[
            pl.BlockSpec(memory_space=pltpu.SMEM),
            pl.BlockSpec((nq, tq), col),
            pl.BlockSpec((w, nk), prev_row), pl.BlockSpec((tq, nk), row),
            pl.BlockSpec((nv, w), prev_col), pl.BlockSpec((nv, tq), col),
            pl.BlockSpec((tq, d), row),
            _const_spec((1, d)), _const_spec(w_o.shape), _const_spec((1, d)),
            _const_spec((1, d)), _const_spec((1, d)),
        ],
        out_specs=pl.BlockSpec((tq, d), row),
        scratch_shapes=[
            pltpu.VMEM((tq + w, nk), BF16), pltpu.VMEM((nv, tq + w), BF16),
            pltpu.VMEM((nq, tq), F32),
            *[pltpu.VMEM((2 * w, w * SWA_HEADS // SWA_KV_HEADS), F32)] * SWA_SBUFS],
        compiler_params=_params(1),
        name="swa_attn",
    )(sinks, qt, k, k, vt, vt, x, gate, w_o, b_o, ln_g, ln_b)


def _rope_inv(rot_dim):
    return ROPE_THETA ** (-jnp.arange(0, rot_dim, 2, dtype=F32) / rot_dim)


def _rope_tables(positions, rot_dim):
    ang = _rope_inv(rot_dim)[:, None] * positions.astype(F32)[None, :]
    return jnp.cos(ang), jnp.sin(ang)


def _mla_weights(w_in, q_norm, w_q_b, kv_norm, w_kv_b, w_o):
    d = D_MODEL
    w_in_p = jnp.concatenate([w_in, jnp.zeros((d, 64), F32)], axis=1).astype(BF16)
    wkv = w_kv_b.reshape(MLA_KV_RANK, MLA_HEADS, MLA_NOPE + MLA_V)
    wk = wkv[:, :, :MLA_NOPE].reshape(MLA_KV_RANK, MLA_HEADS * MLA_NOPE).astype(BF16)
    wvt = wkv[:, :, MLA_NOPE:].reshape(MLA_KV_RANK, MLA_HEADS * MLA_V).T.astype(BF16)
    return {
        "w_in": w_in_p, "q_norm": q_norm.reshape(1, -1), "wqt": w_q_b.T.astype(BF16),
        "kv_norm": kv_norm.reshape(1, -1), "wk": wk, "wvt": wvt, "w_o": w_o.astype(BF16),
    }


def _swa_weights(w_qkv, b_qkv, w_o):
    nq = SWA_HEADS * SWA_HEAD_DIM
    nkv = SWA_KV_HEADS * SWA_HEAD_DIM

    def pad_heads(t):
        lead = t.shape[:-1]
        t = t.reshape(lead + (SWA_KV_HEADS, SWA_HEAD_DIM))
        t = jnp.concatenate([t, jnp.zeros_like(t)], axis=-1)
        return t.reshape(lead + (SWA_KV_HEADS * LANES,))

    def relayout(t):
        return jnp.concatenate([t[..., :nq], pad_heads(t[..., nq:nq + nkv]), t[..., nq + nkv:]], axis=-1)

    return {
        "w_qkv_t": relayout(w_qkv).T.astype(BF16),
        "b_qkv_t": relayout(b_qkv).reshape(-1, 1),
        "w_o": w_o.astype(BF16),
    }


def kernel(x, c, positions, ada_w, ada_b, ln_mix_g, ln_mix_b, ln_ffn_g, ln_ffn_b, ffn_w_gate, ffn_w_up, ffn_w_down, mla_w_in, mla_q_norm, mla_w_q_b, mla_kv_norm, mla_w_kv_b, mla_w_o, swa_w_qkv, swa_b_qkv, swa_sinks, swa_w_o, swa_b_o):
    b, s, d = x.shape
    assert (b, s, d) == (1, SEQ, D_MODEL)
    xs = x.reshape(s, d)
    pos = positions.reshape(s)
    mod = _modulation(c, ada_w, ada_b)
    mla_tabs = _rope_tables(pos, MLA_ROPE)
    swa_tabs = _rope_tables(pos, SWA_ROT)
    vec = lambda t: t.reshape(1, d)
    wg, wu, wd = ffn_w_gate.astype(BF16), ffn_w_up.astype(BF16), ffn_w_down.astype(BF16)
    for i in range(DEPTH):
        sh_m, sc_m, g_m, sh_f, sc_f, g_f = (mod[i, k] for k in range(6))
        j = i // 2
        mix = None
        if i % 2 == 0:
            w = _mla_weights(mla_w_in[j], mla_q_norm[j], mla_w_q_b[j], mla_kv_norm[j], mla_w_kv_b[j], mla_w_o[j])
            qt, k, vt = _mla_proj(xs, sc_m, sh_m, w, mla_tabs)
            o = _mla_attn(qt, k, vt)
            mix = (o, w["w_o"], g_m, vec(ln_mix_g[i]), vec(ln_mix_b[i]))
        else:
            w = _swa_weights(swa_w_qkv[j], swa_b_qkv[j], swa_w_o[j])
            qt, k, vt = _swa_proj(xs, sc_m, sh_m, w, swa_tabs)
            xs = _swa_attn(swa_sinks[j], qt, k, vt, xs, g_m, w["w_o"], vec(swa_b_o[j]),
                           vec(ln_mix_g[i]), vec(ln_mix_b[i]))
        xs = _ffn(xs, i, sc_f, sh_f, g_f, wg, wu, wd, vec(ln_ffn_g[i]), vec(ln_ffn_b[i]), mix=mix)
    return xs.reshape(b, s, d)
```

```python
import functools

import jax
import jax.numpy as jnp
from jax import lax
from jax.experimental import pallas as pl
from jax.experimental.pallas import tpu as pltpu

D_MODEL = 1024
SEQ = 16384
DEPTH = 4
ROPE_THETA = 500000.0
LN_EPS = 1e-5
RMS_EPS = 1e-6

MLA_HEADS = 8
MLA_NOPE = 128
MLA_ROPE = 64
MLA_V = 128
MLA_Q_RANK = 384
MLA_KV_RANK = 256
MLA_QK = MLA_NOPE + MLA_ROPE
MLA_QK_PAD = 256

SWA_HEADS = 16
SWA_KV_HEADS = 4
SWA_HEAD_DIM = 64
SWA_WINDOW = 128
SWA_ROT = SWA_HEAD_DIM // 4

DEEPNORM_ALPHA = (2 * DEPTH) ** 0.25

LANES = 128
SUBLANES = 8
BF16_SUBLANES = 16
MLA_V_AUG = MLA_V + BF16_SUBLANES
VMEM_LIMIT = 56 * 1024 * 1024

ROW_BLOCK = 512
PROJ_ROW_BLOCK = 1024
MLA_T = 512
MLA_UNROLL = 16
MLA_SBUFS = 2
MLA_DIAG_UNROLL = 8
MLA_ATTN_VMEM = 58 * 1024 * 1024
SWA_TQ = 1024
SWA_SBUFS = 4
LOG2E = 1.4426950408889634
MLA_Q_SCALE = MLA_QK ** -0.5 * LOG2E
SWA_Q_SCALE = SWA_HEAD_DIM ** -0.5 * LOG2E

BF16 = jnp.bfloat16
F32 = jnp.float32


def _dot(a, b):
    return jnp.dot(a, b, preferred_element_type=F32)


def _dot_nt(a, b):
    return lax.dot_general(a, b, (((1,), (1,)), ((), ())), preferred_element_type=F32)


def _layer_norm(z, g, b):
    mu = jnp.mean(z, axis=-1, keepdims=True)
    zc = z - mu
    var = jnp.mean(zc * zc, axis=-1, keepdims=True)
    return zc * lax.rsqrt(var + LN_EPS) * g + b


def _rms_norm(z, g):
    return z * lax.rsqrt(jnp.mean(z * z, axis=-1, keepdims=True) + RMS_EPS) * g


def _const_spec(shape):
    nd = len(shape)
    return pl.BlockSpec(shape, lambda *_: (0,) * nd, pipeline_mode=pl.Buffered(1))


def _params(n_grid):
    return pltpu.CompilerParams(
        dimension_semantics=("arbitrary",) * n_grid, vmem_limit_bytes=VMEM_LIMIT)


def _mod_kernel(c_ref, w_ref, b_ref, o_ref):
    c = c_ref[...]
    cond = (c * jax.nn.sigmoid(c)).astype(BF16)
    cond_rows = jnp.broadcast_to(cond, (SUBLANES, D_MODEL))
    y = _dot(cond_rows, w_ref[0].astype(BF16))
    o_ref[0] = y[0:1] + b_ref[0]


def _modulation(c, ada_w, ada_b):
    d = D_MODEL
    out = pl.pallas_call(
        _mod_kernel,
        out_shape=jax.ShapeDtypeStruct((DEPTH, 1, 6 * d), F32),
        grid=(DEPTH, 6),
        in_specs=[
            pl.BlockSpec((1, d), lambda i, k: (0, 0)),
            pl.BlockSpec((1, d, d), lambda i, k: (i, 0, k)),
            pl.BlockSpec((1, 1, d), lambda i, k: (i, 0, k)),
        ],
        out_specs=pl.BlockSpec((1, 1, d), lambda i, k: (i, 0, k)),
        compiler_params=_params(2),
        name="adaln_mod",
    )(c, ada_w, ada_b.reshape(DEPTH, 1, 6 * d))
    return out.reshape(DEPTH, 6, 1, d)


def _mla_proj_kernel(x_ref, sc_ref, sh_ref, w_in_ref, qn_ref, wqt_ref, kvn_ref, wk_ref, wvt_ref,
                     ckt_ref, skt_ref, qt_ref, k_ref, vt_ref):
    tm = x_ref.shape[0]
    h = (x_ref[...] * (1.0 + sc_ref[...]) + sh_ref[...]).astype(BF16)
    lat = _dot(h, w_in_ref[...])
    q_lat = lat[:, :MLA_Q_RANK]
    kv_lat = lat[:, MLA_Q_RANK:MLA_Q_RANK + MLA_KV_RANK]
    kr = lat[:, MLA_Q_RANK + MLA_KV_RANK:]

    qn = _rms_norm(q_lat, qn_ref[...]).astype(BF16)
    qt = _dot_nt(wqt_ref[...], qn) * MLA_Q_SCALE
    ck = ckt_ref[...]
    sk = skt_ref[...]
    hw = MLA_QK
    half = MLA_ROPE // 2
    for hd in range(MLA_HEADS):
        r0 = hd * hw
        x1 = qt[r0 + MLA_NOPE:r0 + MLA_NOPE + half]
        x2 = qt[r0 + MLA_NOPE + half:r0 + hw]
        qt_ref[hd, 0:MLA_NOPE, :] = qt[r0:r0 + MLA_NOPE].astype(BF16)
        qt_ref[hd, MLA_NOPE:MLA_NOPE + half, :] = (x1 * ck - x2 * sk).astype(BF16)
        qt_ref[hd, MLA_NOPE + half:hw, :] = (x2 * ck + x1 * sk).astype(BF16)

    kvn = _rms_norm(kv_lat, kvn_ref[...]).astype(BF16)
    kn = _dot(kvn, wk_ref[...])
    krt = kr.T
    x1 = krt[0:half]
    x2 = krt[half:MLA_ROPE]
    k_rope = jnp.concatenate(
        [x1 * ck - x2 * sk, x2 * ck + x1 * sk, krt[MLA_ROPE:]], axis=0).T.astype(BF16)
    for hd in range(MLA_HEADS):
        k_ref[:, hd * MLA_QK_PAD:hd * MLA_QK_PAD + MLA_NOPE] = kn[:, hd * MLA_NOPE:(hd + 1) * MLA_NOPE].astype(BF16)
        k_ref[:, hd * MLA_QK_PAD + MLA_NOPE:(hd + 1) * MLA_QK_PAD] = k_rope

    vt = _dot_nt(wvt_ref[...], kvn)
    row = lax.broadcasted_iota(jnp.int32, (MLA_V_AUG - MLA_V, tm), 0)
    ones_row = jnp.where(row == 0, 1.0, 0.0).astype(BF16)
    for hd in range(MLA_HEADS):
        vt_ref[hd, 0:MLA_V, :] = vt[hd * MLA_V:(hd + 1) * MLA_V].astype(BF16)
        vt_ref[hd, MLA_V:, :] = ones_row


def _mla_proj(x, sc, sh, w, tabs):
    s, d = x.shape
    tm = PROJ_ROW_BLOCK
    hk = MLA_HEADS * MLA_QK_PAD
    row = lambda i: (i, 0)
    col = lambda i: (0, i)
    return pl.pallas_call(
        _mla_proj_kernel,
        out_shape=(
            jax.ShapeDtypeStruct((MLA_HEADS, MLA_QK, s), BF16),
            jax.ShapeDtypeStruct((s, hk), BF16),
            jax.ShapeDtypeStruct((MLA_HEADS, MLA_V_AUG, s), BF16),
        ),
        grid=(s // tm,),
        in_specs=[
            pl.BlockSpec((tm, d), row),
            _const_spec((1, d)), _const_spec((1, d)),
            _const_spec(w["w_in"].shape), _const_spec(w["q_norm"].shape), _const_spec(w["wqt"].shape),
            _const_spec(w["kv_norm"].shape), _const_spec(w["wk"].shape), _const_spec(w["wvt"].shape),
            pl.BlockSpec((MLA_ROPE // 2, tm), col), pl.BlockSpec((MLA_ROPE // 2, tm), col),
        ],
        out_specs=(
            pl.BlockSpec((MLA_HEADS, MLA_QK, tm), lambda i: (0, 0, i)),
            pl.BlockSpec((tm, hk), row),
            pl.BlockSpec((MLA_HEADS, MLA_V_AUG, tm), lambda i: (0, 0, i)),
        ),
        compiler_params=_params(1),
        name="mla_proj",
    )(x, sc, sh, w["w_in"], w["q_norm"], w["wqt"], w["kv_norm"], w["wk"], w["wvt"],
      tabs[0], tabs[1])


def _mla_attn_kernel(qt_ref, k_ref, vt_ref, o_ref, *scratch):
    t = MLA_T
    nq = qt_ref.shape[2] // t
    n_full = nq * (nq - 1) // 2
    assert n_full % MLA_UNROLL == 0 and MLA_UNROLL % MLA_SBUFS == 0
    assert nq % MLA_DIAG_UNROLL == 0 and MLA_DIAG_UNROLL % MLA_SBUFS == 0
    bufs = scratch[:MLA_SBUFS]
    mbufs = scratch[MLA_SBUFS:2 * MLA_SBUFS]
    m_ref, acc_ref = scratch[2 * MLA_SBUFS:]
    m_ref[...] = jnp.full(m_ref.shape, -jnp.inf, F32)
    acc_ref[...] = jnp.zeros(acc_ref.shape, F32)

    def scores(slot, qi, j, diag):
        q0 = pl.multiple_of(qi * t, t)
        k0 = pl.multiple_of(j * t, t)
        st = _dot(k_ref[pl.ds(k0, t), 0:MLA_QK], qt_ref[0, :, pl.ds(q0, t)])
        bufs[slot][...] = st
        if not diag:
            mbufs[slot][...] = jnp.max(st.reshape(t // SUBLANES, SUBLANES, t), axis=0)

    def update(slot, qi, j, diag):
        k0 = pl.multiple_of(j * t, t)
        st = bufs[slot][...]
        m_prev = m_ref[qi]
        if diag:
            key = lax.broadcasted_iota(jnp.int32, (t, t), 0)
            qry = lax.broadcasted_iota(jnp.int32, (t, t), 1)
            st = jnp.where(key <= qry, st, -jnp.inf)
            m_blk = jnp.max(st, axis=0, keepdims=True)
        else:
            m_blk = jnp.max(mbufs[slot][...], axis=0, keepdims=True)
        m_new = jnp.maximum(m_prev, m_blk)
        alpha = jnp.exp2(m_prev - m_new)
        p = jnp.exp2(st - m_new).astype(BF16)
        acc_new = alpha * acc_ref[qi] + _dot(vt_ref[0, :, pl.ds(k0, t)], p)
        if diag:
            q0 = pl.multiple_of(qi * t, t)
            o_ref[pl.ds(q0, t), :] = (acc_new[0:MLA_V] / acc_new[MLA_V:MLA_V + 1]).T.astype(BF16)
        else:
            m_ref[qi] = m_new
            acc_ref[qi] = acc_new

    scores(0, 1, 0, False)

    def full_body(_, carry):
        qi, j = carry
        for u in range(MLA_UNROLL):
            last = j + 1 == qi
            qi_n = jnp.where(last, qi + 1, qi)
            j_n = jnp.where(last, 0, j + 1)
            scores((u + 1) % MLA_SBUFS, jnp.minimum(qi_n, nq - 1), j_n, False)
            update(u % MLA_SBUFS, qi, j, False)
            qi, j = qi_n, j_n
        return qi, j

    lax.fori_loop(0, n_full // MLA_UNROLL, full_body, (jnp.int32(1), jnp.int32(0)))

    scores(0, 0, 0, True)

    def diag_body(i, carry):
        for u in range(MLA_DIAG_UNROLL):
            qi = MLA_DIAG_UNROLL * i + u
            nxt = jnp.minimum(qi + 1, nq - 1)
            scores((u + 1) % MLA_SBUFS, nxt, nxt, True)
            update(u % MLA_SBUFS, qi, qi, True)
        return carry

    lax.fori_loop(0, nq // MLA_DIAG_UNROLL, diag_body, 0)


def _mla_attn(qt, k, vt):
    s = k.shape[0]
    t = MLA_T
    return pl.pallas_call(
        _mla_attn_kernel,
        out_shape=jax.ShapeDtypeStruct((s, MLA_HEADS * MLA_V), BF16),
        grid=(MLA_HEADS,),
        in_specs=[
            pl.BlockSpec((1, MLA_QK, s), lambda h: (h, 0, 0)),
            pl.BlockSpec((s, MLA_QK_PAD), lambda h: (0, h)),
            pl.BlockSpec((1, MLA_V_AUG, s), lambda h: (h, 0, 0)),
        ],
        out_specs=pl.BlockSpec((s, MLA_V), lambda h: (0, h)),
        scratch_shapes=[
            *[pltpu.VMEM((t, t), F32)] * MLA_SBUFS,
            *[pltpu.VMEM((SUBLANES, t), F32)] * MLA_SBUFS,
            pltpu.VMEM((s // t, 1, t), F32), pltpu.VMEM((s // t, MLA_V_AUG, t), F32)],
        compiler_params=pltpu.CompilerParams(
            dimension_semantics=("arbitrary",), vmem_limit_bytes=MLA_ATTN_VMEM),
        name="mla_attn",
    )(qt, k, vt)


def _ffn_kernel(*refs, with_mix):
    if with_mix:
        o_ref, wo_ref, gm_ref, lnmg_ref, lnmb_ref = refs[:5]
        refs = refs[5:]
    x_ref, sc_ref, sh_ref, g_ref, wg_ref, wu_ref, wd_ref, lng_ref, lnb_ref, y_ref = refs
    x = x_ref[...]
    if with_mix:
        z = DEEPNORM_ALPHA * x + gm_ref[...] * _dot(o_ref[...], wo_ref[...])
        x = _layer_norm(z, lnmg_ref[...], lnmb_ref[...])
    h = (x * (1.0 + sc_ref[...]) + sh_ref[...]).astype(BF16)
    gate = _dot(h, wg_ref[0])
    up = _dot(h, wu_ref[0])
    act = (gate * jax.nn.sigmoid(gate) * up).astype(BF16)
    y = _dot(act, wd_ref[0])
    z = DEEPNORM_ALPHA * x + g_ref[...] * y
    y_ref[...] = _layer_norm(z, lng_ref[...], lnb_ref[...])


def _ffn(x, layer, sc, sh, gate, wg, wu, wd, ln_g, ln_b, mix=None):
    s, d = x.shape
    tm = ROW_BLOCK
    row = lambda i: (i, 0)
    vec = _const_spec((1, d))
    stacked = lambda w: pl.BlockSpec((1,) + w.shape[1:], lambda i: (layer, 0, 0), pipeline_mode=pl.Buffered(1))
    mix_specs, mix_args = [], []
    if mix is not None:
        o, w_o = mix[0], mix[1]
        mix_specs = [pl.BlockSpec((tm, o.shape[1]), row), _const_spec(w_o.shape), vec, vec, vec]
        mix_args = list(mix)
    return pl.pallas_call(
        functools.partial(_ffn_kernel, with_mix=mix is not None),
        out_shape=jax.ShapeDtypeStruct((s, d), F32),
        grid=(s // tm,),
        in_specs=mix_specs + [
            pl.BlockSpec((tm, d), row), vec, vec, vec,
            stacked(wg), stacked(wu), stacked(wd), vec, vec,
        ],
        out_specs=pl.BlockSpec((tm, d), row),
        compiler_params=_params(1),
        name="mix_ffn" if mix is not None else "ffn",
    )(*mix_args, x, sc, sh, gate, wg, wu, wd, ln_g, ln_b)


def _rope_rows(xt, c, s):
    half = SWA_ROT // 2
    x1 = xt[0:half]
    x2 = xt[half:SWA_ROT]
    return jnp.concatenate([x1 * c - x2 * s, x2 * c + x1 * s, xt[SWA_ROT:]], axis=0)


def _swa_proj_kernel(x_ref, sc_ref, sh_ref, wt_ref, b_ref, ct_ref, st_ref, qt_ref, k_ref, vt_ref):
    h = (x_ref[...] * (1.0 + sc_ref[...]) + sh_ref[...]).astype(BF16)
    qkvt = _dot_nt(wt_ref[...], h) + b_ref[...]
    c = ct_ref[...]
    s = st_ref[...]
    hd = SWA_HEAD_DIM
    nq = SWA_HEADS * hd
    nk = SWA_KV_HEADS * LANES
    for i in range(SWA_HEADS):
        qt_ref[i * hd:(i + 1) * hd, :] = (_rope_rows(qkvt[i * hd:(i + 1) * hd], c, s) * SWA_Q_SCALE).astype(BF16)
    for g in range(SWA_KV_HEADS):
        r0 = nq + g * LANES
        kt = jnp.concatenate([_rope_rows(qkvt[r0:r0 + hd], c, s), qkvt[r0 + hd:r0 + LANES]], axis=0)
        k_ref[:, g * LANES:(g + 1) * LANES] = kt.T.astype(BF16)
    vt_ref[...] = qkvt[nq + nk:].astype(BF16)


def _swa_proj(x, sc, sh, w, tabs):
    s, d = x.shape
    tm = PROJ_ROW_BLOCK
    nq = SWA_HEADS * SWA_HEAD_DIM
    nk = SWA_KV_HEADS * LANES
    nv = SWA_KV_HEADS * SWA_HEAD_DIM
    row = lambda i: (i, 0)
    col = lambda i: (0, i)
    return pl.pallas_call(
        _swa_proj_kernel,
        out_shape=(
            jax.ShapeDtypeStruct((nq, s), BF16),
            jax.ShapeDtypeStruct((s, nk), BF16),
            jax.ShapeDtypeStruct((nv, s), BF16),
        ),
        grid=(s // tm,),
        in_specs=[
            pl.BlockSpec((tm, d), row),
            _const_spec((1, d)), _const_spec((1, d)),
            _const_spec(w["w_qkv_t"].shape), _const_spec(w["b_qkv_t"].shape),
            pl.BlockSpec((SWA_ROT // 2, tm), col), pl.BlockSpec((SWA_ROT // 2, tm), col),
        ],
        out_specs=(pl.BlockSpec((nq, tm), col), pl.BlockSpec((tm, nk), row), pl.BlockSpec((nv, tm), col)),
        compiler_params=_params(1),
        name="swa_proj",
    )(x, sc, sh, w["w_qkv_t"], w["b_qkv_t"], tabs[0], tabs[1])


def _swa_attn_kernel(sinks_ref, qt_ref, kp_ref, kc_ref, vtp_ref, vtc_ref, x_ref, g_ref, w_ref, b_ref,
                     lng_ref, lnb_ref, y_ref, kwin_ref, vtwin_ref, ot_ref, *s_refs):
    i = pl.program_id(0)
    w = SWA_WINDOW
    tq = SWA_TQ
    hd = SWA_HEAD_DIM
    grp = SWA_HEADS // SWA_KV_HEADS
    kwin_ref[0:w, :] = kp_ref[...]
    kwin_ref[w:, :] = kc_ref[...]
    vtwin_ref[:, 0:w] = vtp_ref[...]
    vtwin_ref[:, w:] = vtc_ref[...]

    key = lax.broadcasted_iota(jnp.int32, (2 * w, w), 0)
    qry = lax.broadcasted_iota(jnp.int32, (2 * w, w), 1)
    band = (key > qry) & (key <= qry + w)

    def scores(b, g):
        r0 = b * w
        k_g = kwin_ref[r0:r0 + 2 * w, g * LANES:g * LANES + hd]
        q_g = jnp.concatenate(
            [qt_ref[h * hd:(h + 1) * hd, r0:r0 + w] for h in range(g * grp, (g + 1) * grp)], axis=1)
        return _dot(k_g, q_g)

    tasks = [(b, g) for b in range(tq // w) for g in range(SWA_KV_HEADS)]
    s_refs[0][...] = scores(*tasks[0])
    for n, (b, g) in enumerate(tasks):
        if n + 1 < len(tasks):
            s_refs[(n + 1) % len(s_refs)][...] = scores(*tasks[n + 1])
        st = s_refs[n % len(s_refs)][...]
        r0 = b * w
        heads = range(g * grp, (g + 1) * grp)
        if g == 0:
            first_key = jnp.where(i * (tq // w) + b > 0, 0, w)
            neg = jnp.where(band & (key >= first_key), 0.0, -jnp.inf)
            neg = jnp.concatenate([neg] * grp, axis=1)
        st = st + neg
        sink = jnp.concatenate(
            [jnp.full((1, w), sinks_ref[h] * LOG2E, F32) for h in heads], axis=1)
        m = jnp.maximum(jnp.max(st, axis=0, keepdims=True), sink)
        p = jnp.exp2(st - m)
        denom = jnp.sum(p, axis=0, keepdims=True) + jnp.exp2(sink - m)
        vt_g = vtwin_ref[g * hd:(g + 1) * hd, r0:r0 + 2 * w]
        ot = _dot(vt_g, p.astype(BF16)) / denom
        for c, h in enumerate(heads):
            ot_ref[h * hd:(h + 1) * hd, r0:r0 + w] = ot[:, c * w:(c + 1) * w]

    o = ot_ref[...].T.astype(BF16)
    y = _dot(o, w_ref[...]) + b_ref[...]
    z = DEEPNORM_ALPHA * x_ref[...] + g_ref[...] * y
    y_ref[...] = _layer_norm(z, lng_ref[...], lnb_ref[...])


def _swa_attn(sinks, qt, k, vt, x, gate, w_o, b_o, ln_g, ln_b):
    s, d = x.shape
    tq = SWA_TQ
    w = SWA_WINDOW
    nq = qt.shape[0]
    nk = k.shape[1]
    nv = vt.shape[0]
    r = tq // w
    row = lambda i: (i, 0)
    col = lambda i: (0, i)
    prev_row = lambda i: (jnp.maximum(i * r - 1, 0), 0)
    prev_col = lambda i: (0, jnp.maximum(i * r - 1, 0))
    return pl.pallas_call(
        _swa_attn_kernel,
        out_shape=jax.ShapeDtypeStruct((s, d), F32),
        grid=(s // tq,),
        in_specs=[
            pl.BlockSpec(memory_space=pltpu.SMEM),
            pl.BlockSpec((nq, tq), col),
            pl.BlockSpec((w, nk), prev_row), pl.BlockSpec((tq, nk), row),
            pl.BlockSpec((nv, w), prev_col), pl.BlockSpec((nv, tq), col),
            pl.BlockSpec((tq, d), row),
            _const_spec((1, d)), _const_spec(w_o.shape), _const_spec((1, d)),
            _const_spec((1, d)), _const_spec((1, d)),
        ],
        out_specs=pl.BlockSpec((tq, d), row),
        scratch_shapes=[
            pltpu.VMEM((tq + w, nk), BF16), pltpu.VMEM((nv, tq + w), BF16),
            pltpu.VMEM((nq, tq), F32),
            *[pltpu.VMEM((2 * w, w * SWA_HEADS // SWA_KV_HEADS), F32)] * SWA_SBUFS],
        compiler_params=_params(1),
        name="swa_attn",
    )(sinks, qt, k, k, vt, vt, x, gate, w_o, b_o, ln_g, ln_b)


def _rope_inv(rot_dim):
    return ROPE_THETA ** (-jnp.arange(0, rot_dim, 2, dtype=F32) / rot_dim)


def _rope_tables(positions, rot_dim):
    ang = _rope_inv(rot_dim)[:, None] * positions.astype(F32)[None, :]
    return jnp.cos(ang), jnp.sin(ang)


def _mla_weights(w_in, q_norm, w_q_b, kv_norm, w_kv_b, w_o):
    d = D_MODEL
    w_in_p = jnp.concatenate([w_in, jnp.zeros((d, LANES - MLA_ROPE), F32)], axis=1).astype(BF16)
    wkv = w_kv_b.reshape(MLA_KV_RANK, MLA_HEADS, MLA_NOPE + MLA_V)
    wk = wkv[:, :, :MLA_NOPE].reshape(MLA_KV_RANK, MLA_HEADS * MLA_NOPE).astype(BF16)
    wvt = wkv[:, :, MLA_NOPE:].reshape(MLA_KV_RANK, MLA_HEADS * MLA_V).T.astype(BF16)
    return {
        "w_in": w_in_p, "q_norm": q_norm.reshape(1, -1), "wqt": w_q_b.T.astype(BF16),
        "kv_norm": kv_norm.reshape(1, -1), "wk": wk, "wvt": wvt, "w_o": w_o.astype(BF16),
    }


def _swa_weights(w_qkv, b_qkv, w_o):
    nq = SWA_HEADS * SWA_HEAD_DIM
    nkv = SWA_KV_HEADS * SWA_HEAD_DIM

    def pad_heads(t):
        lead = t.shape[:-1]
        t = t.reshape(lead + (SWA_KV_HEADS, SWA_HEAD_DIM))
        t = jnp.concatenate([t, jnp.zeros_like(t)], axis=-1)
        return t.reshape(lead + (SWA_KV_HEADS * LANES,))

    def relayout(t):
        return jnp.concatenate([t[..., :nq], pad_heads(t[..., nq:nq + nkv]), t[..., nq + nkv:]], axis=-1)

    return {
        "w_qkv_t": relayout(w_qkv).T.astype(BF16),
        "b_qkv_t": relayout(b_qkv).reshape(-1, 1),
        "w_o": w_o.astype(BF16),
    }


def kernel(x, c, positions, ada_w, ada_b, ln_mix_g, ln_mix_b, ln_ffn_g, ln_ffn_b, ffn_w_gate, ffn_w_up, ffn_w_down, mla_w_in, mla_q_norm, mla_w_q_b, mla_kv_norm, mla_w_kv_b, mla_w_o, swa_w_qkv, swa_b_qkv, swa_sinks, swa_w_o, swa_b_o):
    b, s, d = x.shape
    assert (b, s, d) == (1, SEQ, D_MODEL)
    xs = x.reshape(s, d)
    pos = positions.reshape(s)
    mod = _modulation(c, ada_w, ada_b)
    mla_tabs = _rope_tables(pos, MLA_ROPE)
    swa_tabs = _rope_tables(pos, SWA_ROT)
    vec = lambda t: t.reshape(1, d)
    wg, wu, wd = ffn_w_gate.astype(BF16), ffn_w_up.astype(BF16), ffn_w_down.astype(BF16)
    for i in range(DEPTH):
        sh_m, sc_m, g_m, sh_f, sc_f, g_f = (mod[i, k] for k in range(6))
        j = i // 2
        mix = None
        if i % 2 == 0:
            w = _mla_weights(mla_w_in[j], mla_q_norm[j], mla_w_q_b[j], mla_kv_norm[j], mla_w_kv_b[j], mla_w_o[j])
            qt, k, vt = _mla_proj(xs, sc_m, sh_m, w, mla_tabs)
            o = _mla_attn(qt, k, vt)
            mix = (o, w["w_o"], g_m, vec(ln_mix_g[i]), vec(ln_mix_b[i]))
        else:
            w = _swa_weights(swa_w_qkv[j], swa_b_qkv[j], swa_w_o[j])
            qt, k, vt = _swa_proj(xs, sc_m, sh_m, w, swa_tabs)
            xs = _swa_attn(swa_sinks[j], qt, k, vt, xs, g_m, w["w_o"], vec(swa_b_o[j]),
                           vec(ln_mix_g[i]), vec(ln_mix_b[i]))
        xs = _ffn(xs, i, sc_f, sh_f, g_f, wg, wu, wd, vec(ln_ffn_g[i]), vec(ln_ffn_b[i]), mix=mix)
    return xs.reshape(b, s, d)
```

```python
import functools

import jax
import jax.numpy as jnp
from jax import lax
from jax.experimental import pallas as pl
from jax.experimental.pallas import tpu as pltpu

D_MODEL = 1024
SEQ = 16384
DEPTH = 4
ROPE_THETA = 500000.0
LN_EPS = 1e-5
RMS_EPS = 1e-6

MLA_HEADS = 8
MLA_NOPE = 128
MLA_ROPE = 64
MLA_V = 128
MLA_Q_RANK = 384
MLA_KV_RANK = 256
MLA_QK = MLA_NOPE + MLA_ROPE
MLA_QK_PAD = 256

SWA_HEADS = 16
SWA_KV_HEADS = 4
SWA_HEAD_DIM = 64
SWA_WINDOW = 128
SWA_ROT = SWA_HEAD_DIM // 4

DEEPNORM_ALPHA = (2 * DEPTH) ** 0.25

LANES = 128
SUBLANES = 8
BF16_SUBLANES = 16
MLA_V_AUG = MLA_V + BF16_SUBLANES
VMEM_LIMIT = 56 * 1024 * 1024

ROW_BLOCK = 512
PROJ_ROW_BLOCK = 1024
MLA_T = 512
MLA_UNROLL = 62
MLA_SBUFS = 2
MLA_DIAG_UNROLL = 8
MLA_ATTN_VMEM = 58 * 1024 * 1024
SWA_TQ = 1024
SWA_SBUFS = 4
LOG2E = 1.4426950408889634
MLA_Q_SCALE = MLA_QK ** -0.5 * LOG2E
SWA_Q_SCALE = SWA_HEAD_DIM ** -0.5 * LOG2E

BF16 = jnp.bfloat16
F32 = jnp.float32


def _dot(a, b):
    return jnp.dot(a, b, preferred_element_type=F32)


def _dot_nt(a, b):
    return lax.dot_general(a, b, (((1,), (1,)), ((), ())), preferred_element_type=F32)


def _layer_norm(z, g, b):
    mu = jnp.mean(z, axis=-1, keepdims=True)
    zc = z - mu
    var = jnp.mean(zc * zc, axis=-1, keepdims=True)
    return zc * lax.rsqrt(var + LN_EPS) * g + b


def _rms_norm(z, g):
    return z * lax.rsqrt(jnp.mean(z * z, axis=-1, keepdims=True) + RMS_EPS) * g


def _const_spec(shape):
    nd = len(shape)
    return pl.BlockSpec(shape, lambda *_: (0,) * nd, pipeline_mode=pl.Buffered(1))


def _params(n_grid):
    return pltpu.CompilerParams(
        dimension_semantics=("arbitrary",) * n_grid, vmem_limit_bytes=VMEM_LIMIT)


def _mod_kernel(c_ref, w_ref, b_ref, o_ref):
    c = c_ref[...]
    cond = (c * jax.nn.sigmoid(c)).astype(BF16)
    cond_rows = jnp.broadcast_to(cond, (SUBLANES, D_MODEL))
    y = _dot(cond_rows, w_ref[0].astype(BF16))
    o_ref[0] = y[0:1] + b_ref[0]


def _modulation(c, ada_w, ada_b):
    d = D_MODEL
    out = pl.pallas_call(
        _mod_kernel,
        out_shape=jax.ShapeDtypeStruct((DEPTH, 1, 6 * d), F32),
        grid=(DEPTH, 6),
        in_specs=[
            pl.BlockSpec((1, d), lambda i, k: (0, 0)),
            pl.BlockSpec((1, d, d), lambda i, k: (i, 0, k)),
            pl.BlockSpec((1, 1, d), lambda i, k: (i, 0, k)),
        ],
        out_specs=pl.BlockSpec((1, 1, d), lambda i, k: (i, 0, k)),
        compiler_params=_params(2),
        name="adaln_mod",
    )(c, ada_w, ada_b.reshape(DEPTH, 1, 6 * d))
    return out.reshape(DEPTH, 6, 1, d)


def _mla_proj_kernel(x_ref, sc_ref, sh_ref, w_in_ref, qn_ref, wqt_ref, kvn_ref, wk_ref, wvt_ref,
                     ckt_ref, skt_ref, qt_ref, k_ref, vt_ref):
    tm = x_ref.shape[0]
    h = (x_ref[...] * (1.0 + sc_ref[...]) + sh_ref[...]).astype(BF16)
    lat = _dot(h, w_in_ref[...])
    q_lat = lat[:, :MLA_Q_RANK]
    kv_lat = lat[:, MLA_Q_RANK:MLA_Q_RANK + MLA_KV_RANK]
    kr = lat[:, MLA_Q_RANK + MLA_KV_RANK:]

    qn = _rms_norm(q_lat, qn_ref[...]).astype(BF16)
    qt = _dot_nt(wqt_ref[...], qn) * MLA_Q_SCALE
    ck = ckt_ref[...]
    sk = skt_ref[...]
    hw = MLA_QK
    half = MLA_ROPE // 2
    for hd in range(MLA_HEADS):
        r0 = hd * hw
        x1 = qt[r0 + MLA_NOPE:r0 + MLA_NOPE + half]
        x2 = qt[r0 + MLA_NOPE + half:r0 + hw]
        qt_ref[hd, 0:MLA_NOPE, :] = qt[r0:r0 + MLA_NOPE].astype(BF16)
        qt_ref[hd, MLA_NOPE:MLA_NOPE + half, :] = (x1 * ck - x2 * sk).astype(BF16)
        qt_ref[hd, MLA_NOPE + half:hw, :] = (x2 * ck + x1 * sk).astype(BF16)

    kvn = _rms_norm(kv_lat, kvn_ref[...]).astype(BF16)
    kn = _dot(kvn, wk_ref[...])
    krt = kr.T
    x1 = krt[0:half]
    x2 = krt[half:MLA_ROPE]
    k_rope = jnp.concatenate(
        [x1 * ck - x2 * sk, x2 * ck + x1 * sk, krt[MLA_ROPE:]], axis=0).T.astype(BF16)
    for hd in range(MLA_HEADS):
        k_ref[:, hd * MLA_QK_PAD:hd * MLA_QK_PAD + MLA_NOPE] = kn[:, hd * MLA_NOPE:(hd + 1) * MLA_NOPE].astype(BF16)
        k_ref[:, hd * MLA_QK_PAD + MLA_NOPE:(hd + 1) * MLA_QK_PAD] = k_rope

    vt = _dot_nt(wvt_ref[...], kvn)
    row = lax.broadcasted_iota(jnp.int32, (MLA_V_AUG - MLA_V, tm), 0)
    ones_row = jnp.where(row == 0, 1.0, 0.0).astype(BF16)
    for hd in range(MLA_HEADS):
        vt_ref[hd, 0:MLA_V, :] = vt[hd * MLA_V:(hd + 1) * MLA_V].astype(BF16)
        vt_ref[hd, MLA_V:, :] = ones_row


def _mla_proj(x, sc, sh, w, tabs):
    s, d = x.shape
    tm = PROJ_ROW_BLOCK
    hk = MLA_HEADS * MLA_QK_PAD
    row = lambda i: (i, 0)
    col = lambda i: (0, i)
    return pl.pallas_call(
        _mla_proj_kernel,
        out_shape=(
            jax.ShapeDtypeStruct((MLA_HEADS, MLA_QK, s), BF16),
            jax.ShapeDtypeStruct((s, hk), BF16),
            jax.ShapeDtypeStruct((MLA_HEADS, MLA_V_AUG, s), BF16),
        ),
        grid=(s // tm,),
        in_specs=[
            pl.BlockSpec((tm, d), row),
            _const_spec((1, d)), _const_spec((1, d)),
            _const_spec(w["w_in"].shape), _const_spec(w["q_norm"].shape), _const_spec(w["wqt"].shape),
            _const_spec(w["kv_norm"].shape), _const_spec(w["wk"].shape), _const_spec(w["wvt"].shape),
            pl.BlockSpec((MLA_ROPE // 2, tm), col), pl.BlockSpec((MLA_ROPE // 2, tm), col),
        ],
        out_specs=(
            pl.BlockSpec((MLA_HEADS, MLA_QK, tm), lambda i: (0, 0, i)),
            pl.BlockSpec((tm, hk), row),
            pl.BlockSpec((MLA_HEADS, MLA_V_AUG, tm), lambda i: (0, 0, i)),
        ),
        compiler_params=_params(1),
        name="mla_proj",
    )(x, sc, sh, w["w_in"], w["q_norm"], w["wqt"], w["kv_norm"], w["wk"], w["wvt"],
      tabs[0], tabs[1])


def _mla_attn_kernel(qt_ref, k_ref, vt_ref, o_ref, *scratch):
    t = MLA_T
    nq = qt_ref.shape[2] // t
    n_full = nq * (nq - 1) // 2
    assert n_full % MLA_UNROLL == 0 and MLA_UNROLL % MLA_SBUFS == 0
    assert nq % MLA_DIAG_UNROLL == 0 and MLA_DIAG_UNROLL % MLA_SBUFS == 0
    bufs = scratch[:MLA_SBUFS]
    mbufs = scratch[MLA_SBUFS:2 * MLA_SBUFS]
    m_ref, acc_ref = scratch[2 * MLA_SBUFS:]
    m_ref[...] = jnp.full(m_ref.shape, -jnp.inf, F32)
    acc_ref[...] = jnp.zeros(acc_ref.shape, F32)

    def scores(slot, qi, j, diag):
        q0 = pl.multiple_of(qi * t, t)
        k0 = pl.multiple_of(j * t, t)
        st = _dot(k_ref[pl.ds(k0, t), 0:MLA_QK], qt_ref[0, :, pl.ds(q0, t)])
        bufs[slot][...] = st
        if not diag:
            mbufs[slot][...] = jnp.max(st.reshape(t // SUBLANES, SUBLANES, t), axis=0)

    def update(slot, qi, j, diag):
        k0 = pl.multiple_of(j * t, t)
        st = bufs[slot][...]
        m_prev = m_ref[qi]
        if diag:
            key = lax.broadcasted_iota(jnp.int32, (t, t), 0)
            qry = lax.broadcasted_iota(jnp.int32, (t, t), 1)
            st = jnp.where(key <= qry, st, -jnp.inf)
            m_blk = jnp.max(st, axis=0, keepdims=True)
        else:
            m_blk = jnp.max(mbufs[slot][...], axis=0, keepdims=True)
        m_new = jnp.maximum(m_prev, m_blk)
        alpha = jnp.exp2(m_prev - m_new)
        p = jnp.exp2(st - m_new).astype(BF16)
        acc_new = alpha * acc_ref[qi] + _dot(vt_ref[0, :, pl.ds(k0, t)], p)
        if diag:
            q0 = pl.multiple_of(qi * t, t)
            o_ref[pl.ds(q0, t), :] = (acc_new[0:MLA_V] / acc_new[MLA_V:MLA_V + 1]).T.astype(BF16)
        else:
            m_ref[qi] = m_new
            acc_ref[qi] = acc_new

    scores(0, 1, 0, False)

    def full_body(_, carry):
        qi, j = carry
        for u in range(MLA_UNROLL):
            last = j + 1 == qi
            qi_n = jnp.where(last, qi + 1, qi)
            j_n = jnp.where(last, 0, j + 1)
            scores((u + 1) % MLA_SBUFS, jnp.minimum(qi_n, nq - 1), j_n, False)
            update(u % MLA_SBUFS, qi, j, False)
            qi, j = qi_n, j_n
        return qi, j

    lax.fori_loop(0, n_full // MLA_UNROLL, full_body, (jnp.int32(1), jnp.int32(0)))

    scores(0, 0, 0, True)

    def diag_body(i, carry):
        for u in range(MLA_DIAG_UNROLL):
            qi = MLA_DIAG_UNROLL * i + u
            nxt = jnp.minimum(qi + 1, nq - 1)
            scores((u + 1) % MLA_SBUFS, nxt, nxt, True)
            update(u % MLA_SBUFS, qi, qi, True)
        return carry

    lax.fori_loop(0, nq // MLA_DIAG_UNROLL, diag_body, 0)


def _mla_attn(qt, k, vt):
    s = k.shape[0]
    t = MLA_T
    return pl.pallas_call(
        _mla_attn_kernel,
        out_shape=jax.ShapeDtypeStruct((s, MLA_HEADS * MLA_V), BF16),
        grid=(MLA_HEADS,),
        in_specs=[
            pl.BlockSpec((1, MLA_QK, s), lambda h: (h, 0, 0)),
            pl.BlockSpec((s, MLA_QK_PAD), lambda h: (0, h)),
            pl.BlockSpec((1, MLA_V_AUG, s), lambda h: (h, 0, 0)),
        ],
        out_specs=pl.BlockSpec((s, MLA_V), lambda h: (0, h)),
        scratch_shapes=[
            *[pltpu.VMEM((t, t), F32)] * MLA_SBUFS,
            *[pltpu.VMEM((SUBLANES, t), F32)] * MLA_SBUFS,
            pltpu.VMEM((s // t, 1, t), F32), pltpu.VMEM((s // t, MLA_V_AUG, t), F32)],
        compiler_params=pltpu.CompilerParams(
            dimension_semantics=("arbitrary",), vmem_limit_bytes=MLA_ATTN_VMEM),
        name="mla_attn",
    )(qt, k, vt)


def _ffn_kernel(*refs, with_mix):
    if with_mix:
        o_ref, wo_ref, gm_ref, lnmg_ref, lnmb_ref = refs[:5]
        refs = refs[5:]
    x_ref, sc_ref, sh_ref, g_ref, wg_ref, wu_ref, wd_ref, lng_ref, lnb_ref, y_ref = refs
    x = x_ref[...]
    if with_mix:
        z = DEEPNORM_ALPHA * x + gm_ref[...] * _dot(o_ref[...], wo_ref[...])
        x = _layer_norm(z, lnmg_ref[...], lnmb_ref[...])
    h = (x * (1.0 + sc_ref[...]) + sh_ref[...]).astype(BF16)
    gate = _dot(h, wg_ref[0])
    up = _dot(h, wu_ref[0])
    act = (gate * jax.nn.sigmoid(gate) * up).astype(BF16)
    y = _dot(act, wd_ref[0])
    z = DEEPNORM_ALPHA * x + g_ref[...] * y
    y_ref[...] = _layer_norm(z, lng_ref[...], lnb_ref[...])


def _ffn(x, layer, sc, sh, gate, wg, wu, wd, ln_g, ln_b, mix=None):
    s, d = x.shape
    tm = ROW_BLOCK
    row = lambda i: (i, 0)
    vec = _const_spec((1, d))
    stacked = lambda w: pl.BlockSpec((1,) + w.shape[1:], lambda i: (layer, 0, 0), pipeline_mode=pl.Buffered(1))
    mix_specs, mix_args = [], []
    if mix is not None:
        o, w_o = mix[0], mix[1]
        mix_specs = [pl.BlockSpec((tm, o.shape[1]), row), _const_spec(w_o.shape), vec, vec, vec]
        mix_args = list(mix)
    return pl.pallas_call(
        functools.partial(_ffn_kernel, with_mix=mix is not None),
        out_shape=jax.ShapeDtypeStruct((s, d), F32),
        grid=(s // tm,),
        in_specs=mix_specs + [
            pl.BlockSpec((tm, d), row), vec, vec, vec,
            stacked(wg), stacked(wu), stacked(wd), vec, vec,
        ],
        out_specs=pl.BlockSpec((tm, d), row),
        compiler_params=_params(1),
        name="mix_ffn" if mix is not None else "ffn",
    )(*mix_args, x, sc, sh, gate, wg, wu, wd, ln_g, ln_b)


def _rope_rows(xt, c, s):
    half = SWA_ROT // 2
    x1 = xt[0:half]
    x2 = xt[half:SWA_ROT]
    return jnp.concatenate([x1 * c - x2 * s, x2 * c + x1 * s, xt[SWA_ROT:]], axis=0)


def _swa_proj_kernel(x_ref, sc_ref, sh_ref, wt_ref, b_ref, ct_ref, st_ref, qt_ref, k_ref, vt_ref):
    h = (x_ref[...] * (1.0 + sc_ref[...]) + sh_ref[...]).astype(BF16)
    qkvt = _dot_nt(wt_ref[...], h) + b_ref[...]
    c = ct_ref[...]
    s = st_ref[...]
    hd = SWA_HEAD_DIM
    nq = SWA_HEADS * hd
    nk = SWA_KV_HEADS * LANES
    for i in range(SWA_HEADS):
        qt_ref[i * hd:(i + 1) * hd, :] = (_rope_rows(qkvt[i * hd:(i + 1) * hd], c, s) * SWA_Q_SCALE).astype(BF16)
    for g in range(SWA_KV_HEADS):
        r0 = nq + g * LANES
        kt = jnp.concatenate([_rope_rows(qkvt[r0:r0 + hd], c, s), qkvt[r0 + hd:r0 + LANES]], axis=0)
        k_ref[:, g * LANES:(g + 1) * LANES] = kt.T.astype(BF16)
    vt_ref[...] = qkvt[nq + nk:].astype(BF16)


def _swa_proj(x, sc, sh, w, tabs):
    s, d = x.shape
    tm = PROJ_ROW_BLOCK
    nq = SWA_HEADS * SWA_HEAD_DIM
    nk = SWA_KV_HEADS * LANES
    nv = SWA_KV_HEADS * SWA_HEAD_DIM
    row = lambda i: (i, 0)
    col = lambda i: (0, i)
    return pl.pallas_call(
        _swa_proj_kernel,
        out_shape=(
            jax.ShapeDtypeStruct((nq, s), BF16),
            jax.ShapeDtypeStruct((s, nk), BF16),
            jax.ShapeDtypeStruct((nv, s), BF16),
        ),
        grid=(s // tm,),
        in_specs=[
            pl.BlockSpec((tm, d), row),
            _const_spec((1, d)), _const_spec((1, d)),
            _const_spec(w["w_qkv_t"].shape), _const_spec(w["b_qkv_t"].shape),
            pl.BlockSpec((SWA_ROT // 2, tm), col), pl.BlockSpec((SWA_ROT // 2, tm), col),
        ],
        out_specs=(pl.BlockSpec((nq, tm), col), pl.BlockSpec((tm, nk), row), pl.BlockSpec((nv, tm), col)),
        compiler_params=_params(1),
        name="swa_proj",
    )(x, sc, sh, w["w_qkv_t"], w["b_qkv_t"], tabs[0], tabs[1])


def _swa_attn_kernel(sinks_ref, qt_ref, kp_ref, kc_ref, vtp_ref, vtc_ref, x_ref, g_ref, w_ref, b_ref,
                     lng_ref, lnb_ref, y_ref, kwin_ref, vtwin_ref, ot_ref, *s_refs):
    i = pl.program_id(0)
    w = SWA_WINDOW
    tq = SWA_TQ
    hd = SWA_HEAD_DIM
    grp = SWA_HEADS // SWA_KV_HEADS
    kwin_ref[0:w, :] = kp_ref[...]
    kwin_ref[w:, :] = kc_ref[...]
    vtwin_ref[:, 0:w] = vtp_ref[...]
    vtwin_ref[:, w:] = vtc_ref[...]

    key = lax.broadcasted_iota(jnp.int32, (2 * w, w), 0)
    qry = lax.broadcasted_iota(jnp.int32, (2 * w, w), 1)
    band = (key > qry) & (key <= qry + w)

    def scores(b, g):
        r0 = b * w
        k_g = kwin_ref[r0:r0 + 2 * w, g * LANES:g * LANES + hd]
        q_g = jnp.concatenate(
            [qt_ref[h * hd:(h + 1) * hd, r0:r0 + w] for h in range(g * grp, (g + 1) * grp)], axis=1)
        return _dot(k_g, q_g)

    tasks = [(b, g) for b in range(tq // w) for g in range(SWA_KV_HEADS)]
    s_refs[0][...] = scores(*tasks[0])
    for n, (b, g) in enumerate(tasks):
        if n + 1 < len(tasks):
            s_refs[(n + 1) % len(s_refs)][...] = scores(*tasks[n + 1])
        st = s_refs[n % len(s_refs)][...]
        r0 = b * w
        heads = range(g * grp, (g + 1) * grp)
        if g == 0:
            first_key = jnp.where(i * (tq // w) + b > 0, 0, w)
            neg = jnp.where(band & (key >= first_key), 0.0, -jnp.inf)
            neg = jnp.concatenate([neg] * grp, axis=1)
        st = st + neg
        sink = jnp.concatenate(
            [jnp.full((1, w), sinks_ref[h] * LOG2E, F32) for h in heads], axis=1)
        m = jnp.maximum(jnp.max(st, axis=0, keepdims=True), sink)
        p = jnp.exp2(st - m)
        denom = jnp.sum(p, axis=0, keepdims=True) + jnp.exp2(sink - m)
        vt_g = vtwin_ref[g * hd:(g + 1) * hd, r0:r0 + 2 * w]
        ot = _dot(vt_g, p.astype(BF16)) / denom
        for c, h in enumerate(heads):
            ot_ref[h * hd:(h + 1) * hd, r0:r0 + w] = ot[:, c * w:(c + 1) * w]

    o = ot_ref[...].T.astype(BF16)
    y = _dot(o, w_ref[...]) + b_ref[...]
    z = DEEPNORM_ALPHA * x_ref[...] + g_ref[...] * y
    y_ref[...] = _layer_norm(z, lng_ref[...], lnb_ref[...])


def _swa_attn(sinks, qt, k, vt, x, gate, w_o, b_o, ln_g, ln_b):
    s, d = x.shape
    tq = SWA_TQ
    w = SWA_WINDOW
    nq = qt.shape[0]
    nk = k.shape[1]
    nv = vt.shape[0]
    r = tq // w
    row = lambda i: (i, 0)
    col = lambda i: (0, i)
    prev_row = lambda i: (jnp.maximum(i * r - 1, 0), 0)
    prev_col = lambda i: (0, jnp.maximum(i * r - 1, 0))
    return pl.pallas_call(
        _swa_attn_kernel,
        out_shape=jax.ShapeDtypeStruct((s, d), F32),
        grid=(s // tq,),
        in_specs=[
            pl.BlockSpec(memory_space=pltpu.SMEM),
            pl.BlockSpec((nq, tq), col),
            pl.BlockSpec((w, nk), prev_row), pl.BlockSpec((tq, nk), row),
            pl.BlockSpec((nv, w), prev_col), pl.BlockSpec((nv, tq), col),
            pl.BlockSpec((tq, d), row),
            _const_spec((1, d)), _const_spec(w_o.shape), _const_spec((1, d)),
            _const_spec((1, d)), _const_spec((1, d)),
        ],
        out_specs=pl.BlockSpec((tq, d), row),
        scratch_shapes=[
            pltpu.VMEM((tq + w, nk), BF16), pltpu.VMEM((nv, tq + w), BF16),
            pltpu.VMEM((nq, tq), F32),
            *[pltpu.VMEM((2 * w, w * SWA_HEADS // SWA_KV_HEADS), F32)] * SWA_SBUFS],
        compiler_params=_params(1),
        name="swa_attn",
    )(sinks, qt, k, k, vt, vt, x, gate, w_o, b_o, ln_g, ln_b)


def _rope_inv(rot_dim):
    return ROPE_THETA ** (-jnp.arange(0, rot_dim, 2, dtype=F32) / rot_dim)


def _rope_tables(positions, rot_dim):
    ang = _rope_inv(rot_dim)[:, None] * positions.astype(F32)[None, :]
    return jnp.cos(ang), jnp.sin(ang)


def _mla_weights(w_in, q_norm, w_q_b, kv_norm, w_kv_b, w_o):
    d = D_MODEL
    w_in_p = jnp.concatenate([w_in, jnp.zeros((d, LANES - MLA_ROPE), F32)], axis=1).astype(BF16)
    wkv = w_kv_b.reshape(MLA_KV_RANK, MLA_HEADS, MLA_NOPE + MLA_V)
    wk = wkv[:, :, :MLA_NOPE].reshape(MLA_KV_RANK, MLA_HEADS * MLA_NOPE).astype(BF16)
    wvt = wkv[:, :, MLA_NOPE:].reshape(MLA_KV_RANK, MLA_HEADS * MLA_V).T.astype(BF16)
    return {
        "w_in": w_in_p, "q_norm": q_norm.reshape(1, -1), "wqt": w_q_b.T.astype(BF16),
        "kv_norm": kv_norm.reshape(1, -1), "wk": wk, "wvt": wvt, "w_o": w_o.astype(BF16),
    }


def _swa_weights(w_qkv, b_qkv, w_o):
    nq = SWA_HEADS * SWA_HEAD_DIM
    nkv = SWA_KV_HEADS * SWA_HEAD_DIM

    def pad_heads(t):
        lead = t.shape[:-1]
        t = t.reshape(lead + (SWA_KV_HEADS, SWA_HEAD_DIM))
        t = jnp.concatenate([t, jnp.zeros_like(t)], axis=-1)
        return t.reshape(lead + (SWA_KV_HEADS * LANES,))

    def relayout(t):
        return jnp.concatenate([t[..., :nq], pad_heads(t[..., nq:nq + nkv]), t[..., nq + nkv:]], axis=-1)

    return {
        "w_qkv_t": relayout(w_qkv).T.astype(BF16),
        "b_qkv_t": relayout(b_qkv).reshape(-1, 1),
        "w_o": w_o.astype(BF16),
    }


def kernel(x, c, positions, ada_w, ada_b, ln_mix_g, ln_mix_b, ln_ffn_g, ln_ffn_b, ffn_w_gate, ffn_w_up, ffn_w_down, mla_w_in, mla_q_norm, mla_w_q_b, mla_kv_norm, mla_w_kv_b, mla_w_o, swa_w_qkv, swa_b_qkv, swa_sinks, swa_w_o, swa_b_o):
    b, s, d = x.shape
    assert (b, s, d) == (1, SEQ, D_MODEL)
    xs = x.reshape(s, d)
    pos = positions.reshape(s)
    mod = _modulation(c, ada_w, ada_b)
    mla_tabs = _rope_tables(pos, MLA_ROPE)
    swa_tabs = _rope_tables(pos, SWA_ROT)
    vec = lambda t: t.reshape(1, d)
    wg, wu, wd = ffn_w_gate.astype(BF16), ffn_w_up.astype(BF16), ffn_w_down.astype(BF16)
    for i in range(DEPTH):
        sh_m, sc_m, g_m, sh_f, sc_f, g_f = (mod[i, k] for k in range(6))
        j = i // 2
        mix = None
        if i % 2 == 0:
            w = _mla_weights(mla_w_in[j], mla_q_norm[j], mla_w_q_b[j], mla_kv_norm[j], mla_w_kv_b[j], mla_w_o[j])
            qt, k, vt = _mla_proj(xs, sc_m, sh_m, w, mla_tabs)
            o = _mla_attn(qt, k, vt)
            mix = (o, w["w_o"], g_m, vec(ln_mix_g[i]), vec(ln_mix_b[i]))
        else:
            w = _swa_weights(swa_w_qkv[j], swa_b_qkv[j], swa_w_o[j])
            qt, k, vt = _swa_proj(xs, sc_m, sh_m, w, swa_tabs)
            xs = _swa_attn(swa_sinks[j], qt, k, vt, xs, g_m, w["w_o"], vec(swa_b_o[j]),
                           vec(ln_mix_g[i]), vec(ln_mix_b[i]))
        xs = _ffn(xs, i, sc_f, sh_f, g_f, wg, wu, wd, vec(ln_ffn_g[i]), vec(ln_ffn_b[i]), mix=mix)
    return xs.reshape(b, s, d)
```

```python
import functools

import jax
import jax.numpy as jnp
from jax import lax
from jax.experimental import pallas as pl
from jax.experimental.pallas import tpu as pltpu

D_MODEL = 1024
SEQ = 16384
DEPTH = 4
ROPE_THETA = 500000.0
LN_EPS = 1e-5
RMS_EPS = 1e-6

MLA_HEADS = 8
MLA_NOPE = 128
MLA_ROPE = 64
MLA_V = 128
MLA_Q_RANK = 384
MLA_KV_RANK = 256
MLA_QK = MLA_NOPE + MLA_ROPE
MLA_QK_PAD = 256

SWA_HEADS = 16
SWA_KV_HEADS = 4
SWA_HEAD_DIM = 64
SWA_WINDOW = 128
SWA_ROT = SWA_HEAD_DIM // 4

DEEPNORM_ALPHA = (2 * DEPTH) ** 0.25

LANES = 128
SUBLANES = 8
BF16_SUBLANES = 16
MLA_V_AUG = MLA_V + BF16_SUBLANES
VMEM_LIMIT = 56 * 1024 * 1024

ROW_BLOCK = 512
PROJ_ROW_BLOCK = 1024
MLA_T = 512
MLA_UNROLL = 124
MLA_SBUFS = 2
MLA_DIAG_UNROLL = 16
MLA_ATTN_VMEM = 58 * 1024 * 1024
SWA_TQ = 1024
SWA_SBUFS = 4
LOG2E = 1.4426950408889634
MLA_Q_SCALE = MLA_QK ** -0.5 * LOG2E
SWA_Q_SCALE = SWA_HEAD_DIM ** -0.5 * LOG2E

BF16 = jnp.bfloat16
F32 = jnp.float32


def _dot(a, b):
    return jnp.dot(a, b, preferred_element_type=F32)


def _dot_nt(a, b):
    return lax.dot_general(a, b, (((1,), (1,)), ((), ())), preferred_element_type=F32)


def _layer_norm(z, g, b):
    mu = jnp.mean(z, axis=-1, keepdims=True)
    zc = z - mu
    var = jnp.mean(zc * zc, axis=-1, keepdims=True)
    return zc * lax.rsqrt(var + LN_EPS) * g + b


def _rms_norm(z, g):
    return z * lax.rsqrt(jnp.mean(z * z, axis=-1, keepdims=True) + RMS_EPS) * g


def _const_spec(shape):
    nd = len(shape)
    return pl.BlockSpec(shape, lambda *_: (0,) * nd, pipeline_mode=pl.Buffered(1))


def _params(n_grid):
    return pltpu.CompilerParams(
        dimension_semantics=("arbitrary",) * n_grid, vmem_limit_bytes=VMEM_LIMIT)


def _mod_kernel(c_ref, w_ref, b_ref, o_ref):
    c = c_ref[...]
    cond = (c * jax.nn.sigmoid(c)).astype(BF16)
    cond_rows = jnp.broadcast_to(cond, (SUBLANES, D_MODEL))
    y = _dot(cond_rows, w_ref[0].astype(BF16))
    o_ref[0] = y[0:1] + b_ref[0]


def _modulation(c, ada_w, ada_b):
    d = D_MODEL
    out = pl.pallas_call(
        _mod_kernel,
        out_shape=jax.ShapeDtypeStruct((DEPTH, 1, 6 * d), F32),
        grid=(DEPTH, 6),
        in_specs=[
            pl.BlockSpec((1, d), lambda i, k: (0, 0)),
            pl.BlockSpec((1, d, d), lambda i, k: (i, 0, k)),
            pl.BlockSpec((1, 1, d), lambda i, k: (i, 0, k)),
        ],
        out_specs=pl.BlockSpec((1, 1, d), lambda i, k: (i, 0, k)),
        compiler_params=_params(2),
        name="adaln_mod",
    )(c, ada_w, ada_b.reshape(DEPTH, 1, 6 * d))
    return out.reshape(DEPTH, 6, 1, d)


def _mla_proj_kernel(x_ref, sc_ref, sh_ref, w_in_ref, qn_ref, wqt_ref, kvn_ref, wk_ref, wvt_ref,
                     ckt_ref, skt_ref, qt_ref, k_ref, vt_ref):
    tm = x_ref.shape[0]
    h = (x_ref[...] * (1.0 + sc_ref[...]) + sh_ref[...]).astype(BF16)
    lat = _dot(h, w_in_ref[...])
    q_lat = lat[:, :MLA_Q_RANK]
    kv_lat = lat[:, MLA_Q_RANK:MLA_Q_RANK + MLA_KV_RANK]
    kr = lat[:, MLA_Q_RANK + MLA_KV_RANK:]

    qn = _rms_norm(q_lat, qn_ref[...]).astype(BF16)
    qt = _dot_nt(wqt_ref[...], qn) * MLA_Q_SCALE
    ck = ckt_ref[...]
    sk = skt_ref[...]
    hw = MLA_QK
    half = MLA_ROPE // 2
    for hd in range(MLA_HEADS):
        r0 = hd * hw
        x1 = qt[r0 + MLA_NOPE:r0 + MLA_NOPE + half]
        x2 = qt[r0 + MLA_NOPE + half:r0 + hw]
        qt_ref[hd, 0:MLA_NOPE, :] = qt[r0:r0 + MLA_NOPE].astype(BF16)
        qt_ref[hd, MLA_NOPE:MLA_NOPE + half, :] = (x1 * ck - x2 * sk).astype(BF16)
        qt_ref[hd, MLA_NOPE + half:hw, :] = (x2 * ck + x1 * sk).astype(BF16)

    kvn = _rms_norm(kv_lat, kvn_ref[...]).astype(BF16)
    kn = _dot(kvn, wk_ref[...])
    krt = kr.T
    x1 = krt[0:half]
    x2 = krt[half:MLA_ROPE]
    k_rope = jnp.concatenate(
        [x1 * ck - x2 * sk, x2 * ck + x1 * sk, krt[MLA_ROPE:]], axis=0).T.astype(BF16)
    for hd in range(MLA_HEADS):
        k_ref[:, hd * MLA_QK_PAD:hd * MLA_QK_PAD + MLA_NOPE] = kn[:, hd * MLA_NOPE:(hd + 1) * MLA_NOPE].astype(BF16)
        k_ref[:, hd * MLA_QK_PAD + MLA_NOPE:(hd + 1) * MLA_QK_PAD] = k_rope

    vt = _dot_nt(wvt_ref[...], kvn)
    row = lax.broadcasted_iota(jnp.int32, (MLA_V_AUG - MLA_V, tm), 0)
    ones_row = jnp.where(row == 0, 1.0, 0.0).astype(BF16)
    for hd in range(MLA_HEADS):
        vt_ref[hd, 0:MLA_V, :] = vt[hd * MLA_V:(hd + 1) * MLA_V].astype(BF16)
        vt_ref[hd, MLA_V:, :] = ones_row


def _mla_proj(x, sc, sh, w, tabs):
    s, d = x.shape
    tm = PROJ_ROW_BLOCK
    hk = MLA_HEADS * MLA_QK_PAD
    row = lambda i: (i, 0)
    col = lambda i: (0, i)
    return pl.pallas_call(
        _mla_proj_kernel,
        out_shape=(
            jax.ShapeDtypeStruct((MLA_HEADS, MLA_QK, s), BF16),
            jax.ShapeDtypeStruct((s, hk), BF16),
            jax.ShapeDtypeStruct((MLA_HEADS, MLA_V_AUG, s), BF16),
        ),
        grid=(s // tm,),
        in_specs=[
            pl.BlockSpec((tm, d), row),
            _const_spec((1, d)), _const_spec((1, d)),
            _const_spec(w["w_in"].shape), _const_spec(w["q_norm"].shape), _const_spec(w["wqt"].shape),
            _const_spec(w["kv_norm"].shape), _const_spec(w["wk"].shape), _const_spec(w["wvt"].shape),
            pl.BlockSpec((MLA_ROPE // 2, tm), col), pl.BlockSpec((MLA_ROPE // 2, tm), col),
        ],
        out_specs=(
            pl.BlockSpec((MLA_HEADS, MLA_QK, tm), lambda i: (0, 0, i)),
            pl.BlockSpec((tm, hk), row),
            pl.BlockSpec((MLA_HEADS, MLA_V_AUG, tm), lambda i: (0, 0, i)),
        ),
        compiler_params=_params(1),
        name="mla_proj",
    )(x, sc, sh, w["w_in"], w["q_norm"], w["wqt"], w["kv_norm"], w["wk"], w["wvt"],
      tabs[0], tabs[1])


def _mla_attn_kernel(qt_ref, k_ref, vt_ref, o_ref, *scratch):
    t = MLA_T
    nq = qt_ref.shape[2] // t
    n_full = nq * (nq - 1) // 2
    assert n_full % MLA_UNROLL == 0 and MLA_UNROLL % MLA_SBUFS == 0
    assert nq % MLA_DIAG_UNROLL == 0 and MLA_DIAG_UNROLL % MLA_SBUFS == 0
    bufs = scratch[:MLA_SBUFS]
    mbufs = scratch[MLA_SBUFS:2 * MLA_SBUFS]
    m_ref, acc_ref = scratch[2 * MLA_SBUFS:]
    m_ref[...] = jnp.full(m_ref.shape, -jnp.inf, F32)
    acc_ref[...] = jnp.zeros(acc_ref.shape, F32)

    def scores(slot, qi, j, diag):
        q0 = pl.multiple_of(qi * t, t)
        k0 = pl.multiple_of(j * t, t)
        st = _dot(k_ref[pl.ds(k0, t), 0:MLA_QK], qt_ref[0, :, pl.ds(q0, t)])
        bufs[slot][...] = st
        if not diag:
            mbufs[slot][...] = jnp.max(st.reshape(t // SUBLANES, SUBLANES, t), axis=0)

    def update(slot, qi, j, diag):
        k0 = pl.multiple_of(j * t, t)
        st = bufs[slot][...]
        m_prev = m_ref[qi]
        if diag:
            key = lax.broadcasted_iota(jnp.int32, (t, t), 0)
            qry = lax.broadcasted_iota(jnp.int32, (t, t), 1)
            st = jnp.where(key <= qry, st, -jnp.inf)
            m_blk = jnp.max(st, axis=0, keepdims=True)
        else:
            m_blk = jnp.max(mbufs[slot][...], axis=0, keepdims=True)
        m_new = jnp.maximum(m_prev, m_blk)
        alpha = jnp.exp2(m_prev - m_new)
        p = jnp.exp2(st - m_new).astype(BF16)
        acc_new = alpha * acc_ref[qi] + _dot(vt_ref[0, :, pl.ds(k0, t)], p)
        if diag:
            q0 = pl.multiple_of(qi * t, t)
            o_ref[pl.ds(q0, t), :] = (acc_new[0:MLA_V] / acc_new[MLA_V:MLA_V + 1]).T.astype(BF16)
        else:
            m_ref[qi] = m_new
            acc_ref[qi] = acc_new

    scores(0, 1, 0, False)

    def full_body(_, carry):
        qi, j = carry
        for u in range(MLA_UNROLL):
            last = j + 1 == qi
            qi_n = jnp.where(last, qi + 1, qi)
            j_n = jnp.where(last, 0, j + 1)
            scores((u + 1) % MLA_SBUFS, jnp.minimum(qi_n, nq - 1), j_n, False)
            update(u % MLA_SBUFS, qi, j, False)
            qi, j = qi_n, j_n
        return qi, j

    lax.fori_loop(0, n_full // MLA_UNROLL, full_body, (jnp.int32(1), jnp.int32(0)))

    scores(0, 0, 0, True)

    def diag_body(i, carry):
        for u in range(MLA_DIAG_UNROLL):
            qi = MLA_DIAG_UNROLL * i + u
            nxt = jnp.minimum(qi + 1, nq - 1)
            scores((u + 1) % MLA_SBUFS, nxt, nxt, True)
            update(u % MLA_SBUFS, qi, qi, True)
        return carry

    lax.fori_loop(0, nq // MLA_DIAG_UNROLL, diag_body, 0)


def _mla_attn(qt, k, vt):
    s = k.shape[0]
    t = MLA_T
    return pl.pallas_call(
        _mla_attn_kernel,
        out_shape=jax.ShapeDtypeStruct((s, MLA_HEADS * MLA_V), BF16),
        grid=(MLA_HEADS,),
        in_specs=[
            pl.BlockSpec((1, MLA_QK, s), lambda h: (h, 0, 0)),
            pl.BlockSpec((s, MLA_QK_PAD), lambda h: (0, h)),
            pl.BlockSpec((1, MLA_V_AUG, s), lambda h: (h, 0, 0)),
        ],
        out_specs=pl.BlockSpec((s, MLA_V), lambda h: (0, h)),
        scratch_shapes=[
            *[pltpu.VMEM((t, t), F32)] * MLA_SBUFS,
            *[pltpu.VMEM((SUBLANES, t), F32)] * MLA_SBUFS,
            pltpu.VMEM((s // t, 1, t), F32), pltpu.VMEM((s // t, MLA_V_AUG, t), F32)],
        compiler_params=pltpu.CompilerParams(
            dimension_semantics=("arbitrary",), vmem_limit_bytes=MLA_ATTN_VMEM),
        name="mla_attn",
    )(qt, k, vt)


def _ffn_kernel(*refs, with_mix):
    if with_mix:
        o_ref, wo_ref, gm_ref, lnmg_ref, lnmb_ref = refs[:5]
        refs = refs[5:]
    x_ref, sc_ref, sh_ref, g_ref, wg_ref, wu_ref, wd_ref, lng_ref, lnb_ref, y_ref = refs
    x = x_ref[...]
    if with_mix:
        z = DEEPNORM_ALPHA * x + gm_ref[...] * _dot(o_ref[...], wo_ref[...])
        x = _layer_norm(z, lnmg_ref[...], lnmb_ref[...])
    h = (x * (1.0 + sc_ref[...]) + sh_ref[...]).astype(BF16)
    gate = _dot(h, wg_ref[0])
    up = _dot(h, wu_ref[0])
    act = (gate * jax.nn.sigmoid(gate) * up).astype(BF16)
    y = _dot(act, wd_ref[0])
    z = DEEPNORM_ALPHA * x + g_ref[...] * y
    y_ref[...] = _layer_norm(z, lng_ref[...], lnb_ref[...])


def _ffn(x, layer, sc, sh, gate, wg, wu, wd, ln_g, ln_b, mix=None):
    s, d = x.shape
    tm = ROW_BLOCK
    row = lambda i: (i, 0)
    vec = _const_spec((1, d))
    stacked = lambda w: pl.BlockSpec((1,) + w.shape[1:], lambda i: (layer, 0, 0), pipeline_mode=pl.Buffered(1))
    mix_specs, mix_args = [], []
    if mix is not None:
        o, w_o = mix[0], mix[1]
        mix_specs = [pl.BlockSpec((tm, o.shape[1]), row), _const_spec(w_o.shape), vec, vec, vec]
        mix_args = list(mix)
    return pl.pallas_call(
        functools.partial(_ffn_kernel, with_mix=mix is not None),
        out_shape=jax.ShapeDtypeStruct((s, d), F32),
        grid=(s // tm,),
        in_specs=mix_specs + [
            pl.BlockSpec((tm, d), row), vec, vec, vec,
            stacked(wg), stacked(wu), stacked(wd), vec, vec,
        ],
        out_specs=pl.BlockSpec((tm, d), row),
        compiler_params=_params(1),
        name="mix_ffn" if mix is not None else "ffn",
    )(*mix_args, x, sc, sh, gate, wg, wu, wd, ln_g, ln_b)


def _rope_rows(xt, c, s):
    half = SWA_ROT // 2
    x1 = xt[0:half]
    x2 = xt[half:SWA_ROT]
    return jnp.concatenate([x1 * c - x2 * s, x2 * c + x1 * s, xt[SWA_ROT:]], axis=0)


def _swa_proj_kernel(x_ref, sc_ref, sh_ref, wt_ref, b_ref, ct_ref, st_ref, qt_ref, k_ref, vt_ref):
    h = (x_ref[...] * (1.0 + sc_ref[...]) + sh_ref[...]).astype(BF16)
    qkvt = _dot_nt(wt_ref[...], h) + b_ref[...]
    c = ct_ref[...]
    s = st_ref[...]
    hd = SWA_HEAD_DIM
    nq = SWA_HEADS * hd
    nk = SWA_KV_HEADS * LANES
    for i in range(SWA_HEADS):
        qt_ref[i * hd:(i + 1) * hd, :] = (_rope_rows(qkvt[i * hd:(i + 1) * hd], c, s) * SWA_Q_SCALE).astype(BF16)
    for g in range(SWA_KV_HEADS):
        r0 = nq + g * LANES
        kt = jnp.concatenate([_rope_rows(qkvt[r0:r0 + hd], c, s), qkvt[r0 + hd:r0 + LANES]], axis=0)
        k_ref[:, g * LANES:(g + 1) * LANES] = kt.T.astype(BF16)
    vt_ref[...] = qkvt[nq + nk:].astype(BF16)


def _swa_proj(x, sc, sh, w, tabs):
    s, d = x.shape
    tm = PROJ_ROW_BLOCK
    nq = SWA_HEADS * SWA_HEAD_DIM
    nk = SWA_KV_HEADS * LANES
    nv = SWA_KV_HEADS * SWA_HEAD_DIM
    row = lambda i: (i, 0)
    col = lambda i: (0, i)
    return pl.pallas_call(
        _swa_proj_kernel,
        out_shape=(
            jax.ShapeDtypeStruct((nq, s), BF16),
            jax.ShapeDtypeStruct((s, nk), BF16),
            jax.ShapeDtypeStruct((nv, s), BF16),
        ),
        grid=(s // tm,),
        in_specs=[
            pl.BlockSpec((tm, d), row),
            _const_spec((1, d)), _const_spec((1, d)),
            _const_spec(w["w_qkv_t"].shape), _const_spec(w["b_qkv_t"].shape),
            pl.BlockSpec((SWA_ROT // 2, tm), col), pl.BlockSpec((SWA_ROT // 2, tm), col),
        ],
        out_specs=(pl.BlockSpec((nq, tm), col), pl.BlockSpec((tm, nk), row), pl.BlockSpec((nv, tm), col)),
        compiler_params=_params(1),
        name="swa_proj",
    )(x, sc, sh, w["w_qkv_t"], w["b_qkv_t"], tabs[0], tabs[1])


def _swa_attn_kernel(sinks_ref, qt_ref, kp_ref, kc_ref, vtp_ref, vtc_ref, x_ref, g_ref, w_ref, b_ref,
                     lng_ref, lnb_ref, y_ref, kwin_ref, vtwin_ref, ot_ref, *s_refs):
    i = pl.program_id(0)
    w = SWA_WINDOW
    tq = SWA_TQ
    hd = SWA_HEAD_DIM
    grp = SWA_HEADS // SWA_KV_HEADS
    kwin_ref[0:w, :] = kp_ref[...]
    kwin_ref[w:, :] = kc_ref[...]
    vtwin_ref[:, 0:w] = vtp_ref[...]
    vtwin_ref[:, w:] = vtc_ref[...]

    key = lax.broadcasted_iota(jnp.int32, (2 * w, w), 0)
    qry = lax.broadcasted_iota(jnp.int32, (2 * w, w), 1)
    band = (key > qry) & (key <= qry + w)

    def scores(b, g):
        r0 = b * w
        k_g = kwin_ref[r0:r0 + 2 * w, g * LANES:g * LANES + hd]
        q_g = jnp.concatenate(
            [qt_ref[h * hd:(h + 1) * hd, r0:r0 + w] for h in range(g * grp, (g + 1) * grp)], axis=1)
        return _dot(k_g, q_g)

    tasks = [(b, g) for b in range(tq // w) for g in range(SWA_KV_HEADS)]
    s_refs[0][...] = scores(*tasks[0])
    for n, (b, g) in enumerate(tasks):
        if n + 1 < len(tasks):
            s_refs[(n + 1) % len(s_refs)][...] = scores(*tasks[n + 1])
        st = s_refs[n % len(s_refs)][...]
        r0 = b * w
        heads = range(g * grp, (g + 1) * grp)
        if g == 0:
            first_key = jnp.where(i * (tq // w) + b > 0, 0, w)
            neg = jnp.where(band & (key >= first_key), 0.0, -jnp.inf)
            neg = jnp.concatenate([neg] * grp, axis=1)
        st = st + neg
        sink = jnp.concatenate(
            [jnp.full((1, w), sinks_ref[h] * LOG2E, F32) for h in heads], axis=1)
        m = jnp.maximum(jnp.max(st, axis=0, keepdims=True), sink)
        p = jnp.exp2(st - m)
        denom = jnp.sum(p, axis=0, keepdims=True) + jnp.exp2(sink - m)
        vt_g = vtwin_ref[g * hd:(g + 1) * hd, r0:r0 + 2 * w]
        ot = _dot(vt_g, p.astype(BF16)) / denom
        for c, h in enumerate(heads):
            ot_ref[h * hd:(h + 1) * hd, r0:r0 + w] = ot[:, c * w:(c + 1) * w]

    o = ot_ref[...].T.astype(BF16)
    y = _dot(o, w_ref[...]) + b_ref[...]
    z = DEEPNORM_ALPHA * x_ref[...] + g_ref[...] * y
    y_ref[...] = _layer_norm(z, lng_ref[...], lnb_ref[...])


def _swa_attn(sinks, qt, k, vt, x, gate, w_o, b_o, ln_g, ln_b):
    s, d = x.shape
    tq = SWA_TQ
    w = SWA_WINDOW
    nq = qt.shape[0]
    nk = k.shape[1]
    nv = vt.shape[0]
    r = tq // w
    row = lambda i: (i, 0)
    col = lambda i: (0, i)
    prev_row = lambda i: (jnp.maximum(i * r - 1, 0), 0)
    prev_col = lambda i: (0, jnp.maximum(i * r - 1, 0))
    return pl.pallas_call(
        _swa_attn_kernel,
        out_shape=jax.ShapeDtypeStruct((s, d), F32),
        grid=(s // tq,),
        in_specs=[
            pl.BlockSpec(memory_space=pltpu.SMEM),
            pl.BlockSpec((nq, tq), col),
            pl.BlockSpec((w, nk), prev_row), pl.BlockSpec((tq, nk), row),
            pl.BlockSpec((nv, w), prev_col), pl.BlockSpec((nv, tq), col),
            pl.BlockSpec((tq, d), row),
            _const_spec((1, d)), _const_spec(w_o.shape), _const_spec((1, d)),
            _const_spec((1, d)), _const_spec((1, d)),
        ],
        out_specs=pl.BlockSpec((tq, d), row),
        scratch_shapes=[
            pltpu.VMEM((tq + w, nk), BF16), pltpu.VMEM((nv, tq + w), BF16),
            pltpu.VMEM((nq, tq), F32),
            *[pltpu.VMEM((2 * w, w * SWA_HEADS // SWA_KV_HEADS), F32)] * SWA_SBUFS],
        compiler_params=_params(1),
        name="swa_attn",
    )(sinks, qt, k, k, vt, vt, x, gate, w_o, b_o, ln_g, ln_b)


def _rope_inv(rot_dim):
    return ROPE_THETA ** (-jnp.arange(0, rot_dim, 2, dtype=F32) / rot_dim)


def _rope_tables(positions, rot_dim):
    ang = _rope_inv(rot_dim)[:, None] * positions.astype(F32)[None, :]
    return jnp.cos(ang), jnp.sin(ang)


def _mla_weights(w_in, q_norm, w_q_b, kv_norm, w_kv_b, w_o):
    d = D_MODEL
    w_in_p = jnp.concatenate([w_in, jnp.zeros((d, LANES - MLA_ROPE), F32)], axis=1).astype(BF16)
    wkv = w_kv_b.reshape(MLA_KV_RANK, MLA_HEADS, MLA_NOPE + MLA_V)
    wk = wkv[:, :, :MLA_NOPE].reshape(MLA_KV_RANK, MLA_HEADS * MLA_NOPE).astype(BF16)
    wvt = wkv[:, :, MLA_NOPE:].reshape(MLA_KV_RANK, MLA_HEADS * MLA_V).T.astype(BF16)
    return {
        "w_in": w_in_p, "q_norm": q_norm.reshape(1, -1), "wqt": w_q_b.T.astype(BF16),
        "kv_norm": kv_norm.reshape(1, -1), "wk": wk, "wvt": wvt, "w_o": w_o.astype(BF16),
    }


def _swa_weights(w_qkv, b_qkv, w_o):
    nq = SWA_HEADS * SWA_HEAD_DIM
    nkv = SWA_KV_HEADS * SWA_HEAD_DIM

    def pad_heads(t):
        lead = t.shape[:-1]
        t = t.reshape(lead + (SWA_KV_HEADS, SWA_HEAD_DIM))
        t = jnp.concatenate([t, jnp.zeros_like(t)], axis=-1)
        return t.reshape(lead + (SWA_KV_HEADS * LANES,))

    def relayout(t):
        return jnp.concatenate([t[..., :nq], pad_heads(t[..., nq:nq + nkv]), t[..., nq + nkv:]], axis=-1)

    return {
        "w_qkv_t": relayout(w_qkv).T.astype(BF16),
        "b_qkv_t": relayout(b_qkv).reshape(-1, 1),
        "w_o": w_o.astype(BF16),
    }


def kernel(x, c, positions, ada_w, ada_b, ln_mix_g, ln_mix_b, ln_ffn_g, ln_ffn_b, ffn_w_gate, ffn_w_up, ffn_w_down, mla_w_in, mla_q_norm, mla_w_q_b, mla_kv_norm, mla_w_kv_b, mla_w_o, swa_w_qkv, swa_b_qkv, swa_sinks, swa_w_o, swa_b_o):
    b, s, d = x.shape
    assert (b, s, d) == (1, SEQ, D_MODEL)
    xs = x.reshape(s, d)
    pos = positions.reshape(s)
    mod = _modulation(c, ada_w, ada_b)
    mla_tabs = _rope_tables(pos, MLA_ROPE)
    swa_tabs = _rope_tables(pos, SWA_ROT)
    vec = lambda t: t.reshape(1, d)
    wg, wu, wd = ffn_w_gate.astype(BF16), ffn_w_up.astype(BF16), ffn_w_down.astype(BF16)
    for i in range(DEPTH):
        sh_m, sc_m, g_m, sh_f, sc_f, g_f = (mod[i, k] for k in range(6))
        j = i // 2
        mix = None
        if i % 2 == 0:
            w = _mla_weights(mla_w_in[j], mla_q_norm[j], mla_w_q_b[j], mla_kv_norm[j], mla_w_kv_b[j], mla_w_o[j])
            qt, k, vt = _mla_proj(xs, sc_m, sh_m, w, mla_tabs)
            o = _mla_attn(qt, k, vt)
            mix = (o, w["w_o"], g_m, vec(ln_mix_g[i]), vec(ln_mix_b[i]))
        else:
            w = _swa_weights(swa_w_qkv[j], swa_b_qkv[j], swa_w_o[j])
            qt, k, vt = _swa_proj(xs, sc_m, sh_m, w, swa_tabs)
            xs = _swa_attn(swa_sinks[j], qt, k, vt, xs, g_m, w["w_o"], vec(swa_b_o[j]),
                           vec(ln_mix_g[i]), vec(ln_mix_b[i]))
        xs = _ffn(xs, i, sc_f, sh_f, g_f, wg, wu, wd, vec(ln_ffn_g[i]), vec(ln_ffn_b[i]), mix=mix)
    return xs.reshape(b, s, d)
```

```python
import functools

import jax
import jax.numpy as jnp
from jax import lax
from jax.experimental import pallas as pl
from jax.experimental.pallas import tpu as pltpu

D_MODEL = 1024
SEQ = 16384
DEPTH = 4
ROPE_THETA = 500000.0
LN_EPS = 1e-5
RMS_EPS = 1e-6

MLA_HEADS = 8
MLA_NOPE = 128
MLA_ROPE = 64
MLA_V = 128
MLA_Q_RANK = 384
MLA_KV_RANK = 256
MLA_QK = MLA_NOPE + MLA_ROPE
MLA_QK_PAD = 256

SWA_HEADS = 16
SWA_KV_HEADS = 4
SWA_HEAD_DIM = 64
SWA_WINDOW = 128
SWA_ROT = SWA_HEAD_DIM // 4

DEEPNORM_ALPHA = (2 * DEPTH) ** 0.25

LANES = 128
SUBLANES = 8
BF16_SUBLANES = 16
MLA_V_AUG = MLA_V + BF16_SUBLANES
VMEM_LIMIT = 56 * 1024 * 1024

ROW_BLOCK = 512
PROJ_ROW_BLOCK = 1024
MLA_T = 512
MLA_UNROLL = 62
MLA_SBUFS = 2
MLA_DIAG_UNROLL = 8
MLA_ATTN_VMEM = 58 * 1024 * 1024
SWA_TQ = 1024
SWA_SBUFS = 4
LOG2E = 1.4426950408889634
MLA_Q_SCALE = MLA_QK ** -0.5 * LOG2E
SWA_Q_SCALE = SWA_HEAD_DIM ** -0.5 * LOG2E

BF16 = jnp.bfloat16
F32 = jnp.float32


def _dot(a, b):
    return jnp.dot(a, b, preferred_element_type=F32)


def _dot_nt(a, b):
    return lax.dot_general(a, b, (((1,), (1,)), ((), ())), preferred_element_type=F32)


def _layer_norm(z, g, b):
    mu = jnp.mean(z, axis=-1, keepdims=True)
    zc = z - mu
    var = jnp.mean(zc * zc, axis=-1, keepdims=True)
    return zc * lax.rsqrt(var + LN_EPS) * g + b


def _rms_norm(z, g):
    return z * lax.rsqrt(jnp.mean(z * z, axis=-1, keepdims=True) + RMS_EPS) * g


def _const_spec(shape):
    nd = len(shape)
    return pl.BlockSpec(shape, lambda *_: (0,) * nd, pipeline_mode=pl.Buffered(1))


def _params(n_grid):
    return pltpu.CompilerParams(
        dimension_semantics=("arbitrary",) * n_grid, vmem_limit_bytes=VMEM_LIMIT)


def _mod_kernel(c_ref, w_ref, b_ref, o_ref):
    c = c_ref[...]
    cond = (c * jax.nn.sigmoid(c)).astype(BF16)
    cond_rows = jnp.broadcast_to(cond, (SUBLANES, D_MODEL))
    y = _dot(cond_rows, w_ref[0].astype(BF16))
    o_ref[0] = y[0:1] + b_ref[0]


def _modulation(c, ada_w, ada_b):
    d = D_MODEL
    out = pl.pallas_call(
        _mod_kernel,
        out_shape=jax.ShapeDtypeStruct((DEPTH, 1, 6 * d), F32),
        grid=(DEPTH, 6),
        in_specs=[
            pl.BlockSpec((1, d), lambda i, k: (0, 0)),
            pl.BlockSpec((1, d, d), lambda i, k: (i, 0, k)),
            pl.BlockSpec((1, 1, d), lambda i, k: (i, 0, k)),
        ],
        out_specs=pl.BlockSpec((1, 1, d), lambda i, k: (i, 0, k)),
        compiler_params=_params(2),
        name="adaln_mod",
    )(c, ada_w, ada_b.reshape(DEPTH, 1, 6 * d))
    return out.reshape(DEPTH, 6, 1, d)


def _mla_proj_kernel(x_ref, sc_ref, sh_ref, w_in_ref, qn_ref, wqt_ref, kvn_ref, wk_ref, wvt_ref,
                     ckt_ref, skt_ref, qt_ref, k_ref, vt_ref):
    tm = x_ref.shape[0]
    h = (x_ref[...] * (1.0 + sc_ref[...]) + sh_ref[...]).astype(BF16)
    lat = _dot(h, w_in_ref[...])
    q_lat = lat[:, :MLA_Q_RANK]
    kv_lat = lat[:, MLA_Q_RANK:MLA_Q_RANK + MLA_KV_RANK]
    kr = lat[:, MLA_Q_RANK + MLA_KV_RANK:]

    qn = _rms_norm(q_lat, qn_ref[...]).astype(BF16)
    qt = _dot_nt(wqt_ref[...], qn) * MLA_Q_SCALE
    ck = ckt_ref[...]
    sk = skt_ref[...]
    hw = MLA_QK
    half = MLA_ROPE // 2
    for hd in range(MLA_HEADS):
        r0 = hd * hw
        x1 = qt[r0 + MLA_NOPE:r0 + MLA_NOPE + half]
        x2 = qt[r0 + MLA_NOPE + half:r0 + hw]
        qt_ref[hd, 0:MLA_NOPE, :] = qt[r0:r0 + MLA_NOPE].astype(BF16)
        qt_ref[hd, MLA_NOPE:MLA_NOPE + half, :] = (x1 * ck - x2 * sk).astype(BF16)
        qt_ref[hd, MLA_NOPE + half:hw, :] = (x2 * ck + x1 * sk).astype(BF16)

    kvn = _rms_norm(kv_lat, kvn_ref[...]).astype(BF16)
    kn = _dot(kvn, wk_ref[...])
    krt = kr.T
    x1 = krt[0:half]
    x2 = krt[half:MLA_ROPE]
    k_rope = jnp.concatenate(
        [x1 * ck - x2 * sk, x2 * ck + x1 * sk, krt[MLA_ROPE:]], axis=0).T.astype(BF16)
    for hd in range(MLA_HEADS):
        k_ref[:, hd * MLA_QK_PAD:hd * MLA_QK_PAD + MLA_NOPE] = kn[:, hd * MLA_NOPE:(hd + 1) * MLA_NOPE].astype(BF16)
        k_ref[:, hd * MLA_QK_PAD + MLA_NOPE:(hd + 1) * MLA_QK_PAD] = k_rope

    vt = _dot_nt(wvt_ref[...], kvn)
    row = lax.broadcasted_iota(jnp.int32, (MLA_V_AUG - MLA_V, tm), 0)
    ones_row = jnp.where(row == 0, 1.0, 0.0).astype(BF16)
    for hd in range(MLA_HEADS):
        vt_ref[hd, 0:MLA_V, :] = vt[hd * MLA_V:(hd + 1) * MLA_V].astype(BF16)
        vt_ref[hd, MLA_V:, :] = ones_row


def _mla_proj(x, sc, sh, w, tabs):
    s, d = x.shape
    tm = PROJ_ROW_BLOCK
    hk = MLA_HEADS * MLA_QK_PAD
    row = lambda i: (i, 0)
    col = lambda i: (0, i)
    return pl.pallas_call(
        _mla_proj_kernel,
        out_shape=(
            jax.ShapeDtypeStruct((MLA_HEADS, MLA_QK, s), BF16),
            jax.ShapeDtypeStruct((s, hk), BF16),
            jax.ShapeDtypeStruct((MLA_HEADS, MLA_V_AUG, s), BF16),
        ),
        grid=(s // tm,),
        in_specs=[
            pl.BlockSpec((tm, d), row),
            _const_spec((1, d)), _const_spec((1, d)),
            _const_spec(w["w_in"].shape), _const_spec(w["q_norm"].shape), _const_spec(w["wqt"].shape),
            _const_spec(w["kv_norm"].shape), _const_spec(w["wk"].shape), _const_spec(w["wvt"].shape),
            pl.BlockSpec((MLA_ROPE // 2, tm), col), pl.BlockSpec((MLA_ROPE // 2, tm), col),
        ],
        out_specs=(
            pl.BlockSpec((MLA_HEADS, MLA_QK, tm), lambda i: (0, 0, i)),
            pl.BlockSpec((tm, hk), row),
            pl.BlockSpec((MLA_HEADS, MLA_V_AUG, tm), lambda i: (0, 0, i)),
        ),
        compiler_params=_params(1),
        name="mla_proj",
    )(x, sc, sh, w["w_in"], w["q_norm"], w["wqt"], w["kv_norm"], w["wk"], w["wvt"],
      tabs[0], tabs[1])


def _mla_attn_kernel(qt_ref, k_ref, vt_ref, o_ref, *scratch):
    t = MLA_T
    nq = qt_ref.shape[2] // t
    n_full = nq * (nq - 1) // 2
    assert n_full % MLA_UNROLL == 0 and MLA_UNROLL % MLA_SBUFS == 0
    assert nq % MLA_DIAG_UNROLL == 0 and MLA_DIAG_UNROLL % MLA_SBUFS == 0
    bufs = scratch[:MLA_SBUFS]
    mbufs = scratch[MLA_SBUFS:2 * MLA_SBUFS]
    m_ref, acc_ref = scratch[2 * MLA_SBUFS:]
    m_ref[...] = jnp.full(m_ref.shape, -jnp.inf, F32)
    acc_ref[...] = jnp.zeros(acc_ref.shape, F32)

    def scores(slot, qi, j, diag):
        q0 = pl.multiple_of(qi * t, t)
        k0 = pl.multiple_of(j * t, t)
        st = _dot(k_ref[pl.ds(k0, t), 0:MLA_QK], qt_ref[0, :, pl.ds(q0, t)])
        bufs[slot][...] = st
        if not diag:
            mbufs[slot][...] = jnp.max(st.reshape(t // SUBLANES, SUBLANES, t), axis=0)

    def update(slot, qi, j, diag):
        k0 = pl.multiple_of(j * t, t)
        st = bufs[slot][...]
        m_prev = m_ref[qi]
        if diag:
            key = lax.broadcasted_iota(jnp.int32, (t, t), 0)
            qry = lax.broadcasted_iota(jnp.int32, (t, t), 1)
            st = jnp.where(key <= qry, st, -jnp.inf)
            m_blk = jnp.max(st, axis=0, keepdims=True)
        else:
            m_blk = jnp.max(mbufs[slot][...], axis=0, keepdims=True)
        m_new = jnp.maximum(m_prev, m_blk)
        alpha = jnp.exp2(m_prev - m_new)
        p = jnp.exp2(st - m_new).astype(BF16)
        acc_new = alpha * acc_ref[qi] + _dot(vt_ref[0, :, pl.ds(k0, t)], p)
        if diag:
            q0 = pl.multiple_of(qi * t, t)
            o_ref[pl.ds(q0, t), :] = (acc_new[0:MLA_V] / acc_new[MLA_V:MLA_V + 1]).T.astype(BF16)
        else:
            m_ref[qi] = m_new
            acc_ref[qi] = acc_new

    scores(0, 1, 0, False)

    def full_body(_, carry):
        qi, j = carry
        for u in range(MLA_UNROLL):
            last = j + 1 == qi
            qi_n = jnp.where(last, qi + 1, qi)
            j_n = jnp.where(last, 0, j + 1)
            scores((u + 1) % MLA_SBUFS, jnp.minimum(qi_n, nq - 1), j_n, False)
            update(u % MLA_SBUFS, qi, j, False)
            qi, j = qi_n, j_n
        return qi, j

    lax.fori_loop(0, n_full // MLA_UNROLL, full_body, (jnp.int32(1), jnp.int32(0)))

    scores(0, 0, 0, True)

    def diag_body(i, carry):
        for u in range(MLA_DIAG_UNROLL):
            qi = MLA_DIAG_UNROLL * i + u
            nxt = jnp.minimum(qi + 1, nq - 1)
            scores((u + 1) % MLA_SBUFS, nxt, nxt, True)
            update(u % MLA_SBUFS, qi, qi, True)
        return carry

    lax.fori_loop(0, nq // MLA_DIAG_UNROLL, diag_body, 0)


def _mla_attn(qt, k, vt):
    s = k.shape[0]
    t = MLA_T
    return pl.pallas_call(
        _mla_attn_kernel,
        out_shape=jax.ShapeDtypeStruct((s, MLA_HEADS * MLA_V), BF16),
        grid=(MLA_HEADS,),
        in_specs=[
            pl.BlockSpec((1, MLA_QK, s), lambda h: (h, 0, 0)),
            pl.BlockSpec((s, MLA_QK_PAD), lambda h: (0, h)),
            pl.BlockSpec((1, MLA_V_AUG, s), lambda h: (h, 0, 0)),
        ],
        out_specs=pl.BlockSpec((s, MLA_V), lambda h: (0, h)),
        scratch_shapes=[
            *[pltpu.VMEM((t, t), F32)] * MLA_SBUFS,
            *[pltpu.VMEM((SUBLANES, t), F32)] * MLA_SBUFS,
            pltpu.VMEM((s // t, 1, t), F32), pltpu.VMEM((s // t, MLA_V_AUG, t), F32)],
        compiler_params=pltpu.CompilerParams(
            dimension_semantics=("arbitrary",), vmem_limit_bytes=MLA_ATTN_VMEM),
        name="mla_attn",
    )(qt, k, vt)


def _ffn_kernel(*refs, with_mix):
    if with_mix:
        o_ref, wo_ref, gm_ref, lnmg_ref, lnmb_ref = refs[:5]
        refs = refs[5:]
    x_ref, sc_ref, sh_ref, g_ref, wg_ref, wu_ref, wd_ref, lng_ref, lnb_ref, y_ref = refs
    x = x_ref[...]
    if with_mix:
        z = DEEPNORM_ALPHA * x + gm_ref[...] * _dot(o_ref[...], wo_ref[...])
        x = _layer_norm(z, lnmg_ref[...], lnmb_ref[...])
    h = (x * (1.0 + sc_ref[...]) + sh_ref[...]).astype(BF16)
    gate = _dot(h, wg_ref[0])
    up = _dot(h, wu_ref[0])
    act = (gate * jax.nn.sigmoid(gate) * up).astype(BF16)
    y = _dot(act, wd_ref[0])
    z = DEEPNORM_ALPHA * x + g_ref[...] * y
    y_ref[...] = _layer_norm(z, lng_ref[...], lnb_ref[...])


def _ffn(x, layer, sc, sh, gate, wg, wu, wd, ln_g, ln_b, mix=None):
    s, d = x.shape
    tm = ROW_BLOCK
    row = lambda i: (i, 0)
    vec = _const_spec((1, d))
    stacked = lambda w: pl.BlockSpec((1,) + w.shape[1:], lambda i: (layer, 0, 0), pipeline_mode=pl.Buffered(1))
    mix_specs, mix_args = [], []
    if mix is not None:
        o, w_o = mix[0], mix[1]
        mix_specs = [pl.BlockSpec((tm, o.shape[1]), row), _const_spec(w_o.shape), vec, vec, vec]
        mix_args = list(mix)
    return pl.pallas_call(
        functools.partial(_ffn_kernel, with_mix=mix is not None),
        out_shape=jax.ShapeDtypeStruct((s, d), F32),
        grid=(s // tm,),
        in_specs=mix_specs + [
            pl.BlockSpec((tm, d), row), vec, vec, vec,
            stacked(wg), stacked(wu), stacked(wd), vec, vec,
        ],
        out_specs=pl.BlockSpec((tm, d), row),
        compiler_params=_params(1),
        name="mix_ffn" if mix is not None else "ffn",
    )(*mix_args, x, sc, sh, gate, wg, wu, wd, ln_g, ln_b)


def _rope_rows(xt, c, s):
    half = SWA_ROT // 2
    x1 = xt[0:half]
    x2 = xt[half:SWA_ROT]
    return jnp.concatenate([x1 * c - x2 * s, x2 * c + x1 * s, xt[SWA_ROT:]], axis=0)


def _swa_proj_kernel(x_ref, sc_ref, sh_ref, wt_ref, b_ref, ct_ref, st_ref, qt_ref, k_ref, vt_ref):
    h = (x_ref[...] * (1.0 + sc_ref[...]) + sh_ref[...]).astype(BF16)
    qkvt = _dot_nt(wt_ref[...], h) + b_ref[...]
    c = ct_ref[...]
    s = st_ref[...]
    hd = SWA_HEAD_DIM
    nq = SWA_HEADS * hd
    nk = SWA_KV_HEADS * LANES
    for i in range(SWA_HEADS):
        qt_ref[i * hd:(i + 1) * hd, :] = (_rope_rows(qkvt[i * hd:(i + 1) * hd], c, s) * SWA_Q_SCALE).astype(BF16)
    for g in range(SWA_KV_HEADS):
        r0 = nq + g * LANES
        kt = jnp.concatenate([_rope_rows(qkvt[r0:r0 + hd], c, s), qkvt[r0 + hd:r0 + LANES]], axis=0)
        k_ref[:, g * LANES:(g + 1) * LANES] = kt.T.astype(BF16)
    vt_ref[...] = qkvt[nq + nk:].astype(BF16)


def _swa_proj(x, sc, sh, w, tabs):
    s, d = x.shape
    tm = PROJ_ROW_BLOCK
    nq = SWA_HEADS * SWA_HEAD_DIM
    nk = SWA_KV_HEADS * LANES
    nv = SWA_KV_HEADS * SWA_HEAD_DIM
    row = lambda i: (i, 0)
    col = lambda i: (0, i)
    return pl.pallas_call(
        _swa_proj_kernel,
        out_shape=(
            jax.ShapeDtypeStruct((nq, s), BF16),
            jax.ShapeDtypeStruct((s, nk), BF16),
            jax.ShapeDtypeStruct((nv, s), BF16),
        ),
        grid=(s // tm,),
        in_specs=[
            pl.BlockSpec((tm, d), row),
            _const_spec((1, d)), _const_spec((1, d)),
            _const_spec(w["w_qkv_t"].shape), _const_spec(w["b_qkv_t"].shape),
            pl.BlockSpec((SWA_ROT // 2, tm), col), pl.BlockSpec((SWA_ROT // 2, tm), col),
        ],
        out_specs=(pl.BlockSpec((nq, tm), col), pl.BlockSpec((tm, nk), row), pl.BlockSpec((nv, tm), col)),
        compiler_params=_params(1),
        name="swa_proj",
    )(x, sc, sh, w["w_qkv_t"], w["b_qkv_t"], tabs[0], tabs[1])


def _swa_attn_kernel(sinks_ref, qt_ref, kp_ref, kc_ref, vtp_ref, vtc_ref, x_ref, g_ref, w_ref, b_ref,
                     lng_ref, lnb_ref, y_ref, kwin_ref, vtwin_ref, ot_ref, *s_refs):
    i = pl.program_id(0)
    w = SWA_WINDOW
    tq = SWA_TQ
    hd = SWA_HEAD_DIM
    grp = SWA_HEADS // SWA_KV_HEADS
    kwin_ref[0:w, :] = kp_ref[...]
    kwin_ref[w:, :] = kc_ref[...]
    vtwin_ref[:, 0:w] = vtp_ref[...]
    vtwin_ref[:, w:] = vtc_ref[...]

    key = lax.broadcasted_iota(jnp.int32, (2 * w, w), 0)
    qry = lax.broadcasted_iota(jnp.int32, (2 * w, w), 1)
    band = (key > qry) & (key <= qry + w)

    def scores(b, g):
        r0 = b * w
        k_g = kwin_ref[r0:r0 + 2 * w, g * LANES:g * LANES + hd]
        q_g = jnp.concatenate(
            [qt_ref[h * hd:(h + 1) * hd, r0:r0 + w] for h in range(g * grp, (g + 1) * grp)], axis=1)
        return _dot(k_g, q_g)

    tasks = [(b, g) for b in range(tq // w) for g in range(SWA_KV_HEADS)]
    s_refs[0][...] = scores(*tasks[0])
    for n, (b, g) in enumerate(tasks):
        if n + 1 < len(tasks):
            s_refs[(n + 1) % len(s_refs)][...] = scores(*tasks[n + 1])
        st = s_refs[n % len(s_refs)][...]
        r0 = b * w
        heads = range(g * grp, (g + 1) * grp)
        if g == 0:
            first_key = jnp.where(i * (tq // w) + b > 0, 0, w)
            neg = jnp.where(band & (key >= first_key), 0.0, -jnp.inf)
            neg = jnp.concatenate([neg] * grp, axis=1)
        st = st + neg
        sink = jnp.concatenate(
            [jnp.full((1, w), sinks_ref[h] * LOG2E, F32) for h in heads], axis=1)
        m = jnp.maximum(jnp.max(st, axis=0, keepdims=True), sink)
        p = jnp.exp2(st - m)
        denom = jnp.sum(p, axis=0, keepdims=True) + jnp.exp2(sink - m)
        vt_g = vtwin_ref[g * hd:(g + 1) * hd, r0:r0 + 2 * w]
        ot = _dot(vt_g, p.astype(BF16)) / denom
        for c, h in enumerate(heads):
            ot_ref[h * hd:(h + 1) * hd, r0:r0 + w] = ot[:, c * w:(c + 1) * w]

    o = ot_ref[...].T.astype(BF16)
    y = _dot(o, w_ref[...]) + b_ref[...]
    z = DEEPNORM_ALPHA * x_ref[...] + g_ref[...] * y
    y_ref[...] = _layer_norm(z, lng_ref[...], lnb_ref[...])


def _swa_attn(sinks, qt, k, vt, x, gate, w_o, b_o, ln_g, ln_b):
    s, d = x.shape
    tq = SWA_TQ
    w = SWA_WINDOW
    nq = qt.shape[0]
    nk = k.shape[1]
    nv = vt.shape[0]
    r = tq // w
    row = lambda i: (i, 0)
    col = lambda i: (0, i)
    prev_row = lambda i: (jnp.maximum(i * r - 1, 0), 0)
    prev_col = lambda i: (0, jnp.maximum(i * r - 1, 0))
    return pl.pallas_call(
        _swa_attn_kernel,
        out_shape=jax.ShapeDtypeStruct((s, d), F32),
        grid=(s // tq,),
        in_specs=[
            pl.BlockSpec(memory_space=pltpu.SMEM),
            pl.BlockSpec((nq, tq), col),
            pl.BlockSpec((w, nk), prev_row), pl.BlockSpec((tq, nk), row),
            pl.BlockSpec((nv, w), prev_col), pl.BlockSpec((nv, tq), col),
            pl.BlockSpec((tq, d), row),
            _const_spec((1, d)), _const_spec(w_o.shape), _const_spec((1, d)),
            _const_spec((1, d)), _const_spec((1, d)),
        ],
        out_specs=pl.BlockSpec((tq, d), row),
        scratch_shapes=[
            pltpu.VMEM((tq + w, nk), BF16), pltpu.VMEM((nv, tq + w), BF16),
            pltpu.VMEM((nq, tq), F32),
            *[pltpu.VMEM((2 * w, w * SWA_HEADS // SWA_KV_HEADS), F32)] * SWA_SBUFS],
        compiler_params=_params(1),
        name="swa_attn",
    )(sinks, qt, k, k, vt, vt, x, gate, w_o, b_o, ln_g, ln_b)


def _rope_inv(rot_dim):
    return ROPE_THETA ** (-jnp.arange(0, rot_dim, 2, dtype=F32) / rot_dim)


def _rope_tables(positions, rot_dim):
    ang = _rope_inv(rot_dim)[:, None] * positions.astype(F32)[None, :]
    return jnp.cos(ang), jnp.sin(ang)


def _mla_weights(w_in, q_norm, w_q_b, kv_norm, w_kv_b, w_o):
    d = D_MODEL
    w_in_p = jnp.concatenate([w_in, jnp.zeros((d, LANES - MLA_ROPE), F32)], axis=1).astype(BF16)
    wkv = w_kv_b.reshape(MLA_KV_RANK, MLA_HEADS, MLA_NOPE + MLA_V)
    wk = wkv[:, :, :MLA_NOPE].reshape(MLA_KV_RANK, MLA_HEADS * MLA_NOPE).astype(BF16)
    wvt = wkv[:, :, MLA_NOPE:].reshape(MLA_KV_RANK, MLA_HEADS * MLA_V).T.astype(BF16)
    return {
        "w_in": w_in_p, "q_norm": q_norm.reshape(1, -1), "wqt": w_q_b.T.astype(BF16),
        "kv_norm": kv_norm.reshape(1, -1), "wk": wk, "wvt": wvt, "w_o": w_o.astype(BF16),
    }


def _swa_weights(w_qkv, b_qkv, w_o):
    nq = SWA_HEADS * SWA_HEAD_DIM
    nkv = SWA_KV_HEADS * SWA_HEAD_DIM

    def pad_heads(t):
        lead = t.shape[:-1]
        t = t.reshape(lead + (SWA_KV_HEADS, SWA_HEAD_DIM))
        t = jnp.concatenate([t, jnp.zeros_like(t)], axis=-1)
        return t.reshape(lead + (SWA_KV_HEADS * LANES,))

    def relayout(t):
        return jnp.concatenate([t[..., :nq], pad_heads(t[..., nq:nq + nkv]), t[..., nq + nkv:]], axis=-1)

    return {
        "w_qkv_t": relayout(w_qkv).T.astype(BF16),
        "b_qkv_t": relayout(b_qkv).reshape(-1, 1),
        "w_o": w_o.astype(BF16),
    }


def kernel(x, c, positions, ada_w, ada_b, ln_mix_g, ln_mix_b, ln_ffn_g, ln_ffn_b, ffn_w_gate, ffn_w_up, ffn_w_down, mla_w_in, mla_q_norm, mla_w_q_b, mla_kv_norm, mla_w_kv_b, mla_w_o, swa_w_qkv, swa_b_qkv, swa_sinks, swa_w_o, swa_b_o):
    b, s, d = x.shape
    assert (b, s, d) == (1, SEQ, D_MODEL)
    xs = x.reshape(s, d)
    pos = positions.reshape(s)
    mod = _modulation(c, ada_w, ada_b)
    mla_tabs = _rope_tables(pos, MLA_ROPE)
    swa_tabs = _rope_tables(pos, SWA_ROT)
    vec = lambda t: t.reshape(1, d)
    wg, wu, wd = ffn_w_gate.astype(BF16), ffn_w_up.astype(BF16), ffn_w_down.astype(BF16)
    for i in range(DEPTH):
        sh_m, sc_m, g_m, sh_f, sc_f, g_f = (mod[i, k] for k in range(6))
        j = i // 2
        mix = None
        if i % 2 == 0:
            w = _mla_weights(mla_w_in[j], mla_q_norm[j], mla_w_q_b[j], mla_kv_norm[j], mla_w_kv_b[j], mla_w_o[j])
            qt, k, vt = _mla_proj(xs, sc_m, sh_m, w, mla_tabs)
            o = _mla_attn(qt, k, vt)
            mix = (o, w["w_o"], g_m, vec(ln_mix_g[i]), vec(ln_mix_b[i]))
        else:
            w = _swa_weights(swa_w_qkv[j], swa_b_qkv[j], swa_w_o[j])
            qt, k, vt = _swa_proj(xs, sc_m, sh_m, w, swa_tabs)
            xs = _swa_attn(swa_sinks[j], qt, k, vt, xs, g_m, w["w_o"], vec(swa_b_o[j]),
                           vec(ln_mix_g[i]), vec(ln_mix_b[i]))
        xs = _ffn(xs, i, sc_f, sh_f, g_f, wg, wu, wd, vec(ln_ffn_g[i]), vec(ln_ffn_b[i]), mix=mix)
    return xs.reshape(b, s, d)
```

```python
import functools

import jax
import jax.numpy as jnp
from jax import lax
from jax.experimental import pallas as pl
from jax.experimental.pallas import tpu as pltpu

D_MODEL = 1024
SEQ = 16384
DEPTH = 4
ROPE_THETA = 500000.0
LN_EPS = 1e-5
RMS_EPS = 1e-6

MLA_HEADS = 8
MLA_NOPE = 128
MLA_ROPE = 64
MLA_V = 128
MLA_Q_RANK = 384
MLA_KV_RANK = 256
MLA_QK = MLA_NOPE + MLA_ROPE
MLA_QK_PAD = 256

SWA_HEADS = 16
SWA_KV_HEADS = 4
SWA_HEAD_DIM = 64
SWA_WINDOW = 128
SWA_ROT = SWA_HEAD_DIM // 4

DEEPNORM_ALPHA = (2 * DEPTH) ** 0.25

LANES = 128
SUBLANES = 8
BF16_SUBLANES = 16
MLA_V_AUG = MLA_V + BF16_SUBLANES
VMEM_LIMIT = 56 * 1024 * 1024

ROW_BLOCK = 512
FFN_CHUNKS = 2
PROJ_ROW_BLOCK = 1024
MLA_T = 512
MLA_UNROLL = 62
MLA_SBUFS = 2
MLA_DIAG_UNROLL = 8
MLA_ATTN_VMEM = 58 * 1024 * 1024
SWA_TQ = 1024
SWA_SBUFS = 4
LOG2E = 1.4426950408889634
MLA_Q_SCALE = MLA_QK ** -0.5 * LOG2E
SWA_Q_SCALE = SWA_HEAD_DIM ** -0.5 * LOG2E

BF16 = jnp.bfloat16
F32 = jnp.float32


def _dot(a, b):
    return jnp.dot(a, b, preferred_element_type=F32)


def _dot_nt(a, b):
    return lax.dot_general(a, b, (((1,), (1,)), ((), ())), preferred_element_type=F32)


def _layer_norm(z, g, b):
    mu = jnp.mean(z, axis=-1, keepdims=True)
    zc = z - mu
    var = jnp.mean(zc * zc, axis=-1, keepdims=True)
    return zc * lax.rsqrt(var + LN_EPS) * g + b


def _rms_norm(z, g):
    return z * lax.rsqrt(jnp.mean(z * z, axis=-1, keepdims=True) + RMS_EPS) * g


def _const_spec(shape):
    nd = len(shape)
    return pl.BlockSpec(shape, lambda *_: (0,) * nd, pipeline_mode=pl.Buffered(1))


def _params(n_grid):
    return pltpu.CompilerParams(
        dimension_semantics=("arbitrary",) * n_grid, vmem_limit_bytes=VMEM_LIMIT)


def _mod_kernel(c_ref, w_ref, b_ref, o_ref):
    c = c_ref[...]
    cond = (c * jax.nn.sigmoid(c)).astype(BF16)
    cond_rows = jnp.broadcast_to(cond, (SUBLANES, D_MODEL))
    y = _dot(cond_rows, w_ref[0].astype(BF16))
    o_ref[0] = y[0:1] + b_ref[0]


def _modulation(c, ada_w, ada_b):
    d = D_MODEL
    out = pl.pallas_call(
        _mod_kernel,
        out_shape=jax.ShapeDtypeStruct((DEPTH, 1, 6 * d), F32),
        grid=(DEPTH, 6),
        in_specs=[
            pl.BlockSpec((1, d), lambda i, k: (0, 0)),
            pl.BlockSpec((1, d, d), lambda i, k: (i, 0, k)),
            pl.BlockSpec((1, 1, d), lambda i, k: (i, 0, k)),
        ],
        out_specs=pl.BlockSpec((1, 1, d), lambda i, k: (i, 0, k)),
        compiler_params=_params(2),
        name="adaln_mod",
    )(c, ada_w, ada_b.reshape(DEPTH, 1, 6 * d))
    return out.reshape(DEPTH, 6, 1, d)


def _mla_proj_kernel(x_ref, sc_ref, sh_ref, w_in_ref, qn_ref, wqt_ref, kvn_ref, wk_ref, wvt_ref,
                     ckt_ref, skt_ref, qt_ref, k_ref, vt_ref):
    tm = x_ref.shape[0]
    h = (x_ref[...] * (1.0 + sc_ref[...]) + sh_ref[...]).astype(BF16)
    lat = _dot(h, w_in_ref[...])
    q_lat = lat[:, :MLA_Q_RANK]
    kv_lat = lat[:, MLA_Q_RANK:MLA_Q_RANK + MLA_KV_RANK]
    kr = lat[:, MLA_Q_RANK + MLA_KV_RANK:]

    qn = _rms_norm(q_lat, qn_ref[...]).astype(BF16)
    qt = _dot_nt(wqt_ref[...], qn) * MLA_Q_SCALE
    ck = ckt_ref[...]
    sk = skt_ref[...]
    hw = MLA_QK
    half = MLA_ROPE // 2
    for hd in range(MLA_HEADS):
        r0 = hd * hw
        x1 = qt[r0 + MLA_NOPE:r0 + MLA_NOPE + half]
        x2 = qt[r0 + MLA_NOPE + half:r0 + hw]
        qt_ref[hd, 0:MLA_NOPE, :] = qt[r0:r0 + MLA_NOPE].astype(BF16)
        qt_ref[hd, MLA_NOPE:MLA_NOPE + half, :] = (x1 * ck - x2 * sk).astype(BF16)
        qt_ref[hd, MLA_NOPE + half:hw, :] = (x2 * ck + x1 * sk).astype(BF16)

    kvn = _rms_norm(kv_lat, kvn_ref[...]).astype(BF16)
    kn = _dot(kvn, wk_ref[...])
    krt = kr.T
    x1 = krt[0:half]
    x2 = krt[half:MLA_ROPE]
    k_rope = jnp.concatenate(
        [x1 * ck - x2 * sk, x2 * ck + x1 * sk, krt[MLA_ROPE:]], axis=0).T.astype(BF16)
    for hd in range(MLA_HEADS):
        k_ref[:, hd * MLA_QK_PAD:hd * MLA_QK_PAD + MLA_NOPE] = kn[:, hd * MLA_NOPE:(hd + 1) * MLA_NOPE].astype(BF16)
        k_ref[:, hd * MLA_QK_PAD + MLA_NOPE:(hd + 1) * MLA_QK_PAD] = k_rope

    vt = _dot_nt(wvt_ref[...], kvn)
    row = lax.broadcasted_iota(jnp.int32, (MLA_V_AUG - MLA_V, tm), 0)
    ones_row = jnp.where(row == 0, 1.0, 0.0).astype(BF16)
    for hd in range(MLA_HEADS):
        vt_ref[hd, 0:MLA_V, :] = vt[hd * MLA_V:(hd + 1) * MLA_V].astype(BF16)
        vt_ref[hd, MLA_V:, :] = ones_row


def _mla_proj(x, sc, sh, w, tabs):
    s, d = x.shape
    tm = PROJ_ROW_BLOCK
    hk = MLA_HEADS * MLA_QK_PAD
    row = lambda i: (i, 0)
    col = lambda i: (0, i)
    return pl.pallas_call(
        _mla_proj_kernel,
        out_shape=(
            jax.ShapeDtypeStruct((MLA_HEADS, MLA_QK, s), BF16),
            jax.ShapeDtypeStruct((s, hk), BF16),
            jax.ShapeDtypeStruct((MLA_HEADS, MLA_V_AUG, s), BF16),
        ),
        grid=(s // tm,),
        in_specs=[
            pl.BlockSpec((tm, d), row),
            _const_spec((1, d)), _const_spec((1, d)),
            _const_spec(w["w_in"].shape), _const_spec(w["q_norm"].shape), _const_spec(w["wqt"].shape),
            _const_spec(w["kv_norm"].shape), _const_spec(w["wk"].shape), _const_spec(w["wvt"].shape),
            pl.BlockSpec((MLA_ROPE // 2, tm), col), pl.BlockSpec((MLA_ROPE // 2, tm), col),
        ],
        out_specs=(
            pl.BlockSpec((MLA_HEADS, MLA_QK, tm), lambda i: (0, 0, i)),
            pl.BlockSpec((tm, hk), row),
            pl.BlockSpec((MLA_HEADS, MLA_V_AUG, tm), lambda i: (0, 0, i)),
        ),
        compiler_params=_params(1),
        name="mla_proj",
    )(x, sc, sh, w["w_in"], w["q_norm"], w["wqt"], w["kv_norm"], w["wk"], w["wvt"],
      tabs[0], tabs[1])


def _mla_attn_kernel(qt_ref, k_ref, vt_ref, o_ref, *scratch):
    t = MLA_T
    nq = qt_ref.shape[2] // t
    n_full = nq * (nq - 1) // 2
    assert n_full % MLA_UNROLL == 0 and MLA_UNROLL % MLA_SBUFS == 0
    assert nq % MLA_DIAG_UNROLL == 0 and MLA_DIAG_UNROLL % MLA_SBUFS == 0
    bufs = scratch[:MLA_SBUFS]
    mbufs = scratch[MLA_SBUFS:2 * MLA_SBUFS]
    m_ref, acc_ref = scratch[2 * MLA_SBUFS:]
    m_ref[...] = jnp.full(m_ref.shape, -jnp.inf, F32)
    acc_ref[...] = jnp.zeros(acc_ref.shape, F32)

    def scores(slot, qi, j, diag):
        q0 = pl.multiple_of(qi * t, t)
        k0 = pl.multiple_of(j * t, t)
        st = _dot(k_ref[pl.ds(k0, t), 0:MLA_QK], qt_ref[0, :, pl.ds(q0, t)])
        bufs[slot][...] = st
        if not diag:
            mbufs[slot][...] = jnp.max(st.reshape(t // SUBLANES, SUBLANES, t), axis=0)

    def update(slot, qi, j, diag):
        k0 = pl.multiple_of(j * t, t)
        st = bufs[slot][...]
        m_prev = m_ref[qi]
        if diag:
            key = lax.broadcasted_iota(jnp.int32, (t, t), 0)
            qry = lax.broadcasted_iota(jnp.int32, (t, t), 1)
            st = jnp.where(key <= qry, st, -jnp.inf)
            m_blk = jnp.max(st, axis=0, keepdims=True)
        else:
            m_blk = jnp.max(mbufs[slot][...], axis=0, keepdims=True)
        m_new = jnp.maximum(m_prev, m_blk)
        alpha = jnp.exp2(m_prev - m_new)
        p = jnp.exp2(st - m_new).astype(BF16)
        acc_new = alpha * acc_ref[qi] + _dot(vt_ref[0, :, pl.ds(k0, t)], p)
        if diag:
            q0 = pl.multiple_of(qi * t, t)
            o_ref[pl.ds(q0, t), :] = (acc_new[0:MLA_V] / acc_new[MLA_V:MLA_V + 1]).T.astype(BF16)
        else:
            m_ref[qi] = m_new
            acc_ref[qi] = acc_new

    scores(0, 1, 0, False)

    def full_body(_, carry):
        qi, j = carry
        for u in range(MLA_UNROLL):
            last = j + 1 == qi
            qi_n = jnp.where(last, qi + 1, qi)
            j_n = jnp.where(last, 0, j + 1)
            scores((u + 1) % MLA_SBUFS, jnp.minimum(qi_n, nq - 1), j_n, False)
            update(u % MLA_SBUFS, qi, j, False)
            qi, j = qi_n, j_n
        return qi, j

    lax.fori_loop(0, n_full // MLA_UNROLL, full_body, (jnp.int32(1), jnp.int32(0)))

    scores(0, 0, 0, True)

    def diag_body(i, carry):
        for u in range(MLA_DIAG_UNROLL):
            qi = MLA_DIAG_UNROLL * i + u
            nxt = jnp.minimum(qi + 1, nq - 1)
            scores((u + 1) % MLA_SBUFS, nxt, nxt, True)
            update(u % MLA_SBUFS, qi, qi, True)
        return carry

    lax.fori_loop(0, nq // MLA_DIAG_UNROLL, diag_body, 0)


def _mla_attn(qt, k, vt):
    s = k.shape[0]
    t = MLA_T
    return pl.pallas_call(
        _mla_attn_kernel,
        out_shape=jax.ShapeDtypeStruct((s, MLA_HEADS * MLA_V), BF16),
        grid=(MLA_HEADS,),
        in_specs=[
            pl.BlockSpec((1, MLA_QK, s), lambda h: (h, 0, 0)),
            pl.BlockSpec((s, MLA_QK_PAD), lambda h: (0, h)),
            pl.BlockSpec((1, MLA_V_AUG, s), lambda h: (h, 0, 0)),
        ],
        out_specs=pl.BlockSpec((s, MLA_V), lambda h: (0, h)),
        scratch_shapes=[
            *[pltpu.VMEM((t, t), F32)] * MLA_SBUFS,
            *[pltpu.VMEM((SUBLANES, t), F32)] * MLA_SBUFS,
            pltpu.VMEM((s // t, 1, t), F32), pltpu.VMEM((s // t, MLA_V_AUG, t), F32)],
        compiler_params=pltpu.CompilerParams(
            dimension_semantics=("arbitrary",), vmem_limit_bytes=MLA_ATTN_VMEM),
        name="mla_attn",
    )(qt, k, vt)


def _ffn_kernel(*refs, with_mix):
    if with_mix:
        o_ref, wo_ref, gm_ref, lnmg_ref, lnmb_ref = refs[:5]
        refs = refs[5:]
    x_ref, sc_ref, sh_ref, g_ref, wg_ref, wu_ref, wd_ref, lng_ref, lnb_ref, y_ref = refs
    x = x_ref[...]
    if with_mix:
        z = DEEPNORM_ALPHA * x + gm_ref[...] * _dot(o_ref[...], wo_ref[...])
        x = _layer_norm(z, lnmg_ref[...], lnmb_ref[...])
    h = (x * (1.0 + sc_ref[...]) + sh_ref[...]).astype(BF16)
    fc = wg_ref.shape[2] // FFN_CHUNKS
    y = None
    for c in range(FFN_CHUNKS):
        gate = _dot(h, wg_ref[0, :, c * fc:(c + 1) * fc])
        up = _dot(h, wu_ref[0, :, c * fc:(c + 1) * fc])
        act = (gate * jax.nn.sigmoid(gate) * up).astype(BF16)
        part = _dot(act, wd_ref[0, c * fc:(c + 1) * fc, :])
        y = part if y is None else y + part
    z = DEEPNORM_ALPHA * x + g_ref[...] * y
    y_ref[...] = _layer_norm(z, lng_ref[...], lnb_ref[...])


def _ffn(x, layer, sc, sh, gate, wg, wu, wd, ln_g, ln_b, mix=None):
    s, d = x.shape
    tm = ROW_BLOCK
    row = lambda i: (i, 0)
    vec = _const_spec((1, d))
    stacked = lambda w: pl.BlockSpec((1,) + w.shape[1:], lambda i: (layer, 0, 0), pipeline_mode=pl.Buffered(1))
    mix_specs, mix_args = [], []
    if mix is not None:
        o, w_o = mix[0], mix[1]
        mix_specs = [pl.BlockSpec((tm, o.shape[1]), row), _const_spec(w_o.shape), vec, vec, vec]
        mix_args = list(mix)
    return pl.pallas_call(
        functools.partial(_ffn_kernel, with_mix=mix is not None),
        out_shape=jax.ShapeDtypeStruct((s, d), F32),
        grid=(s // tm,),
        in_specs=mix_specs + [
            pl.BlockSpec((tm, d), row), vec, vec, vec,
            stacked(wg), stacked(wu), stacked(wd), vec, vec,
        ],
        out_specs=pl.BlockSpec((tm, d), row),
        compiler_params=_params(1),
        name="mix_ffn" if mix is not None else "ffn",
    )(*mix_args, x, sc, sh, gate, wg, wu, wd, ln_g, ln_b)


def _rope_rows(xt, c, s):
    half = SWA_ROT // 2
    x1 = xt[0:half]
    x2 = xt[half:SWA_ROT]
    return jnp.concatenate([x1 * c - x2 * s, x2 * c + x1 * s, xt[SWA_ROT:]], axis=0)


def _swa_proj_kernel(x_ref, sc_ref, sh_ref, wt_ref, b_ref, ct_ref, st_ref, qt_ref, k_ref, vt_ref):
    h = (x_ref[...] * (1.0 + sc_ref[...]) + sh_ref[...]).astype(BF16)
    qkvt = _dot_nt(wt_ref[...], h) + b_ref[...]
    c = ct_ref[...]
    s = st_ref[...]
    hd = SWA_HEAD_DIM
    nq = SWA_HEADS * hd
    nk = SWA_KV_HEADS * LANES
    for i in range(SWA_HEADS):
        qt_ref[i * hd:(i + 1) * hd, :] = (_rope_rows(qkvt[i * hd:(i + 1) * hd], c, s) * SWA_Q_SCALE).astype(BF16)
    for g in range(SWA_KV_HEADS):
        r0 = nq + g * LANES
        kt = jnp.concatenate([_rope_rows(qkvt[r0:r0 + hd], c, s), qkvt[r0 + hd:r0 + LANES]], axis=0)
        k_ref[:, g * LANES:(g + 1) * LANES] = kt.T.astype(BF16)
    vt_ref[...] = qkvt[nq + nk:].astype(BF16)


def _swa_proj(x, sc, sh, w, tabs):
    s, d = x.shape
    tm = PROJ_ROW_BLOCK
    nq = SWA_HEADS * SWA_HEAD_DIM
    nk = SWA_KV_HEADS * LANES
    nv = SWA_KV_HEADS * SWA_HEAD_DIM
    row = lambda i: (i, 0)
    col = lambda i: (0, i)
    return pl.pallas_call(
        _swa_proj_kernel,
        out_shape=(
            jax.ShapeDtypeStruct((nq, s), BF16),
            jax.ShapeDtypeStruct((s, nk), BF16),
            jax.ShapeDtypeStruct((nv, s), BF16),
        ),
        grid=(s // tm,),
        in_specs=[
            pl.BlockSpec((tm, d), row),
            _const_spec((1, d)), _const_spec((1, d)),
            _const_spec(w["w_qkv_t"].shape), _const_spec(w["b_qkv_t"].shape),
            pl.BlockSpec((SWA_ROT // 2, tm), col), pl.BlockSpec((SWA_ROT // 2, tm), col),
        ],
        out_specs=(pl.BlockSpec((nq, tm), col), pl.BlockSpec((tm, nk), row), pl.BlockSpec((nv, tm), col)),
        compiler_params=_params(1),
        name="swa_proj",
    )(x, sc, sh, w["w_qkv_t"], w["b_qkv_t"], tabs[0], tabs[1])


def _swa_attn_kernel(sinks_ref, qt_ref, kp_ref, kc_ref, vtp_ref, vtc_ref, x_ref, g_ref, w_ref, b_ref,
                     lng_ref, lnb_ref, y_ref, kwin_ref, vtwin_ref, ot_ref, *s_refs):
    i = pl.program_id(0)
    w = SWA_WINDOW
    tq = SWA_TQ
    hd = SWA_HEAD_DIM
    grp = SWA_HEADS // SWA_KV_HEADS
    kwin_ref[0:w, :] = kp_ref[...]
    kwin_ref[w:, :] = kc_ref[...]
    vtwin_ref[:, 0:w] = vtp_ref[...]
    vtwin_ref[:, w:] = vtc_ref[...]

    key = lax.broadcasted_iota(jnp.int32, (2 * w, w), 0)
    qry = lax.broadcasted_iota(jnp.int32, (2 * w, w), 1)
    band = (key > qry) & (key <= qry + w)

    def scores(b, g):
        r0 = b * w
        k_g = kwin_ref[r0:r0 + 2 * w, g * LANES:g * LANES + hd]
        q_g = jnp.concatenate(
            [qt_ref[h * hd:(h + 1) * hd, r0:r0 + w] for h in range(g * grp, (g + 1) * grp)], axis=1)
        return _dot(k_g, q_g)

    tasks = [(b, g) for b in range(tq // w) for g in range(SWA_KV_HEADS)]
    s_refs[0][...] = scores(*tasks[0])
    for n, (b, g) in enumerate(tasks):
        if n + 1 < len(tasks):
            s_refs[(n + 1) % len(s_refs)][...] = scores(*tasks[n + 1])
        st = s_refs[n % len(s_refs)][...]
        r0 = b * w
        heads = range(g * grp, (g + 1) * grp)
        if g == 0:
            first_key = jnp.where(i * (tq // w) + b > 0, 0, w)
            neg = jnp.where(band & (key >= first_key), 0.0, -jnp.inf)
            neg = jnp.concatenate([neg] * grp, axis=1)
        st = st + neg
        sink = jnp.concatenate(
            [jnp.full((1, w), sinks_ref[h] * LOG2E, F32) for h in heads], axis=1)
        m = jnp.maximum(jnp.max(st, axis=0, keepdims=True), sink)
        p = jnp.exp2(st - m)
        denom = jnp.sum(p, axis=0, keepdims=True) + jnp.exp2(sink - m)
        vt_g = vtwin_ref[g * hd:(g + 1) * hd, r0:r0 + 2 * w]
        ot = _dot(vt_g, p.astype(BF16)) / denom
        for c, h in enumerate(heads):
            ot_ref[h * hd:(h + 1) * hd, r0:r0 + w] = ot[:, c * w:(c + 1) * w]

    o = ot_ref[...].T.astype(BF16)
    y = _dot(o, w_ref[...]) + b_ref[...]
    z = DEEPNORM_ALPHA * x_ref[...] + g_ref[...] * y
    y_ref[...] = _layer_norm(z, lng_ref[...], lnb_ref[...])


def _swa_attn(sinks, qt, k, vt, x, gate, w_o, b_o, ln_g, ln_b):
    s, d = x.shape
    tq = SWA_TQ
    w = SWA_WINDOW
    nq = qt.shape[0]
    nk = k.shape[1]
    nv = vt.shape[0]
    r = tq // w
    row = lambda i: (i, 0)
    col = lambda i: (0, i)
    prev_row = lambda i: (jnp.maximum(i * r - 1, 0), 0)
    prev_col = lambda i: (0, jnp.maximum(i * r - 1, 0))
    return pl.pallas_call(
        _swa_attn_kernel,
        out_shape=jax.ShapeDtypeStruct((s, d), F32),
        grid=(s // tq,),
        in_specs=[
            pl.BlockSpec(memory_space=pltpu.SMEM),
            pl.BlockSpec((nq, tq), col),
            pl.BlockSpec((w, nk), prev_row), pl.BlockSpec((tq, nk), row),
            pl.BlockSpec((nv, w), prev_col), pl.BlockSpec((nv, tq), col),
            pl.BlockSpec((tq, d), row),
            _const_spec((1, d)), _const_spec(w_o.shape), _const_spec((1, d)),
            _const_spec((1, d)), _const_spec((1, d)),
        ],
        out_specs=pl.BlockSpec((tq, d), row),
        scratch_shapes=[
            pltpu.VMEM((tq + w, nk), BF16), pltpu.VMEM((nv, tq + w), BF16),
            pltpu.VMEM((nq, tq), F32),
            *[pltpu.VMEM((2 * w, w * SWA_HEADS // SWA_KV_HEADS), F32)] * SWA_SBUFS],
        compiler_params=_params(1),
        name="swa_attn",
    )(sinks, qt, k, k, vt, vt, x, gate, w_o, b_o, ln_g, ln_b)


def _rope_inv(rot_dim):
    return ROPE_THETA ** (-jnp.arange(0, rot_dim, 2, dtype=F32) / rot_dim)


def _rope_tables(positions, rot_dim):
    ang = _rope_inv(rot_dim)[:, None] * positions.astype(F32)[None, :]
    return jnp.cos(ang), jnp.sin(ang)


def _mla_weights(w_in, q_norm, w_q_b, kv_norm, w_kv_b, w_o):
    d = D_MODEL
    w_in_p = jnp.concatenate([w_in, jnp.zeros((d, LANES - MLA_ROPE), F32)], axis=1).astype(BF16)
    wkv = w_kv_b.reshape(MLA_KV_RANK, MLA_HEADS, MLA_NOPE + MLA_V)
    wk = wkv[:, :, :MLA_NOPE].reshape(MLA_KV_RANK, MLA_HEADS * MLA_NOPE).astype(BF16)
    wvt = wkv[:, :, MLA_NOPE:].reshape(MLA_KV_RANK, MLA_HEADS * MLA_V).T.astype(BF16)
    return {
        "w_in": w_in_p, "q_norm": q_norm.reshape(1, -1), "wqt": w_q_b.T.astype(BF16),
        "kv_norm": kv_norm.reshape(1, -1), "wk": wk, "wvt": wvt, "w_o": w_o.astype(BF16),
    }


def _swa_weights(w_qkv, b_qkv, w_o):
    nq = SWA_HEADS * SWA_HEAD_DIM
    nkv = SWA_KV_HEADS * SWA_HEAD_DIM

    def pad_heads(t):
        lead = t.shape[:-1]
        t = t.reshape(lead + (SWA_KV_HEADS, SWA_HEAD_DIM))
        t = jnp.concatenate([t, jnp.zeros_like(t)], axis=-1)
        return t.reshape(lead + (SWA_KV_HEADS * LANES,))

    def relayout(t):
        return jnp.concatenate([t[..., :nq], pad_heads(t[..., nq:nq + nkv]), t[..., nq + nkv:]], axis=-1)

    return {
        "w_qkv_t": relayout(w_qkv).T.astype(BF16),
        "b_qkv_t": relayout(b_qkv).reshape(-1, 1),
        "w_o": w_o.astype(BF16),
    }


def kernel(x, c, positions, ada_w, ada_b, ln_mix_g, ln_mix_b, ln_ffn_g, ln_ffn_b, ffn_w_gate, ffn_w_up, ffn_w_down, mla_w_in, mla_q_norm, mla_w_q_b, mla_kv_norm, mla_w_kv_b, mla_w_o, swa_w_qkv, swa_b_qkv, swa_sinks, swa_w_o, swa_b_o):
    b, s, d = x.shape
    assert (b, s, d) == (1, SEQ, D_MODEL)
    xs = x.reshape(s, d)
    pos = positions.reshape(s)
    mod = _modulation(c, ada_w, ada_b)
    mla_tabs = _rope_tables(pos, MLA_ROPE)
    swa_tabs = _rope_tables(pos, SWA_ROT)
    vec = lambda t: t.reshape(1, d)
    wg, wu, wd = ffn_w_gate.astype(BF16), ffn_w_up.astype(BF16), ffn_w_down.astype(BF16)
    for i in range(DEPTH):
        sh_m, sc_m, g_m, sh_f, sc_f, g_f = (mod[i, k] for k in range(6))
        j = i // 2
        mix = None
        if i % 2 == 0:
            w = _mla_weights(mla_w_in[j], mla_q_norm[j], mla_w_q_b[j], mla_kv_norm[j], mla_w_kv_b[j], mla_w_o[j])
            qt, k, vt = _mla_proj(xs, sc_m, sh_m, w, mla_tabs)
            o = _mla_attn(qt, k, vt)
            mix = (o, w["w_o"], g_m, vec(ln_mix_g[i]), vec(ln_mix_b[i]))
        else:
            w = _swa_weights(swa_w_qkv[j], swa_b_qkv[j], swa_w_o[j])
            qt, k, vt = _swa_proj(xs, sc_m, sh_m, w, swa_tabs)
            xs = _swa_attn(swa_sinks[j], qt, k, vt, xs, g_m, w["w_o"], vec(swa_b_o[j]),
                           vec(ln_mix_g[i]), vec(ln_mix_b[i]))
        xs = _ffn(xs, i, sc_f, sh_f, g_f, wg, wu, wd, vec(ln_ffn_g[i]), vec(ln_ffn_b[i]), mix=mix)
    return xs.reshape(b, s, d)
```
